```python
import jax, jax.numpy as jnp
from jax import lax
import numpy as np

D_MODEL = 2048
BATCH = 4
SEQ = 2048
DEPTH = 1

N_META = 16
HEAD_DIM = 64
N_Q_HEADS = 32
N_KV_HEADS = 4
GQA_GROUP = N_Q_HEADS // N_KV_HEADS
WINDOW = 128
BLOCK = WINDOW
ATTN_WIDTH = N_Q_HEADS * HEAD_DIM
KV_WIDTH = N_KV_HEADS * HEAD_DIM
CONV_WIDTH = D_MODEL
CONV_K = 3
N_GROUPS = 4
EXPERTS_PER_GROUP = 8
N_EXPERTS = N_GROUPS * EXPERTS_PER_GROUP
TOP_K = 2
D_EXPERT = D_MODEL // 4
LN_EPS = 1e-5
ALPHA = (2.0 * DEPTH) ** 0.25
BETA = (8.0 * DEPTH) ** -0.25
NEG_INF = -1e30
IN_SIZES = (ATTN_WIDTH, KV_WIDTH, KV_WIDTH, CONV_WIDTH, CONV_WIDTH, CONV_WIDTH, D_MODEL, D_MODEL)
IN_COLS = sum(IN_SIZES)
IN_SPLITS = tuple(int(v) for v in np.cumsum(IN_SIZES)[:-1])

kernel_name = "hybrid_swa_sink_shortconv_hmoe_deepnorm"


def layer_norm(x, g, b):
    x32 = x.astype(jnp.float32)
    mu = jnp.mean(x32, axis=-1, keepdims=True)
    var = jnp.mean(jnp.square(x32 - mu), axis=-1, keepdims=True)
    y = (x32 - mu) * lax.rsqrt(var + LN_EPS)
    return (y * g.astype(jnp.float32) + b.astype(jnp.float32)).astype(x.dtype)


def softmax_with_sink(logits, sink):
    sink_col = jnp.broadcast_to(sink[:, :, None, None], logits.shape[:-1] + (1,))
    full = jnp.concatenate([logits, sink_col], axis=-1)
    return jax.nn.softmax(full, axis=-1)[..., :-1]


def sliding_window_sink_attention(q, k, v, sinks):
    b, l = q.shape[0], q.shape[1]
    s = l - N_META
    nb = s // BLOCK
    scale = HEAD_DIM ** -0.5
    f32 = jnp.float32
    q = q.reshape(b, l, N_KV_HEADS, GQA_GROUP, HEAD_DIM)
    sink = sinks.astype(f32).reshape(N_KV_HEADS, GQA_GROUP)
    qm, qr = q[:, :N_META], q[:, N_META:]
    km, kr = k[:, :N_META], k[:, N_META:]
    vm, vr = v[:, :N_META], v[:, N_META:]

    sm = jnp.einsum('bqkgd,bskd->bkgqs', qm, km, preferred_element_type=f32) * scale
    causal = jnp.tril(jnp.ones((N_META, N_META), dtype=bool))
    sm = jnp.where(causal, sm, NEG_INF)
    pm = softmax_with_sink(sm, sink).astype(v.dtype)
    om = jnp.einsum('bkgqs,bskd->bqkgd', pm, vm)

    qb = qr.reshape(b, nb, BLOCK, N_KV_HEADS, GQA_GROUP, HEAD_DIM)
    kb = kr.reshape(b, nb, BLOCK, N_KV_HEADS, HEAD_DIM)
    vb = vr.reshape(b, nb, BLOCK, N_KV_HEADS, HEAD_DIM)
    zpad = jnp.zeros_like(kb[:, :1])
    k_band = jnp.concatenate([jnp.concatenate([zpad, kb[:, :-1]], axis=1), kb], axis=2)
    v_band = jnp.concatenate([jnp.concatenate([zpad, vb[:, :-1]], axis=1), vb], axis=2)
    s_meta = jnp.einsum('bnqkgd,bmkd->bnkgqm', qb, km, preferred_element_type=f32) * scale
    s_band = jnp.einsum('bnqkgd,bnskd->bnkgqs', qb, k_band, preferred_element_type=f32) * scale
    qi = jnp.arange(BLOCK)[:, None]
    si = jnp.arange(2 * BLOCK)[None, :]
    rel = qi + BLOCK - si
    blk = jnp.arange(nb)[:, None, None]
    valid = (rel >= 0) & (rel < WINDOW) & ((blk > 0) | (si >= BLOCK))
    s_band = jnp.where(valid[None, :, None, None], s_band, NEG_INF)
    p = softmax_with_sink(jnp.concatenate([s_meta, s_band], axis=-1), sink).astype(v.dtype)
    p_meta, p_band = p[..., :N_META], p[..., N_META:]
    orl = (jnp.einsum('bnkgqm,bmkd->bnqkgd', p_meta, vm)
           + jnp.einsum('bnkgqs,bnskd->bnqkgd', p_band, v_band))
    orl = orl.reshape(b, s, ATTN_WIDTH)
    return jnp.concatenate([om.reshape(b, N_META, ATTN_WIDTH), orl], axis=1)


def causal_depthwise_conv(u, w):
    l = u.shape[1]
    up = jnp.pad(u, ((0, 0), (CONV_K - 1, 0), (0, 0)))
    return sum(w[j] * up[:, j:j + l] for j in range(CONV_K))


def token_mixers(h, w_in, sinks, conv_w, w_branch_attn, w_branch_conv, w_out):
    b, l, _ = h.shape
    proj = h @ w_in
    q, k, v, cb, cc, ch, ga, gc = jnp.split(proj, IN_SPLITS, axis=-1)
    attn = sliding_window_sink_attention(
        q.reshape(b, l, N_Q_HEADS, HEAD_DIM),
        k.reshape(b, l, N_KV_HEADS, HEAD_DIM),
        v.reshape(b, l, N_KV_HEADS, HEAD_DIM), sinks)
    y_attn = attn @ w_branch_attn
    y_conv = (cb * causal_depthwise_conv(cc * ch, conv_w)) @ w_branch_conv
    merged = jax.nn.sigmoid(ga) * y_attn + jax.nn.sigmoid(gc) * y_conv
    return merged @ w_out


def hierarchical_moe(h, rg_w, rg_b, re_w, re_b, w_gate, w_up, w_down):
    n = h.shape[0]
    f32 = jnp.float32
    g_prob = jax.nn.softmax((h @ rg_w + rg_b).astype(f32), axis=-1)
    p_group, g_sel = lax.top_k(g_prob, 1)
    e_logits = (h @ re_w + re_b).astype(f32).reshape(n, N_GROUPS, EXPERTS_PER_GROUP)
    e_in = jnp.take_along_axis(e_logits, g_sel[:, :, None], axis=1)[:, 0]
    top_p, top_i = lax.top_k(jax.nn.softmax(e_in, axis=-1), TOP_K)
    top_w = top_p / jnp.sum(top_p, axis=-1, keepdims=True) * p_group
    e_idx = g_sel * EXPERTS_PER_GROUP + top_i
    combine = jnp.zeros((n, N_EXPERTS), f32).at[jnp.arange(n)[:, None], e_idx].add(top_w)
    combine = combine.astype(h.dtype)
    out = jnp.zeros_like(h)
    for e in range(N_EXPERTS):
        y = (jax.nn.silu(h @ w_gate[e]) * (h @ w_up[e])) @ w_down[e]
        out = out + combine[:, e:e + 1] * y
    return out


def setup_inputs(seed: int = 0) -> dict:
    key = jax.random.key(seed)
    ks = jax.random.split(key, 24)
    nrm = lambda k, shape, s: jax.random.normal(k, shape, jnp.float32) * s
    D = D_MODEL
    return {
        "x": nrm(ks[0], (BATCH, SEQ, D), 1.0),
        "meta_tokens": nrm(ks[1], (N_META, D), 1.0),
        "ln_in_g": 1.0 + nrm(ks[2], (D,), 0.02),
        "ln_in_b": nrm(ks[3], (D,), 0.02),
        "w_in": nrm(ks[4], (DEPTH, D, IN_COLS), D ** -0.5),
        "sinks": nrm(ks[5], (DEPTH, N_Q_HEADS), 1.0),
        "conv_w": nrm(ks[6], (DEPTH, CONV_K, CONV_WIDTH), CONV_K ** -0.5),
        "w_branch_attn": nrm(ks[7], (DEPTH, ATTN_WIDTH, D), ATTN_WIDTH ** -0.5),
        "w_branch_conv": nrm(ks[8], (DEPTH, CONV_WIDTH, D), CONV_WIDTH ** -0.5),
        "w_out": nrm(ks[9], (DEPTH, D, D), BETA * D ** -0.5),
        "ln1_g": 1.0 + nrm(ks[10], (DEPTH, D), 0.02),
        "ln1_b": nrm(ks[11], (DEPTH, D), 0.02),
        "router_group_w": nrm(ks[12], (DEPTH, D, N_GROUPS), D ** -0.5),
        "router_group_b": nrm(ks[13], (DEPTH, N_GROUPS), 0.01),
        "router_expert_w": nrm(ks[14], (DEPTH, D, N_EXPERTS), D ** -0.5),
        "router_expert_b": nrm(ks[15], (DEPTH, N_EXPERTS), 0.01),
        "w_gate": nrm(ks[16], (DEPTH, N_EXPERTS, D, D_EXPERT), D ** -0.5),
        "w_up": nrm(ks[17], (DEPTH, N_EXPERTS, D, D_EXPERT), D ** -0.5),
        "w_down": nrm(ks[18], (DEPTH, N_EXPERTS, D_EXPERT, D), BETA * D_EXPERT ** -0.5),
        "ln2_g": 1.0 + nrm(ks[19], (DEPTH, D), 0.02),
        "ln2_b": nrm(ks[20], (DEPTH, D), 0.02),
    }


def reference(x, meta_tokens, ln_in_g, ln_in_b, w_in, sinks, conv_w, w_branch_attn, w_branch_conv,
              w_out, ln1_g, ln1_b, router_group_w, router_group_b, router_expert_w, router_expert_b,
              w_gate, w_up, w_down, ln2_g, ln2_b):
    b = x.shape[0]
    meta = jnp.broadcast_to(meta_tokens[None].astype(x.dtype), (b, N_META, D_MODEL))
    h = layer_norm(jnp.concatenate([meta, x], axis=1), ln_in_g, ln_in_b)
    for layer in range(DEPTH):
        mix = token_mixers(h, w_in[layer], sinks[layer], conv_w[layer], w_branch_attn[layer],
                           w_branch_conv[layer], w_out[layer])
        h = layer_norm(ALPHA * h + mix, ln1_g[layer], ln1_b[layer])
        f = hierarchical_moe(h.reshape(-1, D_MODEL), router_group_w[layer], router_group_b[layer],
                             router_expert_w[layer], router_expert_b[layer], w_gate[layer],
                             w_up[layer], w_down[layer]).reshape(h.shape)
        h = layer_norm(ALPHA * h + f, ln2_g[layer], ln2_b[layer])
    return h[:, N_META:]
```

```python
import functools

import jax
import jax.numpy as jnp
from jax import lax
from jax.experimental import pallas as pl
from jax.experimental.pallas import tpu as pltpu

D_MODEL = 2048
N_META = 16
HEAD_DIM = 64
N_Q_HEADS = 32
N_KV_HEADS = 4
BLOCK = 128
KV_WIDTH = N_KV_HEADS * HEAD_DIM
CONV_K = 3
N_GROUPS = 4
EXPERTS_PER_GROUP = 8
N_EXPERTS = N_GROUPS * EXPERTS_PER_GROUP
D_EXPERT = D_MODEL // 4
LN_EPS = 1e-5
DEPTH = 1
ALPHA = (2.0 * DEPTH) ** 0.25
NEG_INF = -1e30
Q_SCALE = HEAD_DIM ** -0.5

V7X_VMEM_LIMIT = 56 * 1024 * 1024
LANE = 128
CHUNK = 512
N_CHUNKS = D_MODEL // CHUNK
ROW_TILE = 512
MOE_TILE = 256
TOK_TILE = 256
MAX_ITEMS = 2 * 8192 // MOE_TILE + N_EXPERTS
ROUTE_ROWS = 128
EXPERT_ROW0 = 8

BF16 = jnp.bfloat16
F32 = jnp.float32


def _layer_norm(x, g, b):
    mu = jnp.mean(x, axis=-1, keepdims=True)
    xc = x - mu
    var = jnp.mean(xc * xc, axis=-1, keepdims=True)
    return xc * lax.rsqrt(var + LN_EPS) * g + b


def _dot(a, b):
    return jnp.dot(a, b, preferred_element_type=F32)


def _dot_nt(a, b):
    return lax.dot_general(a, b, (((1,), (1,)), ((), ())), preferred_element_type=F32)


def _params(*sem):
    return pltpu.CompilerParams(dimension_semantics=sem, vmem_limit_bytes=V7X_VMEM_LIMIT)


def _meta_kernel(meta_ref, g_ref, b_ref, wkv_ref, wcc_ref, wch_ref, kv_ref, u_ref):
    hm = _layer_norm(meta_ref[...], g_ref[...], b_ref[...]).astype(BF16)
    kv_ref[...] = _dot(hm, wkv_ref[...]).astype(BF16)
    u_ref[...] = _dot(hm, wcc_ref[...]) * _dot(hm, wch_ref[...])


def _meta_prep(meta, g, b, wkv_dup, wcat):
    n_kv = wkv_dup.shape[1] // N_CHUNKS
    return pl.pallas_call(
        _meta_kernel,
        grid=(N_CHUNKS,),
        in_specs=[
            pl.BlockSpec((N_META, D_MODEL), lambda j: (0, 0)),
            pl.BlockSpec((1, D_MODEL), lambda j: (0, 0)),
            pl.BlockSpec((1, D_MODEL), lambda j: (0, 0)),
            pl.BlockSpec((D_MODEL, n_kv), lambda j: (0, j)),
            pl.BlockSpec((D_MODEL, CHUNK), lambda j: (0, N_CHUNKS + j)),
            pl.BlockSpec((D_MODEL, CHUNK), lambda j: (0, 2 * N_CHUNKS + j)),
        ],
        out_specs=[
            pl.BlockSpec((N_META, n_kv), lambda j: (0, j)),
            pl.BlockSpec((N_META, CHUNK), lambda j: (0, j)),
        ],
        out_shape=[
            jax.ShapeDtypeStruct((N_META, wkv_dup.shape[1]), BF16),
            jax.ShapeDtypeStruct((N_META, D_MODEL), F32),
        ],
        compiler_params=_params("arbitrary"),
        name="meta_prep",
    )(meta, g, b, wkv_dup, wcat, wcat)


def _qkv_kernel(x_ref, g_ref, b_ref, w_ref, q_ref, k_ref, v_ref):
    hb = _layer_norm(x_ref[...], g_ref[...], b_ref[...]).astype(BF16)
    kw = 2 * KV_WIDTH
    q_ref[...] = (_dot(hb, w_ref[:, :D_MODEL]) * Q_SCALE).astype(BF16)
    k_ref[...] = _dot(hb, w_ref[:, D_MODEL:D_MODEL + kw]).astype(BF16)
    v_ref[...] = _dot(hb, w_ref[:, D_MODEL + kw:]).astype(BF16)


def _ln_qkv(x2d, g, b, wqkv):
    n = x2d.shape[0]
    kw = 2 * KV_WIDTH
    return pl.pallas_call(
        _qkv_kernel,
        grid=(n // ROW_TILE,),
        in_specs=[
            pl.BlockSpec((ROW_TILE, D_MODEL), lambda i: (i, 0)),
            pl.BlockSpec((1, D_MODEL), lambda i: (0, 0)),
            pl.BlockSpec((1, D_MODEL), lambda i: (0, 0)),
            pl.BlockSpec(wqkv.shape, lambda i: (0, 0)),
        ],
        out_specs=[
            pl.BlockSpec((ROW_TILE, D_MODEL), lambda i: (i, 0)),
            pl.BlockSpec((ROW_TILE, kw), lambda i: (i, 0)),
            pl.BlockSpec((ROW_TILE, kw), lambda i: (i, 0)),
        ],
        out_shape=[
            jax.ShapeDtypeStruct((n, D_MODEL), BF16),
            jax.ShapeDtypeStruct((n, kw), BF16),
            jax.ShapeDtypeStruct((n, kw), BF16),
        ],
        compiler_params=_params("arbitrary"),
        name="ln_qkv",
    )(x2d, g, b, wqkv)


PAIRS = N_Q_HEADS // N_KV_HEADS // 2
QROWS = PAIRS * BLOCK


def _attn_kernel(sink_ref, q_ref, kp_ref, kc_ref, vp_ref, vc_ref, kvm_ref, o_ref):
    blk = pl.program_id(1)
    lane = lax.broadcasted_iota(jnp.int32, (2 * BLOCK, LANE), 1)
    lane_m = lax.broadcasted_iota(jnp.int32, (N_META, LANE), 1)
    qi = lax.broadcasted_iota(jnp.int32, (QROWS, 2 * BLOCK), 0) % BLOCK
    kj = lax.broadcasted_iota(jnp.int32, (QROWS, 2 * BLOCK), 1)
    prev_shift = jnp.where(blk > 0, 0, BLOCK)
    valid = ((kj < BLOCK) & (kj > qi + prev_shift)) | ((kj >= BLOCK) & (kj - BLOCK <= qi))
    rowpair = lax.broadcasted_iota(jnp.int32, (QROWS, 1), 0) // BLOCK
    zero = jnp.zeros((), BF16)

    for g in range(N_KV_HEADS):
        gs = slice(g * LANE, (g + 1) * LANE)
        qs = jnp.concatenate(
            [q_ref[:, (g * PAIRS + p) * LANE:(g * PAIRS + p + 1) * LANE] for p in range(PAIRS)], axis=0)
        kband = jnp.concatenate([kp_ref[:, gs], kc_ref[:, gs]], axis=0)
        vband = jnp.concatenate([vp_ref[:, gs], vc_ref[:, gs]], axis=0)
        km = kvm_ref[:, gs]
        vm = kvm_ref[:, N_KV_HEADS * LANE + g * LANE:N_KV_HEADS * LANE + (g + 1) * LANE]
        acc = None
        for half in range(2):
            keep = (lane >= HEAD_DIM) if half else (lane < HEAD_DIM)
            keep_m = (lane_m >= HEAD_DIM) if half else (lane_m < HEAD_DIM)
            sb = _dot_nt(qs, jnp.where(keep, kband, zero))
            sm = _dot_nt(qs, jnp.where(keep_m, km, zero))
            sb = jnp.where(valid, sb, NEG_INF)
            sink = jnp.zeros((QROWS, 1), F32)
            for p in range(PAIRS):
                sink = jnp.where(rowpair == p, sink_ref[g * 2 * PAIRS + 2 * p + half], sink)
            m = jnp.maximum(jnp.maximum(jnp.max(sb, axis=-1, keepdims=True),
                                        jnp.max(sm, axis=-1, keepdims=True)), sink)
            eb = jnp.exp(sb - m)
            em = jnp.exp(sm - m)
            den = (jnp.sum(eb, axis=-1, keepdims=True) + jnp.sum(em, axis=-1, keepdims=True)
                   + jnp.exp(sink - m))
            pb = (eb / den).astype(BF16)
            pm = (em / den).astype(BF16)
            part = _dot(pm, jnp.where(keep_m, vm, zero)) + _dot(pb, jnp.where(keep, vband, zero))
            acc = part if acc is None else acc + part
        for p in range(PAIRS):
            col = (g * PAIRS + p) * LANE
            o_ref[:, col:col + LANE] = acc[p * BLOCK:(p + 1) * BLOCK].astype(BF16)


def _swa_attn(sinks, q, kd, vd, kvm, batch, nblk):
    n = q.shape[0]
    kw = 2 * KV_WIDTH
    cur = lambda b, i: (b * nblk + i, 0)
    prev = lambda b, i: (b * nblk + jnp.maximum(i - 1, 0), 0)
    return pl.pallas_call(
        _attn_kernel,
        grid=(batch, nblk),
        in_specs=[
            pl.BlockSpec(memory_space=pltpu.SMEM),
            pl.BlockSpec((BLOCK, D_MODEL), cur),
            pl.BlockSpec((BLOCK, kw), prev),
            pl.BlockSpec((BLOCK, kw), cur),
            pl.BlockSpec((BLOCK, kw), prev),
            pl.BlockSpec((BLOCK, kw), cur),
            pl.BlockSpec((N_META, 2 * kw), lambda b, i: (0, 0)),
        ],
        out_specs=pl.BlockSpec((BLOCK, D_MODEL), cur),
        out_shape=jax.ShapeDtypeStruct((n, D_MODEL), BF16),
        compiler_params=_params("arbitrary", "arbitrary"),
        name="swa_attn",
    )(sinks, q, kd, kd, vd, vd, kvm)


def _tail_kernel(x_ref, a_ref, g0_ref, b0_ref, um_ref, cw_ref, w0_ref, w1_ref, w2_ref, wr_ref,
                 g1_ref, b1_ref, rw_ref, rb_ref, h1_ref, lg_ref,
                 hb_ref, uext_ref, yconv_ref, mix_ref, *, tiles_per_seq):
    i = pl.program_id(0)
    s = pl.program_id(1)
    tm = ROW_TILE

    @pl.when(s == 0)
    def _():
        hb_ref[...] = _layer_norm(x_ref[...], g0_ref[...], b0_ref[...]).astype(BF16)

    @pl.when(s < N_CHUNKS)
    def _():
        hb = hb_ref[...]
        cb = _dot(hb, w0_ref[...])
        u = _dot(hb, w1_ref[...]) * _dot(hb, w2_ref[...])
        ue = uext_ref.at[s]

        @pl.when(i % tiles_per_seq == 0)
        def _():
            ue[0:8, :] = um_ref[N_META - 8:N_META, :]

        ue[8:tm + 8, :] = u
        conv = (cw_ref[0:1, :] * ue[6:tm + 6, :] + cw_ref[1:2, :] * ue[7:tm + 7, :]
                + cw_ref[2:3, :] * u)
        ue[0:8, :] = u[tm - 8:tm, :]
        y = _dot((cb * conv).astype(BF16), wr_ref[...])

        @pl.when(s == 0)
        def _():
            for c in range(N_CHUNKS):
                yconv_ref[c] = y[:, c * CHUNK:(c + 1) * CHUNK]

        @pl.when(s > 0)
        def _():
            for c in range(N_CHUNKS):
                yconv_ref[c] += y[:, c * CHUNK:(c + 1) * CHUNK]

    @pl.when(s >= N_CHUNKS)
    def _():
        hb = hb_ref[...]
        ya = _dot(a_ref[...], w0_ref[...])
        ga = _dot(hb, w1_ref[...])
        gc = _dot(hb, w2_ref[...])
        merged = jax.nn.sigmoid(ga) * ya + jax.nn.sigmoid(gc) * yconv_ref[s - N_CHUNKS]
        y = _dot(merged.astype(BF16), wr_ref[...])

        @pl.when(s == N_CHUNKS)
        def _():
            mix_ref[...] = y

        @pl.when(s > N_CHUNKS)
        def _():
            mix_ref[...] += y

    @pl.when(s == 2 * N_CHUNKS - 1)
    def _():
        h = _layer_norm(x_ref[...], g0_ref[...], b0_ref[...])
        h1 = _layer_norm(ALPHA * h + mix_ref[...], g1_ref[...], b1_ref[...])
        h1_ref[...] = h1
        lg_ref[...] = _dot_nt(rw_ref[...], h1.astype(BF16)) + rb_ref[...][:, 0:1]


def _mixer_tail(x2d, attn, g0, b0, um, cw, wcat, wrows, g1, b1, rw, rb, seq):
    n = x2d.shape[0]
    nc = N_CHUNKS

    def col_map(first, second):
        return lambda i, s: (0, jnp.where(s < nc, first + s, second + s - nc))

    lo = lambda i, s: (0, jnp.minimum(s, nc - 1))
    const = lambda i, s: (0, 0)
    row = lambda i, s: (i, 0)
    return pl.pallas_call(
        functools.partial(_tail_kernel, tiles_per_seq=seq // ROW_TILE),
        grid=(n // ROW_TILE, 2 * nc),
        in_specs=[
            pl.BlockSpec((ROW_TILE, D_MODEL), row),
            pl.BlockSpec((ROW_TILE, D_MODEL), row),
            pl.BlockSpec((1, D_MODEL), const),
            pl.BlockSpec((1, D_MODEL), const),
            pl.BlockSpec((N_META, CHUNK), lo),
            pl.BlockSpec((CONV_K, CHUNK), lo),
            pl.BlockSpec((D_MODEL, CHUNK), col_map(0, 5 * nc)),
            pl.BlockSpec((D_MODEL, CHUNK), col_map(nc, 3 * nc)),
            pl.BlockSpec((D_MODEL, CHUNK), col_map(2 * nc, 4 * nc)),
            pl.BlockSpec((CHUNK, D_MODEL), lambda i, s: (s, 0)),
            pl.BlockSpec((1, D_MODEL), const),
            pl.BlockSpec((1, D_MODEL), const),
            pl.BlockSpec((ROUTE_ROWS, D_MODEL), const),
            pl.BlockSpec((ROUTE_ROWS, LANE), const),
        ],
        out_specs=[
            pl.BlockSpec((ROW_TILE, D_MODEL), row),
            pl.BlockSpec((ROUTE_ROWS, ROW_TILE), lambda i, s: (0, i)),
        ],
        out_shape=[
            jax.ShapeDtypeStruct((n, D_MODEL), F32),
            jax.ShapeDtypeStruct((ROUTE_ROWS, n), F32),
        ],
        scratch_shapes=[
            pltpu.VMEM((ROW_TILE, D_MODEL), BF16),
            pltpu.VMEM((nc, ROW_TILE + 8, CHUNK), F32),
            pltpu.VMEM((nc, ROW_TILE, CHUNK), F32),
            pltpu.VMEM((ROW_TILE, D_MODEL), F32),
        ],
        compiler_params=_params("arbitrary", "arbitrary"),
        name="mixer_tail",
    )(x2d, attn, g0, b0, um, cw, wcat, wcat, wcat, wrows, g1, b1, rw, rb)


RCHUNK = 1024
PBLK = 256


def _route_kernel(lg_ref, e_ref, w_ref, pos_ref, cnt_ref, rank_ref):
    n = lg_ref.shape[1]
    epg = EXPERTS_PER_GROUP

    def route_chunk(c, carry):
        sl = pl.ds(pl.multiple_of(c * RCHUNK, RCHUNK), RCHUNK)
        gl = [lg_ref[gi:gi + 1, sl] for gi in range(N_GROUPS)]
        gmax = functools.reduce(jnp.maximum, gl)
        ge = [jnp.exp(v - gmax) for v in gl]
        gsum = functools.reduce(jnp.add, ge)
        gp = [v / gsum for v in ge]
        p_group = functools.reduce(jnp.maximum, gp)
        gsel = jnp.full_like(p_group, float(N_GROUPS - 1))
        for gi in range(N_GROUPS - 2, -1, -1):
            gsel = jnp.where(gp[gi] >= p_group, float(gi), gsel)
        e_in = lg_ref[EXPERT_ROW0 + (N_GROUPS - 1) * epg:EXPERT_ROW0 + N_GROUPS * epg, sl]
        for gi in range(N_GROUPS - 2, -1, -1):
            e_in = jnp.where(gsel == float(gi),
                             lg_ref[EXPERT_ROW0 + gi * epg:EXPERT_ROW0 + (gi + 1) * epg, sl], e_in)
        ee = jnp.exp(e_in - jnp.max(e_in, axis=0, keepdims=True))
        pe = ee / jnp.sum(ee, axis=0, keepdims=True)
        ridx = lax.broadcasted_iota(jnp.int32, pe.shape, 0).astype(F32)
        p0 = jnp.max(pe, axis=0, keepdims=True)
        i0 = jnp.min(jnp.where(pe >= p0, ridx, float(epg)), axis=0, keepdims=True)
        pe2 = jnp.where(ridx == i0, -1.0, pe)
        p1 = jnp.max(pe2, axis=0, keepdims=True)
        i1 = jnp.min(jnp.where(pe2 >= p1, ridx, float(epg)), axis=0, keepdims=True)
        den = p0 + p1
        e_ref[0:1, sl] = (gsel * epg + i0).astype(jnp.int32)
        e_ref[1:2, sl] = (gsel * epg + i1).astype(jnp.int32)
        w_ref[0:1, sl] = p0 / den * p_group
        w_ref[1:2, sl] = p1 / den * p_group
        return carry

    lax.fori_loop(0, n // RCHUNK, route_chunk, 0)

    eid = lax.broadcasted_iota(jnp.int32, (N_EXPERTS, PBLK), 0)
    before = jnp.where(lax.broadcasted_iota(jnp.int32, (PBLK, PBLK), 0)
                       < lax.broadcasted_iota(jnp.int32, (PBLK, PBLK), 1), 1.0, 0.0).astype(BF16)
    ones = jnp.ones((PBLK, LANE), BF16)
    nblk = n // PBLK

    def onehot(k, blk):
        sl = pl.ds(pl.multiple_of(blk * PBLK, PBLK), PBLK)
        return sl, eid == e_ref[pl.ds(k, 1), sl]

    run = jnp.zeros((N_EXPERTS, LANE), F32)
    for k in range(2):
        def count_block(blk, run, k=k):
            sl, hit = onehot(k, blk)
            hit_bf = jnp.where(hit, 1.0, 0.0).astype(BF16)
            prior = _dot(hit_bf, before) + jnp.concatenate([run] * (PBLK // LANE), axis=1)
            rank_ref[pl.ds(k, 1), sl] = jnp.sum(jnp.where(hit, prior, 0.0), axis=0, keepdims=True)
            return run + _dot(hit_bf, ones)
        run = lax.fori_loop(0, nblk, count_block, run)

    cnt_ref[...] = run.astype(jnp.int32)
    hi = jnp.floor(run * (1.0 / LANE))
    lo = run - hi * LANE
    below = jnp.where(lax.broadcasted_iota(jnp.int32, (N_EXPERTS, N_EXPERTS), 1)
                      < lax.broadcasted_iota(jnp.int32, (N_EXPERTS, N_EXPERTS), 0), 1.0, 0.0).astype(BF16)
    off = _dot(below, hi.astype(BF16)) * LANE + _dot(below, lo.astype(BF16))
    off = jnp.concatenate([off] * (PBLK // LANE), axis=1)

    for k in range(2):
        def place_block(blk, carry, k=k):
            sl, hit = onehot(k, blk)
            base = jnp.sum(jnp.where(hit, off, 0.0), axis=0, keepdims=True)
            pos_ref[pl.ds(k, 1), sl] = (base + rank_ref[pl.ds(k, 1), sl]).astype(jnp.int32)
            return carry
        lax.fori_loop(0, nblk, place_block, 0)


def _route(logits_t):
    n = logits_t.shape[1]
    return pl.pallas_call(
        _route_kernel,
        out_shape=[
            jax.ShapeDtypeStruct((2, n), jnp.int32),
            jax.ShapeDtypeStruct((2, n), F32),
            jax.ShapeDtypeStruct((2, n), jnp.int32),
            jax.ShapeDtypeStruct((N_EXPERTS, LANE), jnp.int32),
        ],
        scratch_shapes=[pltpu.VMEM((2, n), F32)],
        compiler_params=pltpu.CompilerParams(vmem_limit_bytes=V7X_VMEM_LIMIT),
        name="route",
    )(logits_t)


def _row_copy(src_ref, src_row, dst_ref, dst_row, sem):
    return pltpu.make_async_copy(src_ref.at[pl.ds(src_row, 1)], dst_ref.at[pl.ds(dst_row, 1)], sem)


def _dispatch_kernel(pos_ref, h_ref, xs_ref, sem):
    def issue(r, c):
        for k in range(2):
            _row_copy(h_ref, r, xs_ref, pos_ref[0, k, r], sem).start()
        return c

    def drain(r, c):
        for k in range(2):
            _row_copy(h_ref, 0, xs_ref, 0, sem).wait()
        return c

    lax.fori_loop(0, TOK_TILE, issue, 0)
    lax.fori_loop(0, TOK_TILE, drain, 0)


def _dispatch(pos_tiles, h1):
    n = h1.shape[0]
    return pl.pallas_call(
        _dispatch_kernel,
        grid=(n // TOK_TILE,),
        in_specs=[
            pl.BlockSpec((1, 2, TOK_TILE), lambda i: (i, 0, 0), memory_space=pltpu.SMEM),
            pl.BlockSpec((TOK_TILE, D_MODEL), lambda i: (i, 0)),
        ],
        out_specs=pl.BlockSpec(memory_space=pl.ANY),
        out_shape=jax.ShapeDtypeStruct((2 * n, D_MODEL), F32),
        scratch_shapes=[pltpu.SemaphoreType.DMA],
        compiler_params=_params("arbitrary"),
        name="dispatch",
    )(pos_tiles, h1)


def _moe_kernel(tile_ref, exp_ref, lo_ref, n_ref, xs_ref, wg_ref, wu_ref, wd_ref, y_ref):
    i = pl.program_id(0)

    @pl.when(i < n_ref[0])
    def _():
        x = xs_ref[...].astype(BF16)
        g = _dot(x, wg_ref[0].astype(BF16))
        u = _dot(x, wu_ref[0].astype(BF16))
        a = (g * jax.nn.sigmoid(g) * u).astype(BF16)
        y = _dot(a, wd_ref[0].astype(BF16))
        lo = lo_ref[i]

        @pl.when(lo == 0)
        def _():
            y_ref[...] = y

        @pl.when(lo > 0)
        def _():
            row = lax.broadcasted_iota(jnp.int32, (MOE_TILE, 1), 0)
            y_ref[...] = jnp.where(row >= lo, y, y_ref[...])


def _moe_experts(item_tile, item_exp, item_lo, n_items, xs, wg, wu, wd):
    rows = xs.shape[0]
    grid_spec = pltpu.PrefetchScalarGridSpec(
        num_scalar_prefetch=4,
        grid=(MAX_ITEMS,),
        in_specs=[
            pl.BlockSpec((MOE_TILE, D_MODEL), lambda i, t, e, l, n: (t[i], 0)),
            pl.BlockSpec((1, D_MODEL, D_EXPERT), lambda i, t, e, l, n: (e[i], 0, 0)),
            pl.BlockSpec((1, D_MODEL, D_EXPERT), lambda i, t, e, l, n: (e[i], 0, 0)),
            pl.BlockSpec((1, D_EXPERT, D_MODEL), lambda i, t, e, l, n: (e[i], 0, 0)),
        ],
        out_specs=pl.BlockSpec((MOE_TILE, D_MODEL), lambda i, t, e, l, n: (t[i], 0)),
    )
    return pl.pallas_call(
        _moe_kernel,
        grid_spec=grid_spec,
        out_shape=jax.ShapeDtypeStruct((rows, D_MODEL), F32),
        compiler_params=_params("arbitrary"),
        name="moe_experts",
    )(item_tile, item_exp, item_lo, n_items, xs, wg, wu, wd)


def _work_items(counts):
    off = jnp.cumsum(counts) - counts
    end = off + counts
    first = off // MOE_TILE
    n_e = jnp.where(counts > 0, (end - 1) // MOE_TILE - first + 1, 0)
    item_end = jnp.cumsum(n_e)
    n_items = item_end[-1]
    idx = jnp.minimum(jnp.arange(MAX_ITEMS, dtype=jnp.int32), n_items - 1)
    exp = jnp.sum((item_end[None, :] <= idx[:, None]).astype(jnp.int32), axis=1)
    tile = first[exp] + idx - (item_end[exp] - n_e[exp])
    lo = jnp.maximum(off[exp] - tile * MOE_TILE, 0)
    return tile.astype(jnp.int32), exp, lo.astype(jnp.int32), n_items.reshape(1).astype(jnp.int32)


def _combine_kernel(pos_ref, h_ref, wt_ref, g_ref, b_ref, y_ref, o_ref, buf_ref, sem):
    def issue(r, c):
        for k in range(2):
            _row_copy(y_ref, pos_ref[0, k, r], buf_ref.at[k], r, sem).start()
        return c

    def drain(r, c):
        for k in range(2):
            _row_copy(y_ref, 0, buf_ref.at[k], 0, sem).wait()
        return c

    lax.fori_loop(0, TOK_TILE, issue, 0)
    lax.fori_loop(0, TOK_TILE, drain, 0)
    f = wt_ref[:, 0:1] * buf_ref[0] + wt_ref[:, 1:2] * buf_ref[1]
    o_ref[...] = _layer_norm(ALPHA * h_ref[...] + f, g_ref[...], b_ref[...])


def _combine_ln(pos_tiles, h1, wt, g, b, y):
    n = h1.shape[0]
    return pl.pallas_call(
        _combine_kernel,
        grid=(n // TOK_TILE,),
        in_specs=[
            pl.BlockSpec((1, 2, TOK_TILE), lambda i: (i, 0, 0), memory_space=pltpu.SMEM),
            pl.BlockSpec((TOK_TILE, D_MODEL), lambda i: (i, 0)),
            pl.BlockSpec((TOK_TILE, 2), lambda i: (i, 0)),
            pl.BlockSpec((1, D_MODEL), lambda i: (0, 0)),
            pl.BlockSpec((1, D_MODEL), lambda i: (0, 0)),
            pl.BlockSpec(memory_space=pl.ANY),
        ],
        out_specs=pl.BlockSpec((TOK_TILE, D_MODEL), lambda i: (i, 0)),
        out_shape=jax.ShapeDtypeStruct((n, D_MODEL), F32),
        scratch_shapes=[pltpu.VMEM((2, TOK_TILE, D_MODEL), F32), pltpu.SemaphoreType.DMA],
        compiler_params=_params("arbitrary"),
        name="combine_ln",
    )(pos_tiles, h1, wt, g, b, y)


def _dup_heads(w):
    d = w.shape[0]
    w = w.reshape(d, N_KV_HEADS, 1, HEAD_DIM)
    return jnp.broadcast_to(w, (d, N_KV_HEADS, 2, HEAD_DIM)).reshape(d, 2 * KV_WIDTH)


def kernel(x, meta_tokens, ln_in_g, ln_in_b, w_in, sinks, conv_w, w_branch_attn, w_branch_conv, w_out,
           ln1_g, ln1_b, router_group_w, router_group_b, router_expert_w, router_expert_b,
           w_gate, w_up, w_down, ln2_g, ln2_b):
    batch, seq, d = x.shape
    n = batch * seq
    assert d == D_MODEL and seq % ROW_TILE == 0 and w_in.shape[0] == DEPTH == 1
    row = lambda v: v.reshape(1, -1).astype(F32)
    x2d = x.reshape(n, d)
    g0, b0 = row(ln_in_g), row(ln_in_b)

    win = w_in[0]
    kcol, vcol, ccol = D_MODEL, D_MODEL + KV_WIDTH, D_MODEL + 2 * KV_WIDTH
    wk_dup = _dup_heads(win[:, kcol:vcol])
    wv_dup = _dup_heads(win[:, vcol:ccol])
    wqkv = jnp.concatenate([win[:, :D_MODEL], wk_dup, wv_dup], axis=1).astype(BF16)
    wkv_dup = jnp.concatenate([wk_dup, wv_dup], axis=1).astype(BF16)
    wcat = jnp.concatenate([win[:, ccol:], w_branch_attn[0]], axis=1).astype(BF16)
    wrows = jnp.concatenate([w_branch_conv[0], w_out[0]], axis=0).astype(BF16)
    rw = jnp.zeros((ROUTE_ROWS, d), F32)
    rw = rw.at[:N_GROUPS].set(router_group_w[0].T).at[EXPERT_ROW0:EXPERT_ROW0 + N_EXPERTS].set(router_expert_w[0].T)
    rb = jnp.zeros((ROUTE_ROWS,), F32)
    rb = rb.at[:N_GROUPS].set(router_group_b[0]).at[EXPERT_ROW0:EXPERT_ROW0 + N_EXPERTS].set(router_expert_b[0])
    rb = jnp.broadcast_to(rb[:, None], (ROUTE_ROWS, LANE))

    kvm, um = _meta_prep(meta_tokens.astype(F32), g0, b0, wkv_dup, wcat)
    q, kd, vd = _ln_qkv(x2d, g0, b0, wqkv)
    attn = _swa_attn(sinks[0].astype(F32), q, kd, vd, kvm, batch, seq // BLOCK)
    h1, logits_t = _mixer_tail(x2d, attn, g0, b0, um, conv_w[0], wcat, wrows,
                               row(ln1_g[0]), row(ln1_b[0]), rw.astype(BF16), rb, seq)
    _, top_w, pos, counts = _route(logits_t)
    pos_tiles = pos.reshape(2, n // TOK_TILE, TOK_TILE).transpose(1, 0, 2)
    xs = _dispatch(pos_tiles, h1)
    y = _moe_experts(*_work_items(counts[:, 0]), xs, w_gate[0], w_up[0], w_down[0])
    out = _combine_ln(pos_tiles, h1, top_w.T, row(ln2_g[0]), row(ln2_b[0]), y)
    return out.reshape(batch, seq, d)
```

```python
import functools

import jax
import jax.numpy as jnp
from jax import lax
from jax.experimental import pallas as pl
from jax.experimental.pallas import tpu as pltpu

D_MODEL = 2048
N_META = 16
HEAD_DIM = 64
N_Q_HEADS = 32
N_KV_HEADS = 4
BLOCK = 128
KV_WIDTH = N_KV_HEADS * HEAD_DIM
CONV_K = 3
N_GROUPS = 4
EXPERTS_PER_GROUP = 8
N_EXPERTS = N_GROUPS * EXPERTS_PER_GROUP
D_EXPERT = D_MODEL // 4
LN_EPS = 1e-5
DEPTH = 1
ALPHA = (2.0 * DEPTH) ** 0.25
NEG_INF = -1e30
Q_SCALE = HEAD_DIM ** -0.5

V7X_VMEM_LIMIT = 56 * 1024 * 1024
LANE = 128
CHUNK = 512
N_CHUNKS = D_MODEL // CHUNK
ROW_TILE = 512
MOE_TILE = 256
TOK_TILE = 256
MAX_ITEMS = 2 * 8192 // MOE_TILE + N_EXPERTS
ROUTE_ROWS = 128
EXPERT_ROW0 = 8

BF16 = jnp.bfloat16
F32 = jnp.float32


def _layer_norm(x, g, b):
    mu = jnp.mean(x, axis=-1, keepdims=True)
    xc = x - mu
    var = jnp.mean(xc * xc, axis=-1, keepdims=True)
    return xc * lax.rsqrt(var + LN_EPS) * g + b


def _dot(a, b):
    return jnp.dot(a, b, preferred_element_type=F32)


def _dot_nt(a, b):
    return lax.dot_general(a, b, (((1,), (1,)), ((), ())), preferred_element_type=F32)


def _params(*sem):
    return pltpu.CompilerParams(dimension_semantics=sem, vmem_limit_bytes=V7X_VMEM_LIMIT)


W_B, W_C, W_H, W_GA, W_GC, W_BA, W_BC, W_OUT, W_Q = (k * N_CHUNKS for k in range(9))
N_WBLOCKS = 9 * N_CHUNKS
IN_FIRST_MIX = (D_MODEL + 2 * KV_WIDTH) // CHUNK


def _pack_kernel(win_ref, ba_ref, bc_ref, out_ref, o_ref):
    j = pl.program_id(0)

    @pl.when((j < W_BA) | (j >= W_Q))
    def _():
        o_ref[...] = win_ref[...].astype(BF16)

    @pl.when((j >= W_BA) & (j < W_BC))
    def _():
        o_ref[...] = ba_ref[...].astype(BF16)

    @pl.when((j >= W_BC) & (j < W_OUT))
    def _():
        o_ref[...] = bc_ref[...].astype(BF16)

    @pl.when((j >= W_OUT) & (j < W_Q))
    def _():
        o_ref[...] = out_ref[...].astype(BF16)


def _pack_weights(win, wba, wbc, wout):
    held = lambda first: (lambda j: (0, jnp.clip(j - first, 0, N_CHUNKS - 1)))
    win_map = lambda j: (0, jnp.where(j >= W_Q, j - W_Q, jnp.minimum(j, W_BA - 1) + IN_FIRST_MIX))
    return pl.pallas_call(
        _pack_kernel,
        grid=(N_WBLOCKS,),
        in_specs=[
            pl.BlockSpec((D_MODEL, CHUNK), win_map),
            pl.BlockSpec((D_MODEL, CHUNK), held(W_BA)),
            pl.BlockSpec((D_MODEL, CHUNK), held(W_BC)),
            pl.BlockSpec((D_MODEL, CHUNK), held(W_OUT)),
        ],
        out_specs=pl.BlockSpec((D_MODEL, CHUNK), lambda j: (0, j)),
        out_shape=jax.ShapeDtypeStruct((D_MODEL, N_WBLOCKS * CHUNK), BF16),
        compiler_params=_params("arbitrary"),
        name="pack_weights",
    )(win, wba, wbc, wout)


def _meta_kernel(meta_ref, g_ref, b_ref, wkv_ref, wcc_ref, wch_ref, kv_ref, u_ref):
    hm = _layer_norm(meta_ref[...], g_ref[...], b_ref[...]).astype(BF16)
    kv_ref[...] = _dot(hm, wkv_ref[...]).astype(BF16)
    u_ref[...] = _dot(hm, wcc_ref[...]) * _dot(hm, wch_ref[...])


def _meta_prep(meta, g, b, wkv_dup, wall):
    n_kv = wkv_dup.shape[1] // N_CHUNKS
    return pl.pallas_call(
        _meta_kernel,
        grid=(N_CHUNKS,),
        in_specs=[
            pl.BlockSpec((N_META, D_MODEL), lambda j: (0, 0)),
            pl.BlockSpec((1, D_MODEL), lambda j: (0, 0)),
            pl.BlockSpec((1, D_MODEL), lambda j: (0, 0)),
            pl.BlockSpec((D_MODEL, n_kv), lambda j: (0, j)),
            pl.BlockSpec((D_MODEL, CHUNK), lambda j: (0, W_C + j)),
            pl.BlockSpec((D_MODEL, CHUNK), lambda j: (0, W_H + j)),
        ],
        out_specs=[
            pl.BlockSpec((N_META, n_kv), lambda j: (0, j)),
            pl.BlockSpec((N_META, CHUNK), lambda j: (0, j)),
        ],
        out_shape=[
            jax.ShapeDtypeStruct((N_META, wkv_dup.shape[1]), BF16),
            jax.ShapeDtypeStruct((N_META, D_MODEL), F32),
        ],
        compiler_params=_params("arbitrary"),
        name="meta_prep",
    )(meta, g, b, wkv_dup, wall, wall)


def _qkv_kernel(x_ref, g_ref, b_ref, wq_ref, wkv_ref, hb_ref, q_ref, k_ref, v_ref):
    hb = _layer_norm(x_ref[...], g_ref[...], b_ref[...]).astype(BF16)
    kw = 2 * KV_WIDTH
    hb_ref[...] = hb
    q_ref[...] = (_dot(hb, wq_ref[...]) * Q_SCALE).astype(BF16)
    k_ref[...] = _dot(hb, wkv_ref[:, :kw]).astype(BF16)
    v_ref[...] = _dot(hb, wkv_ref[:, kw:]).astype(BF16)


def _ln_qkv(x2d, g, b, wall, wkv_dup):
    n = x2d.shape[0]
    kw = 2 * KV_WIDTH
    return pl.pallas_call(
        _qkv_kernel,
        grid=(n // ROW_TILE,),
        in_specs=[
            pl.BlockSpec((ROW_TILE, D_MODEL), lambda i: (i, 0)),
            pl.BlockSpec((1, D_MODEL), lambda i: (0, 0)),
            pl.BlockSpec((1, D_MODEL), lambda i: (0, 0)),
            pl.BlockSpec((D_MODEL, D_MODEL), lambda i: (0, W_Q // N_CHUNKS)),
            pl.BlockSpec(wkv_dup.shape, lambda i: (0, 0)),
        ],
        out_specs=[
            pl.BlockSpec((ROW_TILE, D_MODEL), lambda i: (i, 0)),
            pl.BlockSpec((ROW_TILE, D_MODEL), lambda i: (i, 0)),
            pl.BlockSpec((ROW_TILE, kw), lambda i: (i, 0)),
            pl.BlockSpec((ROW_TILE, kw), lambda i: (i, 0)),
        ],
        out_shape=[
            jax.ShapeDtypeStruct((n, D_MODEL), BF16),
            jax.ShapeDtypeStruct((n, D_MODEL), BF16),
            jax.ShapeDtypeStruct((n, kw), BF16),
            jax.ShapeDtypeStruct((n, kw), BF16),
        ],
        compiler_params=_params("arbitrary"),
        name="ln_qkv",
    )(x2d, g, b, wall, wkv_dup)


PAIRS = N_Q_HEADS // N_KV_HEADS // 2
QROWS = PAIRS * BLOCK


def _attn_kernel(sink_ref, q_ref, kp_ref, kc_ref, vp_ref, vc_ref, kvm_ref, o_ref):
    blk = pl.program_id(1)
    lane = lax.broadcasted_iota(jnp.int32, (2 * BLOCK, LANE), 1)
    lane_m = lax.broadcasted_iota(jnp.int32, (N_META, LANE), 1)
    qi = lax.broadcasted_iota(jnp.int32, (QROWS, 2 * BLOCK), 0) % BLOCK
    kj = lax.broadcasted_iota(jnp.int32, (QROWS, 2 * BLOCK), 1)
    prev_shift = jnp.where(blk > 0, 0, BLOCK)
    valid = ((kj < BLOCK) & (kj > qi + prev_shift)) | ((kj >= BLOCK) & (kj - BLOCK <= qi))
    rowpair = lax.broadcasted_iota(jnp.int32, (QROWS, 1), 0) // BLOCK
    zero = jnp.zeros((), BF16)

    for g in range(N_KV_HEADS):
        gs = slice(g * LANE, (g + 1) * LANE)
        qs = jnp.concatenate(
            [q_ref[:, (g * PAIRS + p) * LANE:(g * PAIRS + p + 1) * LANE] for p in range(PAIRS)], axis=0)
        kband = jnp.concatenate([kp_ref[:, gs], kc_ref[:, gs]], axis=0)
        vband = jnp.concatenate([vp_ref[:, gs], vc_ref[:, gs]], axis=0)
        km = kvm_ref[:, gs]
        vm = kvm_ref[:, N_KV_HEADS * LANE + g * LANE:N_KV_HEADS * LANE + (g + 1) * LANE]
        acc = None
        for half in range(2):
            keep = (lane >= HEAD_DIM) if half else (lane < HEAD_DIM)
            keep_m = (lane_m >= HEAD_DIM) if half else (lane_m < HEAD_DIM)
            sb = _dot_nt(qs, jnp.where(keep, kband, zero))
            sm = _dot_nt(qs, jnp.where(keep_m, km, zero))
            sb = jnp.where(valid, sb, NEG_INF)
            sink = jnp.zeros((QROWS, 1), F32)
            for p in range(PAIRS):
                sink = jnp.where(rowpair == p, sink_ref[g * 2 * PAIRS + 2 * p + half], sink)
            m = jnp.maximum(jnp.maximum(jnp.max(sb, axis=-1, keepdims=True),
                                        jnp.max(sm, axis=-1, keepdims=True)), sink)
            eb = jnp.exp(sb - m)
            em = jnp.exp(sm - m)
            den = (jnp.sum(eb, axis=-1, keepdims=True) + jnp.sum(em, axis=-1, keepdims=True)
                   + jnp.exp(sink - m))
            pb = (eb / den).astype(BF16)
            pm = (em / den).astype(BF16)
            part = _dot(pm, jnp.where(keep_m, vm, zero)) + _dot(pb, jnp.where(keep, vband, zero))
            acc = part if acc is None else acc + part
        for p in range(PAIRS):
            col = (g * PAIRS + p) * LANE
            o_ref[:, col:col + LANE] = acc[p * BLOCK:(p + 1) * BLOCK].astype(BF16)


def _swa_attn(sinks, q, kd, vd, kvm, batch, nblk):
    n = q.shape[0]
    kw = 2 * KV_WIDTH
    cur = lambda b, i: (b * nblk + i, 0)
    prev = lambda b, i: (b * nblk + jnp.maximum(i - 1, 0), 0)
    return pl.pallas_call(
        _attn_kernel,
        grid=(batch, nblk),
        in_specs=[
            pl.BlockSpec(memory_space=pltpu.SMEM),
            pl.BlockSpec((BLOCK, D_MODEL), cur),
            pl.BlockSpec((BLOCK, kw), prev),
            pl.BlockSpec((BLOCK, kw), cur),
            pl.BlockSpec((BLOCK, kw), prev),
            pl.BlockSpec((BLOCK, kw), cur),
            pl.BlockSpec((N_META, 2 * kw), lambda b, i: (0, 0)),
        ],
        out_specs=pl.BlockSpec((BLOCK, D_MODEL), cur),
        out_shape=jax.ShapeDtypeStruct((n, D_MODEL), BF16),
        compiler_params=_params("arbitrary", "arbitrary"),
        name="swa_attn",
    )(sinks, q, kd, kd, vd, vd, kvm)


TAIL_STEPS = 2 * N_CHUNKS + 1


def _lane_concat(ref):
    return jnp.concatenate([ref[k] for k in range(ref.shape[0])], axis=1)


def _tail_kernel(x_ref, hb_ref, a_ref, g0_ref, b0_ref, um_ref, cw_ref, wa_ref, wb_ref, wc_ref, wd_ref,
                 g1_ref, b1_ref, rw_ref, rb_ref, h1_ref, lg_ref,
                 uext_ref, carry_ref, z_ref, m_ref, *, tiles_per_seq):
    i = pl.program_id(0)
    s = pl.program_id(1)
    tm = ROW_TILE

    @pl.when((i == 0) & (s == 0))
    def _():
        carry_ref[...] = jnp.zeros_like(carry_ref)

    @pl.when(s < N_CHUNKS)
    def _():
        hb = hb_ref[...]
        u = _dot(hb, wa_ref[...]) * _dot(hb, wb_ref[...])
        first = i % tiles_per_seq == 0
        uext_ref[0:8, :] = jnp.where(first, um_ref[N_META - 8:N_META, :], carry_ref[s])
        uext_ref[8:tm + 8, :] = u
        conv = (cw_ref[0:1, :] * uext_ref[6:tm + 6, :] + cw_ref[1:2, :] * uext_ref[7:tm + 7, :]
                + cw_ref[2:3, :] * u)
        carry_ref[s] = u[tm - 8:tm, :]
        z_ref[s] = (_dot(hb, wc_ref[...]) * conv).astype(BF16)

    @pl.when((s >= N_CHUNKS) & (s < 2 * N_CHUNKS))
    def _():
        hb = hb_ref[...]
        conv_part = jax.nn.sigmoid(_dot(hb, wb_ref[...])) * _dot(_lane_concat(z_ref), wd_ref[...])
        gate_attn = jax.nn.sigmoid(_dot(hb, wa_ref[...]))
        m_ref[s - N_CHUNKS] = (gate_attn * _dot(a_ref[...], wc_ref[...]) + conv_part).astype(BF16)

    @pl.when(s == 2 * N_CHUNKS)
    def _():
        h = _layer_norm(x_ref[...], g0_ref[...], b0_ref[...])
        m = _lane_concat(m_ref)
        for k, w_ref in enumerate((wa_ref, wb_ref, wc_ref, wd_ref)):
            cs = slice(k * CHUNK, (k + 1) * CHUNK)
            h1_ref[:, cs] = ALPHA * h[:, cs] + _dot(m, w_ref[...])
        h1 = _layer_norm(h1_ref[...], g1_ref[...], b1_ref[...])
        h1_ref[...] = h1
        lg_ref[...] = _dot_nt(rw_ref[...], h1.astype(BF16)) + rb_ref[...][:, 0:1]


def _mixer_tail(x2d, hb, attn, g0, b0, um, cw, wall, g1, b1, rw, rb, seq):
    n = x2d.shape[0]
    nc = N_CHUNKS

    def slot(conv_first, merge_first, out_block, hold=False):
        def index(i, s):
            conv = conv_first if hold else conv_first + s
            return 0, jnp.where(s < nc, conv, jnp.where(s < 2 * nc, merge_first + s - nc, W_OUT + out_block))
        return pl.BlockSpec((D_MODEL, CHUNK), index)

    lo = lambda i, s: (0, jnp.minimum(s, nc - 1))
    const = lambda i, s: (0, 0)
    row = lambda i, s: (i, 0)
    return pl.pallas_call(
        functools.partial(_tail_kernel, tiles_per_seq=seq // ROW_TILE),
        grid=(n // ROW_TILE, TAIL_STEPS),
        in_specs=[
            pl.BlockSpec((ROW_TILE, D_MODEL), row),
            pl.BlockSpec((ROW_TILE, D_MODEL), row),
            pl.BlockSpec((ROW_TILE, D_MODEL), row),
            pl.BlockSpec((1, D_MODEL), const),
            pl.BlockSpec((1, D_MODEL), const),
            pl.BlockSpec((N_META, CHUNK), lo),
            pl.BlockSpec((CONV_K, CHUNK), lo),
            slot(W_C, W_GA, 0),
            slot(W_H, W_GC, 1),
            slot(W_B, W_BA, 2),
            slot(W_BC, W_BC, 3, hold=True),
            pl.BlockSpec((1, D_MODEL), const),
            pl.BlockSpec((1, D_MODEL), const),
            pl.BlockSpec((ROUTE_ROWS, D_MODEL), const),
            pl.BlockSpec((ROUTE_ROWS, LANE), const),
        ],
        out_specs=[
            pl.BlockSpec((ROW_TILE, D_MODEL), row),
            pl.BlockSpec((ROUTE_ROWS, ROW_TILE), lambda i, s: (0, i)),
        ],
        out_shape=[
            jax.ShapeDtypeStruct((n, D_MODEL), F32),
            jax.ShapeDtypeStruct((ROUTE_ROWS, n), F32),
        ],
        scratch_shapes=[
            pltpu.VMEM((ROW_TILE + 8, CHUNK), F32),
            pltpu.VMEM((nc, 8, CHUNK), F32),
            pltpu.VMEM((nc, ROW_TILE, CHUNK), BF16),
            pltpu.VMEM((nc, ROW_TILE, CHUNK), BF16),
        ],
        compiler_params=_params("arbitrary", "arbitrary"),
        name="mixer_tail",
    )(x2d, hb, attn, g0, b0, um, cw, wall, wall, wall, wall, g1, b1, rw, rb)


RCHUNK = 1024
PBLK = 256


def _route_kernel(lg_ref, e_ref, w_ref, pos_ref, cnt_ref, rank_ref):
    n = lg_ref.shape[1]
    epg = EXPERTS_PER_GROUP

    def route_chunk(c, carry):
        sl = pl.ds(pl.multiple_of(c * RCHUNK, RCHUNK), RCHUNK)
        gl = [lg_ref[gi:gi + 1, sl] for gi in range(N_GROUPS)]
        gmax = functools.reduce(jnp.maximum, gl)
        ge = [jnp.exp(v - gmax) for v in gl]
        gsum = functools.reduce(jnp.add, ge)
        gp = [v / gsum for v in ge]
        p_group = functools.reduce(jnp.maximum, gp)
        gsel = jnp.full_like(p_group, float(N_GROUPS - 1))
        for gi in range(N_GROUPS - 2, -1, -1):
            gsel = jnp.where(gp[gi] >= p_group, float(gi), gsel)
        e_in = lg_ref[EXPERT_ROW0 + (N_GROUPS - 1) * epg:EXPERT_ROW0 + N_GROUPS * epg, sl]
        for gi in range(N_GROUPS - 2, -1, -1):
            e_in = jnp.where(gsel == float(gi),
                             lg_ref[EXPERT_ROW0 + gi * epg:EXPERT_ROW0 + (gi + 1) * epg, sl], e_in)
        ee = jnp.exp(e_in - jnp.max(e_in, axis=0, keepdims=True))
        pe = ee / jnp.sum(ee, axis=0, keepdims=True)
        ridx = lax.broadcasted_iota(jnp.int32, pe.shape, 0).astype(F32)
        p0 = jnp.max(pe, axis=0, keepdims=True)
        i0 = jnp.min(jnp.where(pe >= p0, ridx, float(epg)), axis=0, keepdims=True)
        pe2 = jnp.where(ridx == i0, -1.0, pe)
        p1 = jnp.max(pe2, axis=0, keepdims=True)
        i1 = jnp.min(jnp.where(pe2 >= p1, ridx, float(epg)), axis=0, keepdims=True)
        den = p0 + p1
        e_ref[0:1, sl] = (gsel * epg + i0).astype(jnp.int32)
        e_ref[1:2, sl] = (gsel * epg + i1).astype(jnp.int32)
        w_ref[0:1, sl] = p0 / den * p_group
        w_ref[1:2, sl] = p1 / den * p_group
        return carry

    lax.fori_loop(0, n // RCHUNK, route_chunk, 0)

    eid = lax.broadcasted_iota(jnp.int32, (N_EXPERTS, PBLK), 0)
    before = jnp.where(lax.broadcasted_iota(jnp.int32, (PBLK, PBLK), 0)
                       < lax.broadcasted_iota(jnp.int32, (PBLK, PBLK), 1), 1.0, 0.0).astype(BF16)
    ones = jnp.ones((PBLK, LANE), BF16)
    nblk = n // PBLK

    def onehot(k, blk):
        sl = pl.ds(pl.multiple_of(blk * PBLK, PBLK), PBLK)
        return sl, eid == e_ref[pl.ds(k, 1), sl]

    run = jnp.zeros((N_EXPERTS, LANE), F32)
    for k in range(2):
        def count_block(blk, run, k=k):
            sl, hit = onehot(k, blk)
            hit_bf = jnp.where(hit, 1.0, 0.0).astype(BF16)
            prior = _dot(hit_bf, before) + jnp.concatenate([run] * (PBLK // LANE), axis=1)
            rank_ref[pl.ds(k, 1), sl] = jnp.sum(jnp.where(hit, prior, 0.0), axis=0, keepdims=True)
            return run + _dot(hit_bf, ones)
        run = lax.fori_loop(0, nblk, count_block, run)

    cnt_ref[...] = run.astype(jnp.int32)
    hi = jnp.floor(run * (1.0 / LANE))
    lo = run - hi * LANE
    below = jnp.where(lax.broadcasted_iota(jnp.int32, (N_EXPERTS, N_EXPERTS), 1)
                      < lax.broadcasted_iota(jnp.int32, (N_EXPERTS, N_EXPERTS), 0), 1.0, 0.0).astype(BF16)
    off = _dot(below, hi.astype(BF16)) * LANE + _dot(below, lo.astype(BF16))
    off = jnp.concatenate([off] * (PBLK // LANE), axis=1)

    for k in range(2):
        def place_block(blk, carry, k=k):
            sl, hit = onehot(k, blk)
            base = jnp.sum(jnp.where(hit, off, 0.0), axis=0, keepdims=True)
            pos_ref[pl.ds(k, 1), sl] = (base + rank_ref[pl.ds(k, 1), sl]).astype(jnp.int32)
            return carry
        lax.fori_loop(0, nblk, place_block, 0)


def _route(logits_t):
    n = logits_t.shape[1]
    return pl.pallas_call(
        _route_kernel,
        out_shape=[
            jax.ShapeDtypeStruct((2, n), jnp.int32),
            jax.ShapeDtypeStruct((2, n), F32),
            jax.ShapeDtypeStruct((2, n), jnp.int32),
            jax.ShapeDtypeStruct((N_EXPERTS, LANE), jnp.int32),
        ],
        scratch_shapes=[pltpu.VMEM((2, n), F32)],
        compiler_params=pltpu.CompilerParams(vmem_limit_bytes=V7X_VMEM_LIMIT),
        name="route",
    )(logits_t)


def _row_copy(src_ref, src_row, dst_ref, dst_row, sem):
    return pltpu.make_async_copy(src_ref.at[pl.ds(src_row, 1)], dst_ref.at[pl.ds(dst_row, 1)], sem)


def _dispatch_kernel(pos_ref, h_ref, xs_ref, sem):
    def issue(r, c):
        for k in range(2):
            _row_copy(h_ref, r, xs_ref, pos_ref[0, k, r], sem).start()
        return c

    def drain(r, c):
        for k in range(2):
            _row_copy(h_ref, 0, xs_ref, 0, sem).wait()
        return c

    lax.fori_loop(0, TOK_TILE, issue, 0)
    lax.fori_loop(0, TOK_TILE, drain, 0)


def _dispatch(pos_tiles, h1):
    n = h1.shape[0]
    return pl.pallas_call(
        _dispatch_kernel,
        grid=(n // TOK_TILE,),
        in_specs=[
            pl.BlockSpec((1, 2, TOK_TILE), lambda i: (i, 0, 0), memory_space=pltpu.SMEM),
            pl.BlockSpec((TOK_TILE, D_MODEL), lambda i: (i, 0)),
        ],
        out_specs=pl.BlockSpec(memory_space=pl.ANY),
        out_shape=jax.ShapeDtypeStruct((2 * n, D_MODEL), F32),
        scratch_shapes=[pltpu.SemaphoreType.DMA],
        compiler_params=_params("arbitrary"),
        name="dispatch",
    )(pos_tiles, h1)


def _moe_kernel(tile_ref, exp_ref, lo_ref, first_ref, slot_ref, next_ref, n_ref,
                xs_ref, wg_hbm, wu_hbm, wd_hbm, y_ref,
                wg_buf, wu_buf, wd_buf, wg_bf, wu_bf, wd_bf, sem):
    i = pl.program_id(0)

    def weight_copies(expert, slot):
        return (pltpu.make_async_copy(wg_hbm.at[expert], wg_buf.at[slot], sem.at[slot, 0]),
                pltpu.make_async_copy(wu_hbm.at[expert], wu_buf.at[slot], sem.at[slot, 1]),
                pltpu.make_async_copy(wd_hbm.at[expert], wd_buf.at[slot], sem.at[slot, 2]))

    @pl.when(i == 0)
    def _():
        for copy in weight_copies(exp_ref[0], 0):
            copy.start()

    valid = i < n_ref[0]

    @pl.when(valid & (first_ref[i] == 1))
    def _():
        slot = slot_ref[i]
        for copy in weight_copies(exp_ref[i], slot):
            copy.wait()

        @pl.when(next_ref[i] >= 0)
        def _():
            for copy in weight_copies(next_ref[i], 1 - slot):
                copy.start()

        wg_bf[...] = wg_buf[slot].astype(BF16)
        wu_bf[...] = wu_buf[slot].astype(BF16)
        wd_bf[...] = wd_buf[slot].astype(BF16)

    @pl.when(valid)
    def _():
        x = xs_ref[...].astype(BF16)
        g = _dot(x, wg_bf[...])
        u = _dot(x, wu_bf[...])
        a = (g * jax.nn.sigmoid(g) * u).astype(BF16)
        y = _dot(a, wd_bf[...])
        lo = lo_ref[i]

        @pl.when(lo == 0)
        def _():
            y_ref[...] = y

        @pl.when(lo > 0)
        def _():
            row = lax.broadcasted_iota(jnp.int32, (MOE_TILE, 1), 0)
            y_ref[...] = jnp.where(row >= lo, y, y_ref[...])


def _moe_experts(items, xs, wg, wu, wd):
    rows = xs.shape[0]
    tile_map = lambda i, t, *_: (t[i], 0)
    grid_spec = pltpu.PrefetchScalarGridSpec(
        num_scalar_prefetch=len(items),
        grid=(MAX_ITEMS,),
        in_specs=[
            pl.BlockSpec((MOE_TILE, D_MODEL), tile_map),
            pl.BlockSpec(memory_space=pl.ANY),
            pl.BlockSpec(memory_space=pl.ANY),
            pl.BlockSpec(memory_space=pl.ANY),
        ],
        out_specs=pl.BlockSpec((MOE_TILE, D_MODEL), tile_map),
        scratch_shapes=[
            pltpu.VMEM((2, D_MODEL, D_EXPERT), F32),
            pltpu.VMEM((2, D_MODEL, D_EXPERT), F32),
            pltpu.VMEM((2, D_EXPERT, D_MODEL), F32),
            pltpu.VMEM((D_MODEL, D_EXPERT), BF16),
            pltpu.VMEM((D_MODEL, D_EXPERT), BF16),
            pltpu.VMEM((D_EXPERT, D_MODEL), BF16),
            pltpu.SemaphoreType.DMA((2, 3)),
        ],
    )
    return pl.pallas_call(
        _moe_kernel,
        grid_spec=grid_spec,
        out_shape=jax.ShapeDtypeStruct((rows, D_MODEL), F32),
        compiler_params=_params("arbitrary"),
        name="moe_experts",
    )(*items, xs, wg, wu, wd)


def _work_items(counts):
    i32 = jnp.int32
    off = jnp.cumsum(counts) - counts
    end = off + counts
    first_tile = off // MOE_TILE
    n_e = jnp.where(counts > 0, (end - 1) // MOE_TILE - first_tile + 1, 0)
    item_end = jnp.cumsum(n_e)
    n_items = item_end[-1]
    idx = jnp.minimum(jnp.arange(MAX_ITEMS, dtype=i32), n_items - 1)
    exp = jnp.sum((item_end[None, :] <= idx[:, None]).astype(i32), axis=1)
    item0 = item_end[exp] - n_e[exp]
    tile = first_tile[exp] + idx - item0
    lo = jnp.maximum(off[exp] - tile * MOE_TILE, 0)
    present = counts > 0
    ordinal = jnp.cumsum(present.astype(i32)) - 1
    eid = jnp.arange(N_EXPERTS, dtype=i32)
    later = present[None, :] & (eid[None, :] > eid[:, None])
    nxt = jnp.min(jnp.where(later, eid[None, :], N_EXPERTS), axis=1)
    nxt = jnp.where(nxt == N_EXPERTS, -1, nxt)
    items = (tile, exp, lo, idx == item0, ordinal[exp] % 2, nxt[exp], n_items.reshape(1))
    return tuple(v.astype(i32) for v in items)


def _combine_kernel(pos_ref, h_ref, wt_ref, g_ref, b_ref, y_ref, o_ref, buf_ref, sem):
    def issue(r, c):
        for k in range(2):
            _row_copy(y_ref, pos_ref[0, k, r], buf_ref.at[k], r, sem).start()
        return c

    def drain(r, c):
        for k in range(2):
            _row_copy(y_ref, 0, buf_ref.at[k], 0, sem).wait()
        return c

    lax.fori_loop(0, TOK_TILE, issue, 0)
    lax.fori_loop(0, TOK_TILE, drain, 0)
    f = wt_ref[:, 0:1] * buf_ref[0] + wt_ref[:, 1:2] * buf_ref[1]
    o_ref[...] = _layer_norm(ALPHA * h_ref[...] + f, g_ref[...], b_ref[...])


def _combine_ln(pos_tiles, h1, wt, g, b, y):
    n = h1.shape[0]
    return pl.pallas_call(
        _combine_kernel,
        grid=(n // TOK_TILE,),
        in_specs=[
            pl.BlockSpec((1, 2, TOK_TILE), lambda i: (i, 0, 0), memory_space=pltpu.SMEM),
            pl.BlockSpec((TOK_TILE, D_MODEL), lambda i: (i, 0)),
            pl.BlockSpec((TOK_TILE, 2), lambda i: (i, 0)),
            pl.BlockSpec((1, D_MODEL), lambda i: (0, 0)),
            pl.BlockSpec((1, D_MODEL), lambda i: (0, 0)),
            pl.BlockSpec(memory_space=pl.ANY),
        ],
        out_specs=pl.BlockSpec((TOK_TILE, D_MODEL), lambda i: (i, 0)),
        out_shape=jax.ShapeDtypeStruct((n, D_MODEL), F32),
        scratch_shapes=[pltpu.VMEM((2, TOK_TILE, D_MODEL), F32), pltpu.SemaphoreType.DMA],
        compiler_params=_params("arbitrary"),
        name="combine_ln",
    )(pos_tiles, h1, wt, g, b, y)


def _dup_heads(w):
    d = w.shape[0]
    w = w.reshape(d, N_KV_HEADS, 1, HEAD_DIM)
    return jnp.broadcast_to(w, (d, N_KV_HEADS, 2, HEAD_DIM)).reshape(d, 2 * KV_WIDTH)


def kernel(x, meta_tokens, ln_in_g, ln_in_b, w_in, sinks, conv_w, w_branch_attn, w_branch_conv, w_out,
           ln1_g, ln1_b, router_group_w, router_group_b, router_expert_w, router_expert_b,
           w_gate, w_up, w_down, ln2_g, ln2_b):
    batch, seq, d = x.shape
    n = batch * seq
    assert d == D_MODEL and seq % ROW_TILE == 0 and w_in.shape[0] == DEPTH == 1
    row = lambda v: v.reshape(1, -1).astype(F32)
    x2d = x.reshape(n, d)
    g0, b0 = row(ln_in_g), row(ln_in_b)

    win = w_in[0]
    kcol, vcol, ccol = D_MODEL, D_MODEL + KV_WIDTH, D_MODEL + 2 * KV_WIDTH
    wkv_dup = jnp.concatenate([_dup_heads(win[:, kcol:vcol]), _dup_heads(win[:, vcol:ccol])], axis=1).astype(BF16)
    wall = _pack_weights(win, w_branch_attn[0], w_branch_conv[0], w_out[0])
    rw = jnp.zeros((ROUTE_ROWS, d), F32)
    rw = rw.at[:N_GROUPS].set(router_group_w[0].T).at[EXPERT_ROW0:EXPERT_ROW0 + N_EXPERTS].set(router_expert_w[0].T)
    rb = jnp.zeros((ROUTE_ROWS,), F32)
    rb = rb.at[:N_GROUPS].set(router_group_b[0]).at[EXPERT_ROW0:EXPERT_ROW0 + N_EXPERTS].set(router_expert_b[0])
    rb = jnp.broadcast_to(rb[:, None], (ROUTE_ROWS, LANE))

    kvm, um = _meta_prep(meta_tokens.astype(F32), g0, b0, wkv_dup, wall)
    hb, q, kd, vd = _ln_qkv(x2d, g0, b0, wall, wkv_dup)
    attn = _swa_attn(sinks[0].astype(F32), q, kd, vd, kvm, batch, seq // BLOCK)
    h1, logits_t = _mixer_tail(x2d, hb, attn, g0, b0, um, conv_w[0], wall,
                               row(ln1_g[0]), row(ln1_b[0]), rw.astype(BF16), rb, seq)
    _, top_w, pos, counts = _route(logits_t)
    pos_tiles = pos.reshape(2, n // TOK_TILE, TOK_TILE).transpose(1, 0, 2)
    xs = _dispatch(pos_tiles, h1)
    y = _moe_experts(_work_items(counts[:, 0]), xs, w_gate[0], w_up[0], w_down[0])
    out = _combine_ln(pos_tiles, h1, top_w.T, row(ln2_g[0]), row(ln2_b[0]), y)
    return out.reshape(batch, seq, d)
```

```python
import functools

import jax
import jax.numpy as jnp
from jax import lax
from jax.experimental import pallas as pl
from jax.experimental.pallas import tpu as pltpu

D_MODEL = 2048
N_META = 16
HEAD_DIM = 64
N_Q_HEADS = 32
N_KV_HEADS = 4
BLOCK = 128
KV_WIDTH = N_KV_HEADS * HEAD_DIM
CONV_K = 3
N_GROUPS = 4
EXPERTS_PER_GROUP = 8
N_EXPERTS = N_GROUPS * EXPERTS_PER_GROUP
D_EXPERT = D_MODEL // 4
LN_EPS = 1e-5
DEPTH = 1
ALPHA = (2.0 * DEPTH) ** 0.25
NEG_INF = -1e30
Q_SCALE = HEAD_DIM ** -0.5

V7X_VMEM_LIMIT = 56 * 1024 * 1024
LANE = 128
CHUNK = 512
N_CHUNKS = D_MODEL // CHUNK
ROW_TILE = 512
MOE_TILE = 256
TOK_TILE = 256
MAX_ITEMS = 2 * 8192 // MOE_TILE + N_EXPERTS
ROUTE_ROWS = 128
EXPERT_ROW0 = 8

BF16 = jnp.bfloat16
F32 = jnp.float32


def _layer_norm(x, g, b):
    mu = jnp.mean(x, axis=-1, keepdims=True)
    xc = x - mu
    var = jnp.mean(xc * xc, axis=-1, keepdims=True)
    return xc * lax.rsqrt(var + LN_EPS) * g + b


def _dot(a, b):
    return jnp.dot(a, b, preferred_element_type=F32)


def _dot_nt(a, b):
    return lax.dot_general(a, b, (((1,), (1,)), ((), ())), preferred_element_type=F32)


def _params(*sem):
    return pltpu.CompilerParams(dimension_semantics=sem, vmem_limit_bytes=V7X_VMEM_LIMIT)


W_B, W_C, W_H, W_GA, W_GC, W_BA, W_BC, W_OUT, W_Q = (k * N_CHUNKS for k in range(9))
N_WBLOCKS = 9 * N_CHUNKS
IN_FIRST_MIX = (D_MODEL + 2 * KV_WIDTH) // CHUNK


def _pack_kernel(win_ref, ba_ref, bc_ref, out_ref, o_ref):
    j = pl.program_id(0)

    @pl.when((j < W_BA) | (j >= W_Q))
    def _():
        o_ref[...] = win_ref[...].astype(BF16)

    @pl.when((j >= W_BA) & (j < W_BC))
    def _():
        o_ref[...] = ba_ref[...].astype(BF16)

    @pl.when((j >= W_BC) & (j < W_OUT))
    def _():
        o_ref[...] = bc_ref[...].astype(BF16)

    @pl.when((j >= W_OUT) & (j < W_Q))
    def _():
        o_ref[...] = out_ref[...].astype(BF16)


def _pack_weights(win, wba, wbc, wout):
    held = lambda first: (lambda j: (0, jnp.clip(j - first, 0, N_CHUNKS - 1)))
    win_map = lambda j: (0, jnp.where(j >= W_Q, j - W_Q, jnp.minimum(j, W_BA - 1) + IN_FIRST_MIX))
    return pl.pallas_call(
        _pack_kernel,
        grid=(N_WBLOCKS,),
        in_specs=[
            pl.BlockSpec((D_MODEL, CHUNK), win_map),
            pl.BlockSpec((D_MODEL, CHUNK), held(W_BA)),
            pl.BlockSpec((D_MODEL, CHUNK), held(W_BC)),
            pl.BlockSpec((D_MODEL, CHUNK), held(W_OUT)),
        ],
        out_specs=pl.BlockSpec((D_MODEL, CHUNK), lambda j: (0, j)),
        out_shape=jax.ShapeDtypeStruct((D_MODEL, N_WBLOCKS * CHUNK), BF16),
        compiler_params=_params("arbitrary"),
        name="pack_weights",
    )(win, wba, wbc, wout)


def _meta_kernel(meta_ref, g_ref, b_ref, wkv_ref, wcc_ref, wch_ref, kv_ref, u_ref):
    hm = _layer_norm(meta_ref[...], g_ref[...], b_ref[...]).astype(BF16)
    kv_ref[...] = _dot(hm, wkv_ref[...]).astype(BF16)
    u_ref[...] = _dot(hm, wcc_ref[...]) * _dot(hm, wch_ref[...])


def _meta_prep(meta, g, b, wkv_dup, wall):
    n_kv = wkv_dup.shape[1] // N_CHUNKS
    return pl.pallas_call(
        _meta_kernel,
        grid=(N_CHUNKS,),
        in_specs=[
            pl.BlockSpec((N_META, D_MODEL), lambda j: (0, 0)),
            pl.BlockSpec((1, D_MODEL), lambda j: (0, 0)),
            pl.BlockSpec((1, D_MODEL), lambda j: (0, 0)),
            pl.BlockSpec((D_MODEL, n_kv), lambda j: (0, j)),
            pl.BlockSpec((D_MODEL, CHUNK), lambda j: (0, W_C + j)),
            pl.BlockSpec((D_MODEL, CHUNK), lambda j: (0, W_H + j)),
        ],
        out_specs=[
            pl.BlockSpec((N_META, n_kv), lambda j: (0, j)),
            pl.BlockSpec((N_META, CHUNK), lambda j: (0, j)),
        ],
        out_shape=[
            jax.ShapeDtypeStruct((N_META, wkv_dup.shape[1]), BF16),
            jax.ShapeDtypeStruct((N_META, D_MODEL), F32),
        ],
        compiler_params=_params("arbitrary"),
        name="meta_prep",
    )(meta, g, b, wkv_dup, wall, wall)


def _qkv_kernel(x_ref, g_ref, b_ref, wq_ref, wkv_ref, hb_ref, q_ref, k_ref, v_ref):
    hb = _layer_norm(x_ref[...], g_ref[...], b_ref[...]).astype(BF16)
    kw = 2 * KV_WIDTH
    hb_ref[...] = hb
    q_ref[...] = (_dot(hb, wq_ref[...]) * Q_SCALE).astype(BF16)
    k_ref[...] = _dot(hb, wkv_ref[:, :kw]).astype(BF16)
    v_ref[...] = _dot(hb, wkv_ref[:, kw:]).astype(BF16)


def _ln_qkv(x2d, g, b, wall, wkv_dup):
    n = x2d.shape[0]
    kw = 2 * KV_WIDTH
    return pl.pallas_call(
        _qkv_kernel,
        grid=(n // ROW_TILE,),
        in_specs=[
            pl.BlockSpec((ROW_TILE, D_MODEL), lambda i: (i, 0)),
            pl.BlockSpec((1, D_MODEL), lambda i: (0, 0)),
            pl.BlockSpec((1, D_MODEL), lambda i: (0, 0)),
            pl.BlockSpec((D_MODEL, D_MODEL), lambda i: (0, W_Q // N_CHUNKS)),
            pl.BlockSpec(wkv_dup.shape, lambda i: (0, 0)),
        ],
        out_specs=[
            pl.BlockSpec((ROW_TILE, D_MODEL), lambda i: (i, 0)),
            pl.BlockSpec((ROW_TILE, D_MODEL), lambda i: (i, 0)),
            pl.BlockSpec((ROW_TILE, kw), lambda i: (i, 0)),
            pl.BlockSpec((ROW_TILE, kw), lambda i: (i, 0)),
        ],
        out_shape=[
            jax.ShapeDtypeStruct((n, D_MODEL), BF16),
            jax.ShapeDtypeStruct((n, D_MODEL), BF16),
            jax.ShapeDtypeStruct((n, kw), BF16),
            jax.ShapeDtypeStruct((n, kw), BF16),
        ],
        compiler_params=_params("arbitrary"),
        name="ln_qkv",
    )(x2d, g, b, wall, wkv_dup)


PAIRS = N_Q_HEADS // N_KV_HEADS // 2
QROWS = PAIRS * BLOCK


def _attn_kernel(sink_ref, q_ref, kp_ref, kc_ref, vp_ref, vc_ref, kvm_ref, o_ref):
    blk = pl.program_id(1)
    lane = lax.broadcasted_iota(jnp.int32, (BLOCK, LANE), 1)
    qi = lax.broadcasted_iota(jnp.int32, (QROWS, BLOCK), 0) % BLOCK
    kj = lax.broadcasted_iota(jnp.int32, (QROWS, BLOCK), 1)
    from_prev = kj > qi
    has_prev = blk > 0
    is_meta = kj < N_META
    rowpair = lax.broadcasted_iota(jnp.int32, (QROWS, 1), 0) // BLOCK
    zero = jnp.zeros((), BF16)
    zpad = jnp.zeros((BLOCK - N_META, LANE), BF16)

    for g in range(N_KV_HEADS):
        gs = slice(g * LANE, (g + 1) * LANE)
        vs = slice((N_KV_HEADS + g) * LANE, (N_KV_HEADS + g + 1) * LANE)
        qs = jnp.concatenate(
            [q_ref[:, (g * PAIRS + p) * LANE:(g * PAIRS + p + 1) * LANE] for p in range(PAIRS)], axis=0)
        keys = (kp_ref[:, gs], kc_ref[:, gs], jnp.concatenate([kvm_ref[:, gs], zpad], axis=0))
        vals = (vp_ref[:, gs], vc_ref[:, gs], jnp.concatenate([kvm_ref[:, vs], zpad], axis=0))
        acc = None
        for half in range(2):
            keep = (lane >= HEAD_DIM) if half else (lane < HEAD_DIM)
            s = _dot_nt(qs, jnp.concatenate([jnp.where(keep, k, zero) for k in keys], axis=0))
            s_prev = jnp.where(has_prev, s[:, :BLOCK], NEG_INF)
            s_band = jnp.where(from_prev, s_prev, s[:, BLOCK:2 * BLOCK])
            s_meta = jnp.where(is_meta, s[:, 2 * BLOCK:], NEG_INF)
            sink = jnp.zeros((QROWS, 1), F32)
            for p in range(PAIRS):
                sink = jnp.where(rowpair == p, sink_ref[g * 2 * PAIRS + 2 * p + half], sink)
            m = jnp.maximum(jnp.max(jnp.maximum(s_band, s_meta), axis=-1, keepdims=True), sink)
            e_band = jnp.exp(s_band - m)
            e_meta = jnp.exp(s_meta - m)
            den = jnp.sum(e_band + e_meta, axis=-1, keepdims=True) + jnp.exp(sink - m)
            p_band = e_band / den
            probs = jnp.concatenate([jnp.where(from_prev, p_band, 0.0).astype(BF16),
                                     jnp.where(from_prev, 0.0, p_band).astype(BF16),
                                     (e_meta / den).astype(BF16)], axis=1)
            part = _dot(probs, jnp.concatenate([jnp.where(keep, v, zero) for v in vals], axis=0))
            acc = part if acc is None else acc + part
        for p in range(PAIRS):
            col = (g * PAIRS + p) * LANE
            o_ref[:, col:col + LANE] = acc[p * BLOCK:(p + 1) * BLOCK].astype(BF16)


def _swa_attn(sinks, q, kd, vd, kvm, batch, nblk):
    n = q.shape[0]
    kw = 2 * KV_WIDTH
    cur = lambda b, i: (b * nblk + i, 0)
    prev = lambda b, i: (b * nblk + jnp.maximum(i - 1, 0), 0)
    return pl.pallas_call(
        _attn_kernel,
        grid=(batch, nblk),
        in_specs=[
            pl.BlockSpec(memory_space=pltpu.SMEM),
            pl.BlockSpec((BLOCK, D_MODEL), cur),
            pl.BlockSpec((BLOCK, kw), prev),
            pl.BlockSpec((BLOCK, kw), cur),
            pl.BlockSpec((BLOCK, kw), prev),
            pl.BlockSpec((BLOCK, kw), cur),
            pl.BlockSpec((N_META, 2 * kw), lambda b, i: (0, 0)),
        ],
        out_specs=pl.BlockSpec((BLOCK, D_MODEL), cur),
        out_shape=jax.ShapeDtypeStruct((n, D_MODEL), BF16),
        compiler_params=_params("arbitrary", "arbitrary"),
        name="swa_attn",
    )(sinks, q, kd, kd, vd, vd, kvm)


TAIL_STEPS = 2 * N_CHUNKS + 1


def _lane_concat(ref):
    return jnp.concatenate([ref[k] for k in range(ref.shape[0])], axis=1)


def _tail_kernel(x_ref, hb_ref, a_ref, g0_ref, b0_ref, um_ref, cw_ref, wa_ref, wb_ref, wc_ref, wd_ref,
                 g1_ref, b1_ref, rw_ref, rb_ref, h1_ref, lg_ref,
                 uext_ref, carry_ref, z_ref, m_ref, *, tiles_per_seq):
    i = pl.program_id(0)
    s = pl.program_id(1)
    tm = ROW_TILE

    @pl.when((i == 0) & (s == 0))
    def _():
        carry_ref[...] = jnp.zeros_like(carry_ref)

    @pl.when(s < N_CHUNKS)
    def _():
        hb = hb_ref[...]
        u = _dot(hb, wa_ref[...]) * _dot(hb, wb_ref[...])
        first = i % tiles_per_seq == 0
        uext_ref[0:8, :] = jnp.where(first, um_ref[N_META - 8:N_META, :], carry_ref[s])
        uext_ref[8:tm + 8, :] = u
        conv = (cw_ref[0:1, :] * uext_ref[6:tm + 6, :] + cw_ref[1:2, :] * uext_ref[7:tm + 7, :]
                + cw_ref[2:3, :] * u)
        carry_ref[s] = u[tm - 8:tm, :]
        z_ref[s] = (_dot(hb, wc_ref[...]) * conv).astype(BF16)

    @pl.when((s >= N_CHUNKS) & (s < 2 * N_CHUNKS))
    def _():
        hb = hb_ref[...]
        conv_part = jax.nn.sigmoid(_dot(hb, wb_ref[...])) * _dot(_lane_concat(z_ref), wd_ref[...])
        gate_attn = jax.nn.sigmoid(_dot(hb, wa_ref[...]))
        m_ref[s - N_CHUNKS] = (gate_attn * _dot(a_ref[...], wc_ref[...]) + conv_part).astype(BF16)

    @pl.when(s == 2 * N_CHUNKS)
    def _():
        h = _layer_norm(x_ref[...], g0_ref[...], b0_ref[...])
        m = _lane_concat(m_ref)
        for k, w_ref in enumerate((wa_ref, wb_ref, wc_ref, wd_ref)):
            cs = slice(k * CHUNK, (k + 1) * CHUNK)
            h1_ref[:, cs] = ALPHA * h[:, cs] + _dot(m, w_ref[...])
        h1 = _layer_norm(h1_ref[...], g1_ref[...], b1_ref[...])
        h1_ref[...] = h1
        lg_ref[...] = _dot_nt(rw_ref[...], h1.astype(BF16)) + rb_ref[...][:, 0:1]


def _mixer_tail(x2d, hb, attn, g0, b0, um, cw, wall, g1, b1, rw, rb, seq):
    n = x2d.shape[0]
    nc = N_CHUNKS

    def slot(conv_first, merge_first, out_block, hold=False):
        def index(i, s):
            conv = conv_first if hold else conv_first + s
            return 0, jnp.where(s < nc, conv, jnp.where(s < 2 * nc, merge_first + s - nc, W_OUT + out_block))
        return pl.BlockSpec((D_MODEL, CHUNK), index)

    lo = lambda i, s: (0, jnp.minimum(s, nc - 1))
    const = lambda i, s: (0, 0)
    row = lambda i, s: (i, 0)
    return pl.pallas_call(
        functools.partial(_tail_kernel, tiles_per_seq=seq // ROW_TILE),
        grid=(n // ROW_TILE, TAIL_STEPS),
        in_specs=[
            pl.BlockSpec((ROW_TILE, D_MODEL), row),
            pl.BlockSpec((ROW_TILE, D_MODEL), row),
            pl.BlockSpec((ROW_TILE, D_MODEL), row),
            pl.BlockSpec((1, D_MODEL), const),
            pl.BlockSpec((1, D_MODEL), const),
            pl.BlockSpec((N_META, CHUNK), lo),
            pl.BlockSpec((CONV_K, CHUNK), lo),
            slot(W_C, W_GA, 0),
            slot(W_H, W_GC, 1),
            slot(W_B, W_BA, 2),
            slot(W_BC, W_BC, 3, hold=True),
            pl.BlockSpec((1, D_MODEL), const),
            pl.BlockSpec((1, D_MODEL), const),
            pl.BlockSpec((ROUTE_ROWS, D_MODEL), const),
            pl.BlockSpec((ROUTE_ROWS, LANE), const),
        ],
        out_specs=[
            pl.BlockSpec((ROW_TILE, D_MODEL), row),
            pl.BlockSpec((ROUTE_ROWS, ROW_TILE), lambda i, s: (0, i)),
        ],
        out_shape=[
            jax.ShapeDtypeStruct((n, D_MODEL), F32),
            jax.ShapeDtypeStruct((ROUTE_ROWS, n), F32),
        ],
        scratch_shapes=[
            pltpu.VMEM((ROW_TILE + 8, CHUNK), F32),
            pltpu.VMEM((nc, 8, CHUNK), F32),
            pltpu.VMEM((nc, ROW_TILE, CHUNK), BF16),
            pltpu.VMEM((nc, ROW_TILE, CHUNK), BF16),
        ],
        compiler_params=_params("arbitrary", "arbitrary"),
        name="mixer_tail",
    )(x2d, hb, attn, g0, b0, um, cw, wall, wall, wall, wall, g1, b1, rw, rb)


RCHUNK = 1024
PBLK = 256


def _route_kernel(lg_ref, e_ref, w_ref, pos_ref, cnt_ref, rank_ref):
    n = lg_ref.shape[1]
    epg = EXPERTS_PER_GROUP

    def route_chunk(c, carry):
        sl = pl.ds(pl.multiple_of(c * RCHUNK, RCHUNK), RCHUNK)
        gl = [lg_ref[gi:gi + 1, sl] for gi in range(N_GROUPS)]
        gmax = functools.reduce(jnp.maximum, gl)
        ge = [jnp.exp(v - gmax) for v in gl]
        gsum = functools.reduce(jnp.add, ge)
        gp = [v / gsum for v in ge]
        p_group = functools.reduce(jnp.maximum, gp)
        gsel = jnp.full_like(p_group, float(N_GROUPS - 1))
        for gi in range(N_GROUPS - 2, -1, -1):
            gsel = jnp.where(gp[gi] >= p_group, float(gi), gsel)
        e_in = lg_ref[EXPERT_ROW0 + (N_GROUPS - 1) * epg:EXPERT_ROW0 + N_GROUPS * epg, sl]
        for gi in range(N_GROUPS - 2, -1, -1):
            e_in = jnp.where(gsel == float(gi),
                             lg_ref[EXPERT_ROW0 + gi * epg:EXPERT_ROW0 + (gi + 1) * epg, sl], e_in)
        ee = jnp.exp(e_in - jnp.max(e_in, axis=0, keepdims=True))
        pe = ee / jnp.sum(ee, axis=0, keepdims=True)
        ridx = lax.broadcasted_iota(jnp.int32, pe.shape, 0).astype(F32)
        p0 = jnp.max(pe, axis=0, keepdims=True)
        i0 = jnp.min(jnp.where(pe >= p0, ridx, float(epg)), axis=0, keepdims=True)
        pe2 = jnp.where(ridx == i0, -1.0, pe)
        p1 = jnp.max(pe2, axis=0, keepdims=True)
        i1 = jnp.min(jnp.where(pe2 >= p1, ridx, float(epg)), axis=0, keepdims=True)
        den = p0 + p1
        e_ref[0:1, sl] = (gsel * epg + i0).astype(jnp.int32)
        e_ref[1:2, sl] = (gsel * epg + i1).astype(jnp.int32)
        w_ref[0:1, sl] = p0 / den * p_group
        w_ref[1:2, sl] = p1 / den * p_group
        return carry

    lax.fori_loop(0, n // RCHUNK, route_chunk, 0)

    eid = lax.broadcasted_iota(jnp.int32, (N_EXPERTS, PBLK), 0)
    before = jnp.where(lax.broadcasted_iota(jnp.int32, (PBLK, PBLK), 0)
                       < lax.broadcasted_iota(jnp.int32, (PBLK, PBLK), 1), 1.0, 0.0).astype(BF16)
    ones = jnp.ones((PBLK, LANE), BF16)
    nblk = n // PBLK

    def onehot(k, blk):
        sl = pl.ds(pl.multiple_of(blk * PBLK, PBLK), PBLK)
        return sl, eid == e_ref[pl.ds(k, 1), sl]

    run = jnp.zeros((N_EXPERTS, LANE), F32)
    for k in range(2):
        def count_block(blk, run, k=k):
            sl, hit = onehot(k, blk)
            hit_bf = jnp.where(hit, 1.0, 0.0).astype(BF16)
            prior = _dot(hit_bf, before) + jnp.concatenate([run] * (PBLK // LANE), axis=1)
            rank_ref[pl.ds(k, 1), sl] = jnp.sum(jnp.where(hit, prior, 0.0), axis=0, keepdims=True)
            return run + _dot(hit_bf, ones)
        run = lax.fori_loop(0, nblk, count_block, run)

    cnt_ref[...] = run.astype(jnp.int32)
    hi = jnp.floor(run * (1.0 / LANE))
    lo = run - hi * LANE
    below = jnp.where(lax.broadcasted_iota(jnp.int32, (N_EXPERTS, N_EXPERTS), 1)
                      < lax.broadcasted_iota(jnp.int32, (N_EXPERTS, N_EXPERTS), 0), 1.0, 0.0).astype(BF16)
    off = _dot(below, hi.astype(BF16)) * LANE + _dot(below, lo.astype(BF16))
    off = jnp.concatenate([off] * (PBLK // LANE), axis=1)

    for k in range(2):
        def place_block(blk, carry, k=k):
            sl, hit = onehot(k, blk)
            base = jnp.sum(jnp.where(hit, off, 0.0), axis=0, keepdims=True)
            pos_ref[pl.ds(k, 1), sl] = (base + rank_ref[pl.ds(k, 1), sl]).astype(jnp.int32)
            return carry
        lax.fori_loop(0, nblk, place_block, 0)


def _route(logits_t):
    n = logits_t.shape[1]
    return pl.pallas_call(
        _route_kernel,
        out_shape=[
            jax.ShapeDtypeStruct((2, n), jnp.int32),
            jax.ShapeDtypeStruct((2, n), F32),
            jax.ShapeDtypeStruct((2, n), jnp.int32),
            jax.ShapeDtypeStruct((N_EXPERTS, LANE), jnp.int32),
        ],
        scratch_shapes=[pltpu.VMEM((2, n), F32)],
        compiler_params=pltpu.CompilerParams(vmem_limit_bytes=V7X_VMEM_LIMIT),
        name="route",
    )(logits_t)


def _row_copy(src_ref, src_row, dst_ref, dst_row, sem):
    return pltpu.make_async_copy(src_ref.at[pl.ds(src_row, 1)], dst_ref.at[pl.ds(dst_row, 1)], sem)


ISSUE_UNROLL = 8


def _dispatch_kernel(pos_ref, h_ref, xs_ref, sem):
    def issue(r, c):
        for k in range(2):
            _row_copy(h_ref, r, xs_ref, pos_ref[0, k, r], sem).start(priority=k)
        return c

    lax.fori_loop(0, TOK_TILE, issue, 0, unroll=ISSUE_UNROLL)
    for k in range(2):
        pltpu.make_async_copy(h_ref, xs_ref.at[pl.ds(0, TOK_TILE)], sem).wait()


def _dispatch(pos_tiles, h1):
    n = h1.shape[0]
    return pl.pallas_call(
        _dispatch_kernel,
        grid=(n // TOK_TILE,),
        in_specs=[
            pl.BlockSpec((1, 2, TOK_TILE), lambda i: (i, 0, 0), memory_space=pltpu.SMEM),
            pl.BlockSpec((TOK_TILE, D_MODEL), lambda i: (i, 0)),
        ],
        out_specs=pl.BlockSpec(memory_space=pl.ANY),
        out_shape=jax.ShapeDtypeStruct((2 * n, D_MODEL), F32),
        scratch_shapes=[pltpu.SemaphoreType.DMA],
        compiler_params=_params("arbitrary"),
        name="dispatch",
    )(pos_tiles, h1)


def _moe_kernel(tile_ref, exp_ref, lo_ref, first_ref, slot_ref, next_ref, n_ref,
                xs_ref, wg_hbm, wu_hbm, wd_hbm, y_ref,
                wg_buf, wu_buf, wd_buf, wg_bf, wu_bf, wd_bf, sem):
    i = pl.program_id(0)

    def weight_copies(expert, slot):
        return (pltpu.make_async_copy(wg_hbm.at[expert], wg_buf.at[slot], sem.at[slot, 0]),
                pltpu.make_async_copy(wu_hbm.at[expert], wu_buf.at[slot], sem.at[slot, 1]),
                pltpu.make_async_copy(wd_hbm.at[expert], wd_buf.at[slot], sem.at[slot, 2]))

    @pl.when(i == 0)
    def _():
        for copy in weight_copies(exp_ref[0], 0):
            copy.start()

    valid = i < n_ref[0]

    @pl.when(valid & (first_ref[i] == 1))
    def _():
        slot = slot_ref[i]
        for copy in weight_copies(exp_ref[i], slot):
            copy.wait()

        @pl.when(next_ref[i] >= 0)
        def _():
            for copy in weight_copies(next_ref[i], 1 - slot):
                copy.start()

        wg_bf[...] = wg_buf[slot].astype(BF16)
        wu_bf[...] = wu_buf[slot].astype(BF16)
        wd_bf[...] = wd_buf[slot].astype(BF16)

    @pl.when(valid)
    def _():
        x = xs_ref[...].astype(BF16)
        g = _dot(x, wg_bf[...])
        u = _dot(x, wu_bf[...])
        a = (g * jax.nn.sigmoid(g) * u).astype(BF16)
        y = _dot(a, wd_bf[...])
        lo = lo_ref[i]

        @pl.when(lo == 0)
        def _():
            y_ref[...] = y

        @pl.when(lo > 0)
        def _():
            row = lax.broadcasted_iota(jnp.int32, (MOE_TILE, 1), 0)
            y_ref[...] = jnp.where(row >= lo, y, y_ref[...])


def _moe_experts(items, xs, wg, wu, wd):
    rows = xs.shape[0]
    tile_map = lambda i, t, *_: (t[i], 0)
    grid_spec = pltpu.PrefetchScalarGridSpec(
        num_scalar_prefetch=len(items),
        grid=(MAX_ITEMS,),
        in_specs=[
            pl.BlockSpec((MOE_TILE, D_MODEL), tile_map),
            pl.BlockSpec(memory_space=pl.ANY),
            pl.BlockSpec(memory_space=pl.ANY),
            pl.BlockSpec(memory_space=pl.ANY),
        ],
        out_specs=pl.BlockSpec((MOE_TILE, D_MODEL), tile_map),
        scratch_shapes=[
            pltpu.VMEM((2, D_MODEL, D_EXPERT), F32),
            pltpu.VMEM((2, D_MODEL, D_EXPERT), F32),
            pltpu.VMEM((2, D_EXPERT, D_MODEL), F32),
            pltpu.VMEM((D_MODEL, D_EXPERT), BF16),
            pltpu.VMEM((D_MODEL, D_EXPERT), BF16),
            pltpu.VMEM((D_EXPERT, D_MODEL), BF16),
            pltpu.SemaphoreType.DMA((2, 3)),
        ],
    )
    return pl.pallas_call(
        _moe_kernel,
        grid_spec=grid_spec,
        out_shape=jax.ShapeDtypeStruct((rows, D_MODEL), F32),
        compiler_params=_params("arbitrary"),
        name="moe_experts",
    )(*items, xs, wg, wu, wd)


def _work_items(counts):
    i32 = jnp.int32
    off = jnp.cumsum(counts) - counts
    end = off + counts
    first_tile = off // MOE_TILE
    n_e = jnp.where(counts > 0, (end - 1) // MOE_TILE - first_tile + 1, 0)
    item_end = jnp.cumsum(n_e)
    n_items = item_end[-1]
    idx = jnp.minimum(jnp.arange(MAX_ITEMS, dtype=i32), n_items - 1)
    exp = jnp.sum((item_end[None, :] <= idx[:, None]).astype(i32), axis=1)
    item0 = item_end[exp] - n_e[exp]
    tile = first_tile[exp] + idx - item0
    lo = jnp.maximum(off[exp] - tile * MOE_TILE, 0)
    present = counts > 0
    ordinal = jnp.cumsum(present.astype(i32)) - 1
    eid = jnp.arange(N_EXPERTS, dtype=i32)
    later = present[None, :] & (eid[None, :] > eid[:, None])
    nxt = jnp.min(jnp.where(later, eid[None, :], N_EXPERTS), axis=1)
    nxt = jnp.where(nxt == N_EXPERTS, -1, nxt)
    items = (tile, exp, lo, idx == item0, ordinal[exp] % 2, nxt[exp], n_items.reshape(1))
    return tuple(v.astype(i32) for v in items)


def _combine_kernel(pos_ref, pos_next_ref, h_ref, wt_ref, g_ref, b_ref, y_ref, o_ref, buf_ref, sem):
    i = pl.program_id(0)
    slot = i % 2

    def gather(tile_pos_ref, into):
        def issue(r, c):
            for k in range(2):
                _row_copy(y_ref, tile_pos_ref[0, k, r], buf_ref.at[into, k], r, sem.at[into]).start(priority=k)
            return c
        lax.fori_loop(0, TOK_TILE, issue, 0, unroll=ISSUE_UNROLL)

    @pl.when(i == 0)
    def _():
        gather(pos_ref, 0)

    @pl.when(i + 1 < pl.num_programs(0))
    def _():
        gather(pos_next_ref, 1 - slot)

    for k in range(2):
        pltpu.make_async_copy(y_ref.at[pl.ds(0, TOK_TILE)], buf_ref.at[slot, k], sem.at[slot]).wait()
    f = wt_ref[:, 0:1] * buf_ref[slot, 0] + wt_ref[:, 1:2] * buf_ref[slot, 1]
    o_ref[...] = _layer_norm(ALPHA * h_ref[...] + f, g_ref[...], b_ref[...])


def _combine_ln(pos_tiles, h1, wt, g, b, y):
    n = h1.shape[0]
    steps = n // TOK_TILE
    return pl.pallas_call(
        _combine_kernel,
        grid=(steps,),
        in_specs=[
            pl.BlockSpec((1, 2, TOK_TILE), lambda i: (i, 0, 0), memory_space=pltpu.SMEM),
            pl.BlockSpec((1, 2, TOK_TILE), lambda i: (jnp.minimum(i + 1, steps - 1), 0, 0),
                         memory_space=pltpu.SMEM),
            pl.BlockSpec((TOK_TILE, D_MODEL), lambda i: (i, 0)),
            pl.BlockSpec((TOK_TILE, 2), lambda i: (i, 0)),
            pl.BlockSpec((1, D_MODEL), lambda i: (0, 0)),
            pl.BlockSpec((1, D_MODEL), lambda i: (0, 0)),
            pl.BlockSpec(memory_space=pl.ANY),
        ],
        out_specs=pl.BlockSpec((TOK_TILE, D_MODEL), lambda i: (i, 0)),
        out_shape=jax.ShapeDtypeStruct((n, D_MODEL), F32),
        scratch_shapes=[pltpu.VMEM((2, 2, TOK_TILE, D_MODEL), F32), pltpu.SemaphoreType.DMA((2,))],
        compiler_params=_params("arbitrary"),
        name="combine_ln",
    )(pos_tiles, pos_tiles, h1, wt, g, b, y)


def _dup_heads(w):
    d = w.shape[0]
    w = w.reshape(d, N_KV_HEADS, 1, HEAD_DIM)
    return jnp.broadcast_to(w, (d, N_KV_HEADS, 2, HEAD_DIM)).reshape(d, 2 * KV_WIDTH)


def kernel(x, meta_tokens, ln_in_g, ln_in_b, w_in, sinks, conv_w, w_branch_attn, w_branch_conv, w_out,
           ln1_g, ln1_b, router_group_w, router_group_b, router_expert_w, router_expert_b,
           w_gate, w_up, w_down, ln2_g, ln2_b):
    batch, seq, d = x.shape
    n = batch * seq
    assert d == D_MODEL and seq % ROW_TILE == 0 and w_in.shape[0] == DEPTH == 1
    row = lambda v: v.reshape(1, -1).astype(F32)
    x2d = x.reshape(n, d)
    g0, b0 = row(ln_in_g), row(ln_in_b)

    win = w_in[0]
    kcol, vcol, ccol = D_MODEL, D_MODEL + KV_WIDTH, D_MODEL + 2 * KV_WIDTH
    wkv_dup = jnp.concatenate([_dup_heads(win[:, kcol:vcol]), _dup_heads(win[:, vcol:ccol])], axis=1).astype(BF16)
    wall = _pack_weights(win, w_branch_attn[0], w_branch_conv[0], w_out[0])
    rw = jnp.zeros((ROUTE_ROWS, d), F32)
    rw = rw.at[:N_GROUPS].set(router_group_w[0].T).at[EXPERT_ROW0:EXPERT_ROW0 + N_EXPERTS].set(router_expert_w[0].T)
    rb = jnp.zeros((ROUTE_ROWS,), F32)
    rb = rb.at[:N_GROUPS].set(router_group_b[0]).at[EXPERT_ROW0:EXPERT_ROW0 + N_EXPERTS].set(router_expert_b[0])
    rb = jnp.broadcast_to(rb[:, None], (ROUTE_ROWS, LANE))

    kvm, um = _meta_prep(meta_tokens.astype(F32), g0, b0, wkv_dup, wall)
    hb, q, kd, vd = _ln_qkv(x2d, g0, b0, wall, wkv_dup)
    attn = _swa_attn(sinks[0].astype(F32), q, kd, vd, kvm, batch, seq // BLOCK)
    h1, logits_t = _mixer_tail(x2d, hb, attn, g0, b0, um, conv_w[0], wall,
                               row(ln1_g[0]), row(ln1_b[0]), rw.astype(BF16), rb, seq)
    _, top_w, pos, counts = _route(logits_t)
    pos_tiles = pos.reshape(2, n // TOK_TILE, TOK_TILE).transpose(1, 0, 2)
    xs = _dispatch(pos_tiles, h1)
    y = _moe_experts(_work_items(counts[:, 0]), xs, w_gate[0], w_up[0], w_down[0])
    out = _combine_ln(pos_tiles, h1, top_w.T, row(ln2_g[0]), row(ln2_b[0]), y)
    return out.reshape(batch, seq, d)
```

```python
import functools

import jax
import jax.numpy as jnp
from jax import lax
from jax.experimental import pallas as pl
from jax.experimental.pallas import tpu as pltpu

D_MODEL = 2048
N_META = 16
HEAD_DIM = 64
N_Q_HEADS = 32
N_KV_HEADS = 4
BLOCK = 128
KV_WIDTH = N_KV_HEADS * HEAD_DIM
CONV_K = 3
N_GROUPS = 4
EXPERTS_PER_GROUP = 8
N_EXPERTS = N_GROUPS * EXPERTS_PER_GROUP
D_EXPERT = D_MODEL // 4
LN_EPS = 1e-5
DEPTH = 1
ALPHA = (2.0 * DEPTH) ** 0.25
NEG_INF = -1e30
Q_SCALE = HEAD_DIM ** -0.5

V7X_VMEM_LIMIT = 56 * 1024 * 1024
LANE = 128
CHUNK = 512
N_CHUNKS = D_MODEL // CHUNK
ROW_TILE = 512
MOE_TILE = 256
TOK_TILE = 256
MAX_ITEMS = 2 * 8192 // MOE_TILE + N_EXPERTS
ROUTE_ROWS = 128
EXPERT_ROW0 = 8

BF16 = jnp.bfloat16
F32 = jnp.float32


def _layer_norm(x, g, b):
    mu = jnp.mean(x, axis=-1, keepdims=True)
    xc = x - mu
    var = jnp.mean(xc * xc, axis=-1, keepdims=True)
    return xc * lax.rsqrt(var + LN_EPS) * g + b


def _dot(a, b):
    return jnp.dot(a, b, preferred_element_type=F32)


def _dot_nt(a, b):
    return lax.dot_general(a, b, (((1,), (1,)), ((), ())), preferred_element_type=F32)


def _params(*sem):
    return pltpu.CompilerParams(dimension_semantics=sem, vmem_limit_bytes=V7X_VMEM_LIMIT)


W_B, W_C, W_H, W_GA, W_GC, W_BA, W_BC, W_OUT, W_Q = (k * N_CHUNKS for k in range(9))
N_WBLOCKS = 9 * N_CHUNKS
IN_FIRST_MIX = (D_MODEL + 2 * KV_WIDTH) // CHUNK


def _pack_kernel(win_ref, ba_ref, bc_ref, out_ref, o_ref):
    j = pl.program_id(0)

    @pl.when((j < W_BA) | (j >= W_Q))
    def _():
        o_ref[...] = win_ref[...].astype(BF16)

    @pl.when((j >= W_BA) & (j < W_BC))
    def _():
        o_ref[...] = ba_ref[...].astype(BF16)

    @pl.when((j >= W_BC) & (j < W_OUT))
    def _():
        o_ref[...] = bc_ref[...].astype(BF16)

    @pl.when((j >= W_OUT) & (j < W_Q))
    def _():
        o_ref[...] = out_ref[...].astype(BF16)


def _pack_weights(win, wba, wbc, wout):
    held = lambda first: (lambda j: (0, jnp.clip(j - first, 0, N_CHUNKS - 1)))
    win_map = lambda j: (0, jnp.where(j >= W_Q, j - W_Q, jnp.minimum(j, W_BA - 1) + IN_FIRST_MIX))
    return pl.pallas_call(
        _pack_kernel,
        grid=(N_WBLOCKS,),
        in_specs=[
            pl.BlockSpec((D_MODEL, CHUNK), win_map),
            pl.BlockSpec((D_MODEL, CHUNK), held(W_BA)),
            pl.BlockSpec((D_MODEL, CHUNK), held(W_BC)),
            pl.BlockSpec((D_MODEL, CHUNK), held(W_OUT)),
        ],
        out_specs=pl.BlockSpec((D_MODEL, CHUNK), lambda j: (0, j)),
        out_shape=jax.ShapeDtypeStruct((D_MODEL, N_WBLOCKS * CHUNK), BF16),
        compiler_params=_params("arbitrary"),
        name="pack_weights",
    )(win, wba, wbc, wout)


def _meta_kernel(meta_ref, g_ref, b_ref, wkv_ref, wcc_ref, wch_ref, kv_ref, u_ref):
    hm = _layer_norm(meta_ref[...], g_ref[...], b_ref[...]).astype(BF16)
    kv_ref[...] = _dot(hm, wkv_ref[...]).astype(BF16)
    u_ref[...] = _dot(hm, wcc_ref[...]) * _dot(hm, wch_ref[...])


def _meta_prep(meta, g, b, wkv_dup, wall):
    n_kv = wkv_dup.shape[1] // N_CHUNKS
    return pl.pallas_call(
        _meta_kernel,
        grid=(N_CHUNKS,),
        in_specs=[
            pl.BlockSpec((N_META, D_MODEL), lambda j: (0, 0)),
            pl.BlockSpec((1, D_MODEL), lambda j: (0, 0)),
            pl.BlockSpec((1, D_MODEL), lambda j: (0, 0)),
            pl.BlockSpec((D_MODEL, n_kv), lambda j: (0, j)),
            pl.BlockSpec((D_MODEL, CHUNK), lambda j: (0, W_C + j)),
            pl.BlockSpec((D_MODEL, CHUNK), lambda j: (0, W_H + j)),
        ],
        out_specs=[
            pl.BlockSpec((N_META, n_kv), lambda j: (0, j)),
            pl.BlockSpec((N_META, CHUNK), lambda j: (0, j)),
        ],
        out_shape=[
            jax.ShapeDtypeStruct((N_META, wkv_dup.shape[1]), BF16),
            jax.ShapeDtypeStruct((N_META, D_MODEL), F32),
        ],
        compiler_params=_params("arbitrary"),
        name="meta_prep",
    )(meta, g, b, wkv_dup, wall, wall)


def _qkv_kernel(x_ref, g_ref, b_ref, wq_ref, wkv_ref, hb_ref, q_ref, k_ref, v_ref):
    hb = _layer_norm(x_ref[...], g_ref[...], b_ref[...]).astype(BF16)
    kw = 2 * KV_WIDTH
    hb_ref[...] = hb
    q_ref[...] = (_dot(hb, wq_ref[...]) * Q_SCALE).astype(BF16)
    k_ref[...] = _dot(hb, wkv_ref[:, :kw]).astype(BF16)
    v_ref[...] = _dot(hb, wkv_ref[:, kw:]).astype(BF16)


def _ln_qkv(x2d, g, b, wall, wkv_dup):
    n = x2d.shape[0]
    kw = 2 * KV_WIDTH
    return pl.pallas_call(
        _qkv_kernel,
        grid=(n // ROW_TILE,),
        in_specs=[
            pl.BlockSpec((ROW_TILE, D_MODEL), lambda i: (i, 0)),
            pl.BlockSpec((1, D_MODEL), lambda i: (0, 0)),
            pl.BlockSpec((1, D_MODEL), lambda i: (0, 0)),
            pl.BlockSpec((D_MODEL, D_MODEL), lambda i: (0, W_Q // N_CHUNKS)),
            pl.BlockSpec(wkv_dup.shape, lambda i: (0, 0)),
        ],
        out_specs=[
            pl.BlockSpec((ROW_TILE, D_MODEL), lambda i: (i, 0)),
            pl.BlockSpec((ROW_TILE, D_MODEL), lambda i: (i, 0)),
            pl.BlockSpec((ROW_TILE, kw), lambda i: (i, 0)),
            pl.BlockSpec((ROW_TILE, kw), lambda i: (i, 0)),
        ],
        out_shape=[
            jax.ShapeDtypeStruct((n, D_MODEL), BF16),
            jax.ShapeDtypeStruct((n, D_MODEL), BF16),
            jax.ShapeDtypeStruct((n, kw), BF16),
            jax.ShapeDtypeStruct((n, kw), BF16),
        ],
        compiler_params=_params("arbitrary"),
        name="ln_qkv",
    )(x2d, g, b, wall, wkv_dup)


PAIRS = N_Q_HEADS // N_KV_HEADS // 2
QROWS = PAIRS * BLOCK


def _attn_kernel(sink_ref, q_ref, kp_ref, kc_ref, vp_ref, vc_ref, kvm_ref, o_ref):
    blk = pl.program_id(1)
    lane = lax.broadcasted_iota(jnp.int32, (BLOCK, LANE), 1)
    qi = lax.broadcasted_iota(jnp.int32, (QROWS, BLOCK), 0) % BLOCK
    kj = lax.broadcasted_iota(jnp.int32, (QROWS, BLOCK), 1)
    from_prev = kj > qi
    has_prev = blk > 0
    is_meta = kj < N_META
    rowpair = lax.broadcasted_iota(jnp.int32, (QROWS, 1), 0) // BLOCK
    zero = jnp.zeros((), BF16)
    zpad = jnp.zeros((BLOCK - N_META, LANE), BF16)

    for g in range(N_KV_HEADS):
        gs = slice(g * LANE, (g + 1) * LANE)
        vs = slice((N_KV_HEADS + g) * LANE, (N_KV_HEADS + g + 1) * LANE)
        qs = jnp.concatenate(
            [q_ref[:, (g * PAIRS + p) * LANE:(g * PAIRS + p + 1) * LANE] for p in range(PAIRS)], axis=0)
        keys = (kp_ref[:, gs], kc_ref[:, gs], jnp.concatenate([kvm_ref[:, gs], zpad], axis=0))
        vals = (vp_ref[:, gs], vc_ref[:, gs], jnp.concatenate([kvm_ref[:, vs], zpad], axis=0))
        acc = None
        for half in range(2):
            keep = (lane >= HEAD_DIM) if half else (lane < HEAD_DIM)
            s = _dot_nt(qs, jnp.concatenate([jnp.where(keep, k, zero) for k in keys], axis=0))
            s_prev = jnp.where(has_prev, s[:, :BLOCK], NEG_INF)
            s_band = jnp.where(from_prev, s_prev, s[:, BLOCK:2 * BLOCK])
            s_meta = jnp.where(is_meta, s[:, 2 * BLOCK:], NEG_INF)
            sink = jnp.zeros((QROWS, 1), F32)
            for p in range(PAIRS):
                sink = jnp.where(rowpair == p, sink_ref[g * 2 * PAIRS + 2 * p + half], sink)
            m = jnp.maximum(jnp.max(jnp.maximum(s_band, s_meta), axis=-1, keepdims=True), sink)
            e_band = jnp.exp(s_band - m)
            e_meta = jnp.exp(s_meta - m)
            den = jnp.sum(e_band + e_meta, axis=-1, keepdims=True) + jnp.exp(sink - m)
            p_band = e_band / den
            probs = jnp.concatenate([jnp.where(from_prev, p_band, 0.0).astype(BF16),
                                     jnp.where(from_prev, 0.0, p_band).astype(BF16),
                                     (e_meta / den).astype(BF16)], axis=1)
            part = _dot(probs, jnp.concatenate([jnp.where(keep, v, zero) for v in vals], axis=0))
            acc = part if acc is None else acc + part
        for p in range(PAIRS):
            col = (g * PAIRS + p) * LANE
            o_ref[:, col:col + LANE] = acc[p * BLOCK:(p + 1) * BLOCK].astype(BF16)


def _swa_attn(sinks, q, kd, vd, kvm, batch, nblk):
    n = q.shape[0]
    kw = 2 * KV_WIDTH
    cur = lambda b, i: (b * nblk + i, 0)
    prev = lambda b, i: (b * nblk + jnp.maximum(i - 1, 0), 0)
    return pl.pallas_call(
        _attn_kernel,
        grid=(batch, nblk),
        in_specs=[
            pl.BlockSpec(memory_space=pltpu.SMEM),
            pl.BlockSpec((BLOCK, D_MODEL), cur),
            pl.BlockSpec((BLOCK, kw), prev),
            pl.BlockSpec((BLOCK, kw), cur),
            pl.BlockSpec((BLOCK, kw), prev),
            pl.BlockSpec((BLOCK, kw), cur),
            pl.BlockSpec((N_META, 2 * kw), lambda b, i: (0, 0)),
        ],
        out_specs=pl.BlockSpec((BLOCK, D_MODEL), cur),
        out_shape=jax.ShapeDtypeStruct((n, D_MODEL), BF16),
        compiler_params=_params("arbitrary", "arbitrary"),
        name="swa_attn",
    )(sinks, q, kd, kd, vd, vd, kvm)


TAIL_STEPS = 2 * N_CHUNKS + 1


def _lane_concat(ref):
    return jnp.concatenate([ref[k] for k in range(ref.shape[0])], axis=1)


def _tail_kernel(x_ref, hb_ref, a_ref, g0_ref, b0_ref, um_ref, cw_ref, wa_ref, wb_ref, wc_ref, wd_ref,
                 g1_ref, b1_ref, rw_ref, rb_ref, h1_ref, lg_ref,
                 uext_ref, carry_ref, z_ref, m_ref, *, tiles_per_seq):
    i = pl.program_id(0)
    s = pl.program_id(1)
    tm = ROW_TILE

    @pl.when((i == 0) & (s == 0))
    def _():
        carry_ref[...] = jnp.zeros_like(carry_ref)

    @pl.when(s < N_CHUNKS)
    def _():
        hb = hb_ref[...]
        u = _dot(hb, wa_ref[...]) * _dot(hb, wb_ref[...])
        first = i % tiles_per_seq == 0
        uext_ref[0:8, :] = jnp.where(first, um_ref[N_META - 8:N_META, :], carry_ref[s])
        uext_ref[8:tm + 8, :] = u
        conv = (cw_ref[0:1, :] * uext_ref[6:tm + 6, :] + cw_ref[1:2, :] * uext_ref[7:tm + 7, :]
                + cw_ref[2:3, :] * u)
        carry_ref[s] = u[tm - 8:tm, :]
        z_ref[s] = (_dot(hb, wc_ref[...]) * conv).astype(BF16)

    @pl.when((s >= N_CHUNKS) & (s < 2 * N_CHUNKS))
    def _():
        hb = hb_ref[...]
        conv_part = jax.nn.sigmoid(_dot(hb, wb_ref[...])) * _dot(_lane_concat(z_ref), wd_ref[...])
        gate_attn = jax.nn.sigmoid(_dot(hb, wa_ref[...]))
        m_ref[s - N_CHUNKS] = (gate_attn * _dot(a_ref[...], wc_ref[...]) + conv_part).astype(BF16)

    @pl.when(s == 2 * N_CHUNKS)
    def _():
        h = _layer_norm(x_ref[...], g0_ref[...], b0_ref[...])
        m = _lane_concat(m_ref)
        for k, w_ref in enumerate((wa_ref, wb_ref, wc_ref, wd_ref)):
            cs = slice(k * CHUNK, (k + 1) * CHUNK)
            h1_ref[:, cs] = ALPHA * h[:, cs] + _dot(m, w_ref[...])
        h1 = _layer_norm(h1_ref[...], g1_ref[...], b1_ref[...])
        h1_ref[...] = h1
        lg_ref[...] = _dot_nt(rw_ref[...], h1.astype(BF16)) + rb_ref[...][:, 0:1]


def _mixer_tail(x2d, hb, attn, g0, b0, um, cw, wall, g1, b1, rw, rb, seq):
    n = x2d.shape[0]
    nc = N_CHUNKS

    def slot(conv_first, merge_first, out_block, hold=False):
        def index(i, s):
            conv = conv_first if hold else conv_first + s
            return 0, jnp.where(s < nc, conv, jnp.where(s < 2 * nc, merge_first + s - nc, W_OUT + out_block))
        return pl.BlockSpec((D_MODEL, CHUNK), index)

    lo = lambda i, s: (0, jnp.minimum(s, nc - 1))
    const = lambda i, s: (0, 0)
    row = lambda i, s: (i, 0)
    return pl.pallas_call(
        functools.partial(_tail_kernel, tiles_per_seq=seq // ROW_TILE),
        grid=(n // ROW_TILE, TAIL_STEPS),
        in_specs=[
            pl.BlockSpec((ROW_TILE, D_MODEL), row),
            pl.BlockSpec((ROW_TILE, D_MODEL), row),
            pl.BlockSpec((ROW_TILE, D_MODEL), row),
            pl.BlockSpec((1, D_MODEL), const),
            pl.BlockSpec((1, D_MODEL), const),
            pl.BlockSpec((N_META, CHUNK), lo),
            pl.BlockSpec((CONV_K, CHUNK), lo),
            slot(W_C, W_GA, 0),
            slot(W_H, W_GC, 1),
            slot(W_B, W_BA, 2),
            slot(W_BC, W_BC, 3, hold=True),
            pl.BlockSpec((1, D_MODEL), const),
            pl.BlockSpec((1, D_MODEL), const),
            pl.BlockSpec((ROUTE_ROWS, D_MODEL), const),
            pl.BlockSpec((ROUTE_ROWS, LANE), const),
        ],
        out_specs=[
            pl.BlockSpec((ROW_TILE, D_MODEL), row),
            pl.BlockSpec((ROUTE_ROWS, ROW_TILE), lambda i, s: (0, i)),
        ],
        out_shape=[
            jax.ShapeDtypeStruct((n, D_MODEL), F32),
            jax.ShapeDtypeStruct((ROUTE_ROWS, n), F32),
        ],
        scratch_shapes=[
            pltpu.VMEM((ROW_TILE + 8, CHUNK), F32),
            pltpu.VMEM((nc, 8, CHUNK), F32),
            pltpu.VMEM((nc, ROW_TILE, CHUNK), BF16),
            pltpu.VMEM((nc, ROW_TILE, CHUNK), BF16),
        ],
        compiler_params=_params("arbitrary", "arbitrary"),
        name="mixer_tail",
    )(x2d, hb, attn, g0, b0, um, cw, wall, wall, wall, wall, g1, b1, rw, rb)


RCHUNK = 1024
PBLK = 256


def _route_kernel(lg_ref, e_ref, w_ref, pos_ref, cnt_ref, rank_ref):
    n = lg_ref.shape[1]
    epg = EXPERTS_PER_GROUP

    def route_chunk(c, carry):
        sl = pl.ds(pl.multiple_of(c * RCHUNK, RCHUNK), RCHUNK)
        gl = [lg_ref[gi:gi + 1, sl] for gi in range(N_GROUPS)]
        gmax = functools.reduce(jnp.maximum, gl)
        ge = [jnp.exp(v - gmax) for v in gl]
        gsum = functools.reduce(jnp.add, ge)
        gp = [v / gsum for v in ge]
        p_group = functools.reduce(jnp.maximum, gp)
        gsel = jnp.full_like(p_group, float(N_GROUPS - 1))
        for gi in range(N_GROUPS - 2, -1, -1):
            gsel = jnp.where(gp[gi] >= p_group, float(gi), gsel)
        e_in = lg_ref[EXPERT_ROW0 + (N_GROUPS - 1) * epg:EXPERT_ROW0 + N_GROUPS * epg, sl]
        for gi in range(N_GROUPS - 2, -1, -1):
            e_in = jnp.where(gsel == float(gi),
                             lg_ref[EXPERT_ROW0 + gi * epg:EXPERT_ROW0 + (gi + 1) * epg, sl], e_in)
        ee = jnp.exp(e_in - jnp.max(e_in, axis=0, keepdims=True))
        pe = ee / jnp.sum(ee, axis=0, keepdims=True)
        ridx = lax.broadcasted_iota(jnp.int32, pe.shape, 0).astype(F32)
        p0 = jnp.max(pe, axis=0, keepdims=True)
        i0 = jnp.min(jnp.where(pe >= p0, ridx, float(epg)), axis=0, keepdims=True)
        pe2 = jnp.where(ridx == i0, -1.0, pe)
        p1 = jnp.max(pe2, axis=0, keepdims=True)
        i1 = jnp.min(jnp.where(pe2 >= p1, ridx, float(epg)), axis=0, keepdims=True)
        den = p0 + p1
        e_ref[0:1, sl] = (gsel * epg + i0).astype(jnp.int32)
        e_ref[1:2, sl] = (gsel * epg + i1).astype(jnp.int32)
        w_ref[0:1, sl] = p0 / den * p_group
        w_ref[1:2, sl] = p1 / den * p_group
        return carry

    lax.fori_loop(0, n // RCHUNK, route_chunk, 0)

    eid = lax.broadcasted_iota(jnp.int32, (N_EXPERTS, PBLK), 0)
    before = jnp.where(lax.broadcasted_iota(jnp.int32, (PBLK, PBLK), 0)
                       < lax.broadcasted_iota(jnp.int32, (PBLK, PBLK), 1), 1.0, 0.0).astype(BF16)
    ones = jnp.ones((PBLK, LANE), BF16)
    nblk = n // PBLK

    def onehot(k, blk):
        sl = pl.ds(pl.multiple_of(blk * PBLK, PBLK), PBLK)
        return sl, eid == e_ref[pl.ds(k, 1), sl]

    run = jnp.zeros((N_EXPERTS, LANE), F32)
    for k in range(2):
        def count_block(blk, run, k=k):
            sl, hit = onehot(k, blk)
            hit_bf = jnp.where(hit, 1.0, 0.0).astype(BF16)
            prior = _dot(hit_bf, before) + jnp.concatenate([run] * (PBLK // LANE), axis=1)
            rank_ref[pl.ds(k, 1), sl] = jnp.sum(jnp.where(hit, prior, 0.0), axis=0, keepdims=True)
            return run + _dot(hit_bf, ones)
        run = lax.fori_loop(0, nblk, count_block, run)

    cnt_ref[...] = run.astype(jnp.int32)
    hi = jnp.floor(run * (1.0 / LANE))
    lo = run - hi * LANE
    below = jnp.where(lax.broadcasted_iota(jnp.int32, (N_EXPERTS, N_EXPERTS), 1)
                      < lax.broadcasted_iota(jnp.int32, (N_EXPERTS, N_EXPERTS), 0), 1.0, 0.0).astype(BF16)
    off = _dot(below, hi.astype(BF16)) * LANE + _dot(below, lo.astype(BF16))
    off = jnp.concatenate([off] * (PBLK // LANE), axis=1)

    for k in range(2):
        def place_block(blk, carry, k=k):
            sl, hit = onehot(k, blk)
            base = jnp.sum(jnp.where(hit, off, 0.0), axis=0, keepdims=True)
            pos_ref[pl.ds(k, 1), sl] = (base + rank_ref[pl.ds(k, 1), sl]).astype(jnp.int32)
            return carry
        lax.fori_loop(0, nblk, place_block, 0)


def _route(logits_t):
    n = logits_t.shape[1]
    return pl.pallas_call(
        _route_kernel,
        out_shape=[
            jax.ShapeDtypeStruct((2, n), jnp.int32),
            jax.ShapeDtypeStruct((2, n), F32),
            jax.ShapeDtypeStruct((2, n), jnp.int32),
            jax.ShapeDtypeStruct((N_EXPERTS, LANE), jnp.int32),
        ],
        scratch_shapes=[pltpu.VMEM((2, n), F32)],
        compiler_params=pltpu.CompilerParams(vmem_limit_bytes=V7X_VMEM_LIMIT),
        name="route",
    )(logits_t)


def _row_copy(src_ref, src_row, dst_ref, dst_row, sem):
    return pltpu.make_async_copy(src_ref.at[pl.ds(src_row, 1)], dst_ref.at[pl.ds(dst_row, 1)], sem)


def _dispatch_kernel(pos_ref, h_ref, xs_ref, sem):
    for r in range(TOK_TILE):
        for k in range(2):
            _row_copy(h_ref, r, xs_ref, pos_ref[0, k, r], sem).start(priority=k)
    for k in range(2):
        pltpu.make_async_copy(h_ref, xs_ref.at[pl.ds(0, TOK_TILE)], sem).wait()


def _dispatch(pos_tiles, h1):
    n = h1.shape[0]
    return pl.pallas_call(
        _dispatch_kernel,
        grid=(n // TOK_TILE,),
        in_specs=[
            pl.BlockSpec((1, 2, TOK_TILE), lambda i: (i, 0, 0), memory_space=pltpu.SMEM),
            pl.BlockSpec((TOK_TILE, D_MODEL), lambda i: (i, 0)),
        ],
        out_specs=pl.BlockSpec(memory_space=pl.ANY),
        out_shape=jax.ShapeDtypeStruct((2 * n, D_MODEL), F32),
        scratch_shapes=[pltpu.SemaphoreType.DMA],
        compiler_params=_params("arbitrary"),
        name="dispatch",
    )(pos_tiles, h1)


def _moe_kernel(tile_ref, exp_ref, lo_ref, first_ref, slot_ref, next_ref, n_ref,
                xs_ref, wg_hbm, wu_hbm, wd_hbm, y_ref,
                wg_buf, wu_buf, wd_buf, ytile_ref, sem):
    i = pl.program_id(0)

    def weight_copies(expert, slot):
        return (pltpu.make_async_copy(wg_hbm.at[expert], wg_buf.at[slot], sem.at[slot, 0]),
                pltpu.make_async_copy(wu_hbm.at[expert], wu_buf.at[slot], sem.at[slot, 1]),
                pltpu.make_async_copy(wd_hbm.at[expert], wd_buf.at[slot], sem.at[slot, 2]))

    @pl.when(i == 0)
    def _():
        ytile_ref[...] = jnp.zeros_like(ytile_ref)
        for copy in weight_copies(exp_ref[0], 0):
            copy.start()

    valid = i < n_ref[0]

    @pl.when(valid & (first_ref[i] == 1))
    def _():
        slot = slot_ref[i]
        for copy in weight_copies(exp_ref[i], slot):
            copy.wait()

        @pl.when(next_ref[i] >= 0)
        def _():
            for copy in weight_copies(next_ref[i], 1 - slot):
                copy.start()

    @pl.when(valid)
    def _():
        slot = slot_ref[i]
        x = xs_ref[...].astype(BF16)
        g = _dot(x, wg_buf[slot].astype(BF16))
        u = _dot(x, wu_buf[slot].astype(BF16))
        a = (g * jax.nn.sigmoid(g) * u).astype(BF16)
        y = _dot(a, wd_buf[slot].astype(BF16))
        row = lax.broadcasted_iota(jnp.int32, (MOE_TILE, 1), 0)
        merged = jnp.where(row >= lo_ref[i], y, ytile_ref[...])
        ytile_ref[...] = merged
        y_ref[...] = merged


def _moe_experts(items, xs, wg, wu, wd):
    rows = xs.shape[0]
    tile_map = lambda i, t, *_: (t[i], 0)
    grid_spec = pltpu.PrefetchScalarGridSpec(
        num_scalar_prefetch=len(items),
        grid=(MAX_ITEMS,),
        in_specs=[
            pl.BlockSpec((MOE_TILE, D_MODEL), tile_map),
            pl.BlockSpec(memory_space=pl.ANY),
            pl.BlockSpec(memory_space=pl.ANY),
            pl.BlockSpec(memory_space=pl.ANY),
        ],
        out_specs=pl.BlockSpec((MOE_TILE, D_MODEL), tile_map),
        scratch_shapes=[
            pltpu.VMEM((2, D_MODEL, D_EXPERT), F32),
            pltpu.VMEM((2, D_MODEL, D_EXPERT), F32),
            pltpu.VMEM((2, D_EXPERT, D_MODEL), F32),
            pltpu.VMEM((MOE_TILE, D_MODEL), F32),
            pltpu.SemaphoreType.DMA((2, 3)),
        ],
    )
    return pl.pallas_call(
        _moe_kernel,
        grid_spec=grid_spec,
        out_shape=jax.ShapeDtypeStruct((rows, D_MODEL), F32),
        compiler_params=_params("arbitrary"),
        name="moe_experts",
    )(*items, xs, wg, wu, wd)


def _work_items(counts):
    i32 = jnp.int32
    off = jnp.cumsum(counts) - counts
    end = off + counts
    first_tile = off // MOE_TILE
    n_e = jnp.where(counts > 0, (end - 1) // MOE_TILE - first_tile + 1, 0)
    item_end = jnp.cumsum(n_e)
    n_items = item_end[-1]
    idx = jnp.minimum(jnp.arange(MAX_ITEMS, dtype=i32), n_items - 1)
    exp = jnp.sum((item_end[None, :] <= idx[:, None]).astype(i32), axis=1)
    item0 = item_end[exp] - n_e[exp]
    tile = first_tile[exp] + idx - item0
    lo = jnp.maximum(off[exp] - tile * MOE_TILE, 0)
    present = counts > 0
    ordinal = jnp.cumsum(present.astype(i32)) - 1
    eid = jnp.arange(N_EXPERTS, dtype=i32)
    later = present[None, :] & (eid[None, :] > eid[:, None])
    nxt = jnp.min(jnp.where(later, eid[None, :], N_EXPERTS), axis=1)
    nxt = jnp.where(nxt == N_EXPERTS, -1, nxt)
    items = (tile, exp, lo, idx == item0, ordinal[exp] % 2, nxt[exp], n_items.reshape(1))
    return tuple(v.astype(i32) for v in items)


def _combine_kernel(pos_ref, pos_next_ref, h_ref, wt_ref, g_ref, b_ref, y_ref, o_ref, buf_ref, sem):
    i = pl.program_id(0)
    slot = i % 2

    def gather(tile_pos_ref, into):
        for r in range(TOK_TILE):
            for k in range(2):
                _row_copy(y_ref, tile_pos_ref[0, k, r], buf_ref.at[into, k], r, sem.at[into]).start(priority=k)

    @pl.when(i == 0)
    def _():
        gather(pos_ref, 0)

    for other in range(2):
        @pl.when((i + 1 < pl.num_programs(0)) & (slot == 1 - other))
        def _(other=other):
            gather(pos_next_ref, other)

    for k in range(2):
        pltpu.make_async_copy(y_ref.at[pl.ds(0, TOK_TILE)], buf_ref.at[slot, k], sem.at[slot]).wait()
    f = wt_ref[:, 0:1] * buf_ref[slot, 0] + wt_ref[:, 1:2] * buf_ref[slot, 1]
    o_ref[...] = _layer_norm(ALPHA * h_ref[...] + f, g_ref[...], b_ref[...])


def _combine_ln(pos_tiles, h1, wt, g, b, y):
    n = h1.shape[0]
    steps = n // TOK_TILE
    return pl.pallas_call(
        _combine_kernel,
        grid=(steps,),
        in_specs=[
            pl.BlockSpec((1, 2, TOK_TILE), lambda i: (i, 0, 0), memory_space=pltpu.SMEM),
            pl.BlockSpec((1, 2, TOK_TILE), lambda i: (jnp.minimum(i + 1, steps - 1), 0, 0),
                         memory_space=pltpu.SMEM),
            pl.BlockSpec((TOK_TILE, D_MODEL), lambda i: (i, 0)),
            pl.BlockSpec((TOK_TILE, 2), lambda i: (i, 0)),
            pl.BlockSpec((1, D_MODEL), lambda i: (0, 0)),
            pl.BlockSpec((1, D_MODEL), lambda i: (0, 0)),
            pl.BlockSpec(memory_space=pl.ANY),
        ],
        out_specs=pl.BlockSpec((TOK_TILE, D_MODEL), lambda i: (i, 0)),
        out_shape=jax.ShapeDtypeStruct((n, D_MODEL), F32),
        scratch_shapes=[pltpu.VMEM((2, 2, TOK_TILE, D_MODEL), F32), pltpu.SemaphoreType.DMA((2,))],
        compiler_params=_params("arbitrary"),
        name="combine_ln",
    )(pos_tiles, pos_tiles, h1, wt, g, b, y)


def _dup_heads(w):
    d = w.shape[0]
    w = w.reshape(d, N_KV_HEADS, 1, HEAD_DIM)
    return jnp.broadcast_to(w, (d, N_KV_HEADS, 2, HEAD_DIM)).reshape(d, 2 * KV_WIDTH)


def kernel(x, meta_tokens, ln_in_g, ln_in_b, w_in, sinks, conv_w, w_branch_attn, w_branch_conv, w_out,
           ln1_g, ln1_b, router_group_w, router_group_b, router_expert_w, router_expert_b,
           w_gate, w_up, w_down, ln2_g, ln2_b):
    batch, seq, d = x.shape
    n = batch * seq
    assert d == D_MODEL and seq % ROW_TILE == 0 and w_in.shape[0] == DEPTH == 1
    row = lambda v: v.reshape(1, -1).astype(F32)
    x2d = x.reshape(n, d)
    g0, b0 = row(ln_in_g), row(ln_in_b)

    win = w_in[0]
    kcol, vcol, ccol = D_MODEL, D_MODEL + KV_WIDTH, D_MODEL + 2 * KV_WIDTH
    wkv_dup = jnp.concatenate([_dup_heads(win[:, kcol:vcol]), _dup_heads(win[:, vcol:ccol])], axis=1).astype(BF16)
    wall = _pack_weights(win, w_branch_attn[0], w_branch_conv[0], w_out[0])
    rw = jnp.zeros((ROUTE_ROWS, d), F32)
    rw = rw.at[:N_GROUPS].set(router_group_w[0].T).at[EXPERT_ROW0:EXPERT_ROW0 + N_EXPERTS].set(router_expert_w[0].T)
    rb = jnp.zeros((ROUTE_ROWS,), F32)
    rb = rb.at[:N_GROUPS].set(router_group_b[0]).at[EXPERT_ROW0:EXPERT_ROW0 + N_EXPERTS].set(router_expert_b[0])
    rb = jnp.broadcast_to(rb[:, None], (ROUTE_ROWS, LANE))

    kvm, um = _meta_prep(meta_tokens.astype(F32), g0, b0, wkv_dup, wall)
    hb, q, kd, vd = _ln_qkv(x2d, g0, b0, wall, wkv_dup)
    attn = _swa_attn(sinks[0].astype(F32), q, kd, vd, kvm, batch, seq // BLOCK)
    h1, logits_t = _mixer_tail(x2d, hb, attn, g0, b0, um, conv_w[0], wall,
                               row(ln1_g[0]), row(ln1_b[0]), rw.astype(BF16), rb, seq)
    _, top_w, pos, counts = _route(logits_t)
    pos_tiles = pos.reshape(2, n // TOK_TILE, TOK_TILE).transpose(1, 0, 2)
    xs = _dispatch(pos_tiles, h1)
    y = _moe_experts(_work_items(counts[:, 0]), xs, w_gate[0], w_up[0], w_down[0])
    out = _combine_ln(pos_tiles, h1, top_w.T, row(ln2_g[0]), row(ln2_b[0]), y)
    return out.reshape(batch, seq, d)
```

```python
import functools

import jax
import jax.numpy as jnp
from jax import lax
from jax.experimental import pallas as pl
from jax.experimental.pallas import tpu as pltpu

D_MODEL = 2048
N_META = 16
HEAD_DIM = 64
N_Q_HEADS = 32
N_KV_HEADS = 4
BLOCK = 128
KV_WIDTH = N_KV_HEADS * HEAD_DIM
CONV_K = 3
N_GROUPS = 4
EXPERTS_PER_GROUP = 8
N_EXPERTS = N_GROUPS * EXPERTS_PER_GROUP
D_EXPERT = D_MODEL // 4
LN_EPS = 1e-5
DEPTH = 1
ALPHA = (2.0 * DEPTH) ** 0.25
NEG_INF = -1e30
Q_SCALE = HEAD_DIM ** -0.5

V7X_VMEM_LIMIT = 56 * 1024 * 1024
LANE = 128
CHUNK = 512
N_CHUNKS = D_MODEL // CHUNK
ROW_TILE = 512
MOE_TILE = 256
TOK_TILE = 256
MAX_ITEMS = 2 * 8192 // MOE_TILE + N_EXPERTS
ROUTE_ROWS = 128
EXPERT_ROW0 = 8

BF16 = jnp.bfloat16
F32 = jnp.float32


def _layer_norm(x, g, b):
    mu = jnp.mean(x, axis=-1, keepdims=True)
    xc = x - mu
    var = jnp.mean(xc * xc, axis=-1, keepdims=True)
    return xc * lax.rsqrt(var + LN_EPS) * g + b


def _dot(a, b):
    return jnp.dot(a, b, preferred_element_type=F32)


def _dot_nt(a, b):
    return lax.dot_general(a, b, (((1,), (1,)), ((), ())), preferred_element_type=F32)


def _params(*sem):
    return pltpu.CompilerParams(dimension_semantics=sem, vmem_limit_bytes=V7X_VMEM_LIMIT)


W_B, W_C, W_H, W_GA, W_GC, W_BA, W_BC, W_OUT, W_Q = (k * N_CHUNKS for k in range(9))
N_WBLOCKS = 9 * N_CHUNKS
IN_FIRST_MIX = (D_MODEL + 2 * KV_WIDTH) // CHUNK


def _pack_kernel(win_ref, ba_ref, bc_ref, out_ref, o_ref):
    j = pl.program_id(0)

    @pl.when((j < W_BA) | (j >= W_Q))
    def _():
        o_ref[...] = win_ref[...].astype(BF16)

    @pl.when((j >= W_BA) & (j < W_BC))
    def _():
        o_ref[...] = ba_ref[...].astype(BF16)

    @pl.when((j >= W_BC) & (j < W_OUT))
    def _():
        o_ref[...] = bc_ref[...].astype(BF16)

    @pl.when((j >= W_OUT) & (j < W_Q))
    def _():
        o_ref[...] = out_ref[...].astype(BF16)


def _pack_weights(win, wba, wbc, wout):
    held = lambda first: (lambda j: (0, jnp.clip(j - first, 0, N_CHUNKS - 1)))
    win_map = lambda j: (0, jnp.where(j >= W_Q, j - W_Q, jnp.minimum(j, W_BA - 1) + IN_FIRST_MIX))
    return pl.pallas_call(
        _pack_kernel,
        grid=(N_WBLOCKS,),
        in_specs=[
            pl.BlockSpec((D_MODEL, CHUNK), win_map),
            pl.BlockSpec((D_MODEL, CHUNK), held(W_BA)),
            pl.BlockSpec((D_MODEL, CHUNK), held(W_BC)),
            pl.BlockSpec((D_MODEL, CHUNK), held(W_OUT)),
        ],
        out_specs=pl.BlockSpec((D_MODEL, CHUNK), lambda j: (0, j)),
        out_shape=jax.ShapeDtypeStruct((D_MODEL, N_WBLOCKS * CHUNK), BF16),
        compiler_params=_params("arbitrary"),
        name="pack_weights",
    )(win, wba, wbc, wout)


def _meta_kernel(meta_ref, g_ref, b_ref, wkv_ref, wcc_ref, wch_ref, kv_ref, u_ref):
    hm = _layer_norm(meta_ref[...], g_ref[...], b_ref[...]).astype(BF16)
    kv_ref[...] = _dot(hm, wkv_ref[...]).astype(BF16)
    u_ref[...] = _dot(hm, wcc_ref[...]) * _dot(hm, wch_ref[...])


def _meta_prep(meta, g, b, wkv_dup, wall):
    n_kv = wkv_dup.shape[1] // N_CHUNKS
    return pl.pallas_call(
        _meta_kernel,
        grid=(N_CHUNKS,),
        in_specs=[
            pl.BlockSpec((N_META, D_MODEL), lambda j: (0, 0)),
            pl.BlockSpec((1, D_MODEL), lambda j: (0, 0)),
            pl.BlockSpec((1, D_MODEL), lambda j: (0, 0)),
            pl.BlockSpec((D_MODEL, n_kv), lambda j: (0, j)),
            pl.BlockSpec((D_MODEL, CHUNK), lambda j: (0, W_C + j)),
            pl.BlockSpec((D_MODEL, CHUNK), lambda j: (0, W_H + j)),
        ],
        out_specs=[
            pl.BlockSpec((N_META, n_kv), lambda j: (0, j)),
            pl.BlockSpec((N_META, CHUNK), lambda j: (0, j)),
        ],
        out_shape=[
            jax.ShapeDtypeStruct((N_META, wkv_dup.shape[1]), BF16),
            jax.ShapeDtypeStruct((N_META, D_MODEL), F32),
        ],
        compiler_params=_params("arbitrary"),
        name="meta_prep",
    )(meta, g, b, wkv_dup, wall, wall)


def _qkv_kernel(x_ref, g_ref, b_ref, wq_ref, wkv_ref, hb_ref, q_ref, k_ref, v_ref):
    hb = _layer_norm(x_ref[...], g_ref[...], b_ref[...]).astype(BF16)
    kw = 2 * KV_WIDTH
    hb_ref[...] = hb
    q_ref[...] = (_dot(hb, wq_ref[...]) * Q_SCALE).astype(BF16)
    k_ref[...] = _dot(hb, wkv_ref[:, :kw]).astype(BF16)
    v_ref[...] = _dot(hb, wkv_ref[:, kw:]).astype(BF16)


def _ln_qkv(x2d, g, b, wall, wkv_dup):
    n = x2d.shape[0]
    kw = 2 * KV_WIDTH
    return pl.pallas_call(
        _qkv_kernel,
        grid=(n // ROW_TILE,),
        in_specs=[
            pl.BlockSpec((ROW_TILE, D_MODEL), lambda i: (i, 0)),
            pl.BlockSpec((1, D_MODEL), lambda i: (0, 0)),
            pl.BlockSpec((1, D_MODEL), lambda i: (0, 0)),
            pl.BlockSpec((D_MODEL, D_MODEL), lambda i: (0, W_Q // N_CHUNKS)),
            pl.BlockSpec(wkv_dup.shape, lambda i: (0, 0)),
        ],
        out_specs=[
            pl.BlockSpec((ROW_TILE, D_MODEL), lambda i: (i, 0)),
            pl.BlockSpec((ROW_TILE, D_MODEL), lambda i: (i, 0)),
            pl.BlockSpec((ROW_TILE, kw), lambda i: (i, 0)),
            pl.BlockSpec((ROW_TILE, kw), lambda i: (i, 0)),
        ],
        out_shape=[
            jax.ShapeDtypeStruct((n, D_MODEL), BF16),
            jax.ShapeDtypeStruct((n, D_MODEL), BF16),
            jax.ShapeDtypeStruct((n, kw), BF16),
            jax.ShapeDtypeStruct((n, kw), BF16),
        ],
        compiler_params=_params("arbitrary"),
        name="ln_qkv",
    )(x2d, g, b, wall, wkv_dup)


PAIRS = N_Q_HEADS // N_KV_HEADS // 2
QROWS = PAIRS * BLOCK


def _attn_kernel(sink_ref, q_ref, kp_ref, kc_ref, vp_ref, vc_ref, kvm_ref, o_ref):
    blk = pl.program_id(1)
    lane = lax.broadcasted_iota(jnp.int32, (BLOCK, LANE), 1)
    qi = lax.broadcasted_iota(jnp.int32, (QROWS, BLOCK), 0) % BLOCK
    kj = lax.broadcasted_iota(jnp.int32, (QROWS, BLOCK), 1)
    from_prev = kj > qi
    has_prev = blk > 0
    is_meta = kj < N_META
    rowpair = lax.broadcasted_iota(jnp.int32, (QROWS, 1), 0) // BLOCK
    zero = jnp.zeros((), BF16)
    zpad = jnp.zeros((BLOCK - N_META, LANE), BF16)

    for g in range(N_KV_HEADS):
        gs = slice(g * LANE, (g + 1) * LANE)
        vs = slice((N_KV_HEADS + g) * LANE, (N_KV_HEADS + g + 1) * LANE)
        qs = jnp.concatenate(
            [q_ref[:, (g * PAIRS + p) * LANE:(g * PAIRS + p + 1) * LANE] for p in range(PAIRS)], axis=0)
        keys = (kp_ref[:, gs], kc_ref[:, gs], jnp.concatenate([kvm_ref[:, gs], zpad], axis=0))
        vals = (vp_ref[:, gs], vc_ref[:, gs], jnp.concatenate([kvm_ref[:, vs], zpad], axis=0))
        acc = None
        for half in range(2):
            keep = (lane >= HEAD_DIM) if half else (lane < HEAD_DIM)
            s = _dot_nt(qs, jnp.concatenate([jnp.where(keep, k, zero) for k in keys], axis=0))
            s_prev = jnp.where(has_prev, s[:, :BLOCK], NEG_INF)
            s_band = jnp.where(from_prev, s_prev, s[:, BLOCK:2 * BLOCK])
            s_meta = jnp.where(is_meta, s[:, 2 * BLOCK:], NEG_INF)
            sink = jnp.zeros((QROWS, 1), F32)
            for p in range(PAIRS):
                sink = jnp.where(rowpair == p, sink_ref[g * 2 * PAIRS + 2 * p + half], sink)
            m = jnp.maximum(jnp.max(jnp.maximum(s_band, s_meta), axis=-1, keepdims=True), sink)
            e_band = jnp.exp(s_band - m)
            e_meta = jnp.exp(s_meta - m)
            den = jnp.sum(e_band + e_meta, axis=-1, keepdims=True) + jnp.exp(sink - m)
            p_band = e_band / den
            probs = jnp.concatenate([jnp.where(from_prev, p_band, 0.0).astype(BF16),
                                     jnp.where(from_prev, 0.0, p_band).astype(BF16),
                                     (e_meta / den).astype(BF16)], axis=1)
            part = _dot(probs, jnp.concatenate([jnp.where(keep, v, zero) for v in vals], axis=0))
            acc = part if acc is None else acc + part
        for p in range(PAIRS):
            col = (g * PAIRS + p) * LANE
            o_ref[:, col:col + LANE] = acc[p * BLOCK:(p + 1) * BLOCK].astype(BF16)


def _swa_attn(sinks, q, kd, vd, kvm, batch, nblk):
    n = q.shape[0]
    kw = 2 * KV_WIDTH
    cur = lambda b, i: (b * nblk + i, 0)
    prev = lambda b, i: (b * nblk + jnp.maximum(i - 1, 0), 0)
    return pl.pallas_call(
        _attn_kernel,
        grid=(batch, nblk),
        in_specs=[
            pl.BlockSpec(memory_space=pltpu.SMEM),
            pl.BlockSpec((BLOCK, D_MODEL), cur),
            pl.BlockSpec((BLOCK, kw), prev),
            pl.BlockSpec((BLOCK, kw), cur),
            pl.BlockSpec((BLOCK, kw), prev),
            pl.BlockSpec((BLOCK, kw), cur),
            pl.BlockSpec((N_META, 2 * kw), lambda b, i: (0, 0)),
        ],
        out_specs=pl.BlockSpec((BLOCK, D_MODEL), cur),
        out_shape=jax.ShapeDtypeStruct((n, D_MODEL), BF16),
        compiler_params=_params("arbitrary", "arbitrary"),
        name="swa_attn",
    )(sinks, q, kd, kd, vd, vd, kvm)


TAIL_STEPS = 2 * N_CHUNKS + 1


def _lane_concat(ref):
    return jnp.concatenate([ref[k] for k in range(ref.shape[0])], axis=1)


def _tail_kernel(x_ref, hb_ref, a_ref, g0_ref, b0_ref, um_ref, cw_ref, wa_ref, wb_ref, wc_ref, wd_ref,
                 g1_ref, b1_ref, rw_ref, rb_ref, h1_ref, lg_ref,
                 uext_ref, carry_ref, z_ref, m_ref, *, tiles_per_seq):
    i = pl.program_id(0)
    s = pl.program_id(1)
    tm = ROW_TILE

    @pl.when((i == 0) & (s == 0))
    def _():
        carry_ref[...] = jnp.zeros_like(carry_ref)

    @pl.when(s < N_CHUNKS)
    def _():
        hb = hb_ref[...]
        u = _dot(hb, wa_ref[...]) * _dot(hb, wb_ref[...])
        first = i % tiles_per_seq == 0
        uext_ref[0:8, :] = jnp.where(first, um_ref[N_META - 8:N_META, :], carry_ref[s])
        uext_ref[8:tm + 8, :] = u
        conv = (cw_ref[0:1, :] * uext_ref[6:tm + 6, :] + cw_ref[1:2, :] * uext_ref[7:tm + 7, :]
                + cw_ref[2:3, :] * u)
        carry_ref[s] = u[tm - 8:tm, :]
        z_ref[s] = (_dot(hb, wc_ref[...]) * conv).astype(BF16)

    @pl.when((s >= N_CHUNKS) & (s < 2 * N_CHUNKS))
    def _():
        hb = hb_ref[...]
        conv_part = jax.nn.sigmoid(_dot(hb, wb_ref[...])) * _dot(_lane_concat(z_ref), wd_ref[...])
        gate_attn = jax.nn.sigmoid(_dot(hb, wa_ref[...]))
        m_ref[s - N_CHUNKS] = (gate_attn * _dot(a_ref[...], wc_ref[...]) + conv_part).astype(BF16)

    @pl.when(s == 2 * N_CHUNKS)
    def _():
        h = _layer_norm(x_ref[...], g0_ref[...], b0_ref[...])
        m = _lane_concat(m_ref)
        for k, w_ref in enumerate((wa_ref, wb_ref, wc_ref, wd_ref)):
            cs = slice(k * CHUNK, (k + 1) * CHUNK)
            h1_ref[:, cs] = ALPHA * h[:, cs] + _dot(m, w_ref[...])
        h1 = _layer_norm(h1_ref[...], g1_ref[...], b1_ref[...])
        h1_ref[...] = h1
        lg_ref[...] = _dot_nt(rw_ref[...], h1.astype(BF16)) + rb_ref[...][:, 0:1]


def _mixer_tail(x2d, hb, attn, g0, b0, um, cw, wall, g1, b1, rw, rb, seq):
    n = x2d.shape[0]
    nc = N_CHUNKS

    def slot(conv_first, merge_first, out_block, hold=False):
        def index(i, s):
            conv = conv_first if hold else conv_first + s
            return 0, jnp.where(s < nc, conv, jnp.where(s < 2 * nc, merge_first + s - nc, W_OUT + out_block))
        return pl.BlockSpec((D_MODEL, CHUNK), index)

    lo = lambda i, s: (0, jnp.minimum(s, nc - 1))
    const = lambda i, s: (0, 0)
    row = lambda i, s: (i, 0)
    return pl.pallas_call(
        functools.partial(_tail_kernel, tiles_per_seq=seq // ROW_TILE),
        grid=(n // ROW_TILE, TAIL_STEPS),
        in_specs=[
            pl.BlockSpec((ROW_TILE, D_MODEL), row),
            pl.BlockSpec((ROW_TILE, D_MODEL), row),
            pl.BlockSpec((ROW_TILE, D_MODEL), row),
            pl.BlockSpec((1, D_MODEL), const),
            pl.BlockSpec((1, D_MODEL), const),
            pl.BlockSpec((N_META, CHUNK), lo),
            pl.BlockSpec((CONV_K, CHUNK), lo),
            slot(W_C, W_GA, 0),
            slot(W_H, W_GC, 1),
            slot(W_B, W_BA, 2),
            slot(W_BC, W_BC, 3, hold=True),
            pl.BlockSpec((1, D_MODEL), const),
            pl.BlockSpec((1, D_MODEL), const),
            pl.BlockSpec((ROUTE_ROWS, D_MODEL), const),
            pl.BlockSpec((ROUTE_ROWS, LANE), const),
        ],
        out_specs=[
            pl.BlockSpec((ROW_TILE, D_MODEL), row),
            pl.BlockSpec((ROUTE_ROWS, ROW_TILE), lambda i, s: (0, i)),
        ],
        out_shape=[
            jax.ShapeDtypeStruct((n, D_MODEL), F32),
            jax.ShapeDtypeStruct((ROUTE_ROWS, n), F32),
        ],
        scratch_shapes=[
            pltpu.VMEM((ROW_TILE + 8, CHUNK), F32),
            pltpu.VMEM((nc, 8, CHUNK), F32),
            pltpu.VMEM((nc, ROW_TILE, CHUNK), BF16),
            pltpu.VMEM((nc, ROW_TILE, CHUNK), BF16),
        ],
        compiler_params=_params("arbitrary", "arbitrary"),
        name="mixer_tail",
    )(x2d, hb, attn, g0, b0, um, cw, wall, wall, wall, wall, g1, b1, rw, rb)


RCHUNK = 1024
PBLK = 256


def _route_kernel(lg_ref, e_ref, w_ref, pos_ref, cnt_ref, rank_ref):
    n = lg_ref.shape[1]
    epg = EXPERTS_PER_GROUP

    def route_chunk(c, carry):
        sl = pl.ds(pl.multiple_of(c * RCHUNK, RCHUNK), RCHUNK)
        gl = [lg_ref[gi:gi + 1, sl] for gi in range(N_GROUPS)]
        gmax = functools.reduce(jnp.maximum, gl)
        ge = [jnp.exp(v - gmax) for v in gl]
        gsum = functools.reduce(jnp.add, ge)
        gp = [v / gsum for v in ge]
        p_group = functools.reduce(jnp.maximum, gp)
        gsel = jnp.full_like(p_group, float(N_GROUPS - 1))
        for gi in range(N_GROUPS - 2, -1, -1):
            gsel = jnp.where(gp[gi] >= p_group, float(gi), gsel)
        e_in = lg_ref[EXPERT_ROW0 + (N_GROUPS - 1) * epg:EXPERT_ROW0 + N_GROUPS * epg, sl]
        for gi in range(N_GROUPS - 2, -1, -1):
            e_in = jnp.where(gsel == float(gi),
                             lg_ref[EXPERT_ROW0 + gi * epg:EXPERT_ROW0 + (gi + 1) * epg, sl], e_in)
        ee = jnp.exp(e_in - jnp.max(e_in, axis=0, keepdims=True))
        pe = ee / jnp.sum(ee, axis=0, keepdims=True)
        ridx = lax.broadcasted_iota(jnp.int32, pe.shape, 0).astype(F32)
        p0 = jnp.max(pe, axis=0, keepdims=True)
        i0 = jnp.min(jnp.where(pe >= p0, ridx, float(epg)), axis=0, keepdims=True)
        pe2 = jnp.where(ridx == i0, -1.0, pe)
        p1 = jnp.max(pe2, axis=0, keepdims=True)
        i1 = jnp.min(jnp.where(pe2 >= p1, ridx, float(epg)), axis=0, keepdims=True)
        den = p0 + p1
        e_ref[0:1, sl] = (gsel * epg + i0).astype(jnp.int32)
        e_ref[1:2, sl] = (gsel * epg + i1).astype(jnp.int32)
        w_ref[0:1, sl] = p0 / den * p_group
        w_ref[1:2, sl] = p1 / den * p_group
        return carry

    lax.fori_loop(0, n // RCHUNK, route_chunk, 0)

    eid = lax.broadcasted_iota(jnp.int32, (N_EXPERTS, PBLK), 0)
    before = jnp.where(lax.broadcasted_iota(jnp.int32, (PBLK, PBLK), 0)
                       < lax.broadcasted_iota(jnp.int32, (PBLK, PBLK), 1), 1.0, 0.0).astype(BF16)
    ones = jnp.ones((PBLK, LANE), BF16)
    nblk = n // PBLK

    def onehot(k, blk):
        sl = pl.ds(pl.multiple_of(blk * PBLK, PBLK), PBLK)
        return sl, eid == e_ref[pl.ds(k, 1), sl]

    run = jnp.zeros((N_EXPERTS, LANE), F32)
    for k in range(2):
        def count_block(blk, run, k=k):
            sl, hit = onehot(k, blk)
            hit_bf = jnp.where(hit, 1.0, 0.0).astype(BF16)
            prior = _dot(hit_bf, before) + jnp.concatenate([run] * (PBLK // LANE), axis=1)
            rank_ref[pl.ds(k, 1), sl] = jnp.sum(jnp.where(hit, prior, 0.0), axis=0, keepdims=True)
            return run + _dot(hit_bf, ones)
        run = lax.fori_loop(0, nblk, count_block, run)

    cnt_ref[...] = run.astype(jnp.int32)
    hi = jnp.floor(run * (1.0 / LANE))
    lo = run - hi * LANE
    below = jnp.where(lax.broadcasted_iota(jnp.int32, (N_EXPERTS, N_EXPERTS), 1)
                      < lax.broadcasted_iota(jnp.int32, (N_EXPERTS, N_EXPERTS), 0), 1.0, 0.0).astype(BF16)
    off = _dot(below, hi.astype(BF16)) * LANE + _dot(below, lo.astype(BF16))
    off = jnp.concatenate([off] * (PBLK // LANE), axis=1)

    for k in range(2):
        def place_block(blk, carry, k=k):
            sl, hit = onehot(k, blk)
            base = jnp.sum(jnp.where(hit, off, 0.0), axis=0, keepdims=True)
            pos_ref[pl.ds(k, 1), sl] = (base + rank_ref[pl.ds(k, 1), sl]).astype(jnp.int32)
            return carry
        lax.fori_loop(0, nblk, place_block, 0)


def _route(logits_t):
    n = logits_t.shape[1]
    return pl.pallas_call(
        _route_kernel,
        out_shape=[
            jax.ShapeDtypeStruct((2, n), jnp.int32),
            jax.ShapeDtypeStruct((2, n), F32),
            jax.ShapeDtypeStruct((2, n), jnp.int32),
            jax.ShapeDtypeStruct((N_EXPERTS, LANE), jnp.int32),
        ],
        scratch_shapes=[pltpu.VMEM((2, n), F32)],
        compiler_params=pltpu.CompilerParams(vmem_limit_bytes=V7X_VMEM_LIMIT),
        name="route",
    )(logits_t)


def _row_copy(src_ref, src_row, dst_ref, dst_row, sem):
    return pltpu.make_async_copy(src_ref.at[pl.ds(src_row, 1)], dst_ref.at[pl.ds(dst_row, 1)], sem)


def _dispatch_kernel(pos_ref, h_ref, xs_ref, sem):
    for r in range(TOK_TILE):
        for k in range(2):
            _row_copy(h_ref, r, xs_ref, pos_ref[0, k, r], sem).start(priority=k)
    for k in range(2):
        pltpu.make_async_copy(h_ref, xs_ref.at[pl.ds(0, TOK_TILE)], sem).wait()


def _dispatch(pos_tiles, h1):
    n = h1.shape[0]
    return pl.pallas_call(
        _dispatch_kernel,
        grid=(n // TOK_TILE,),
        in_specs=[
            pl.BlockSpec((1, 2, TOK_TILE), lambda i: (i, 0, 0), memory_space=pltpu.SMEM),
            pl.BlockSpec((TOK_TILE, D_MODEL), lambda i: (i, 0)),
        ],
        out_specs=pl.BlockSpec(memory_space=pl.ANY),
        out_shape=jax.ShapeDtypeStruct((2 * n, D_MODEL), F32),
        scratch_shapes=[pltpu.SemaphoreType.DMA],
        compiler_params=_params("arbitrary"),
        name="dispatch",
    )(pos_tiles, h1)


W_SLOTS = 3
W_LOOKAHEAD = W_SLOTS - 1
def _moe_kernel(tile_ref, exp_ref, lo_ref, first_ref, slot_ref, ahead_ref, head_ref, n_ref,
                xs_ref, wg_hbm, wu_hbm, wd_hbm, y_ref,
                wg_buf, wu_buf, wd_buf, ytile_ref, sem):
    i = pl.program_id(0)

    def weight_copies(expert, slot):
        return (pltpu.make_async_copy(wg_hbm.at[expert], wg_buf.at[slot], sem.at[slot, 0]),
                pltpu.make_async_copy(wu_hbm.at[expert], wu_buf.at[slot], sem.at[slot, 1]),
                pltpu.make_async_copy(wd_hbm.at[expert], wd_buf.at[slot], sem.at[slot, 2]))

    @pl.when(i == 0)
    def _():
        ytile_ref[...] = jnp.zeros_like(ytile_ref)
        for k in range(W_LOOKAHEAD):
            @pl.when(head_ref[k] >= 0)
            def _(k=k):
                for copy in weight_copies(head_ref[k], k):
                    copy.start()

    valid = i < n_ref[0]

    @pl.when(valid & (first_ref[i] == 1))
    def _():
        slot = slot_ref[i]
        for copy in weight_copies(exp_ref[i], slot):
            copy.wait()

        @pl.when(ahead_ref[i] >= 0)
        def _():
            for copy in weight_copies(ahead_ref[i], (slot + W_LOOKAHEAD) % W_SLOTS):
                copy.start()

    @pl.when(valid)
    def _():
        slot = slot_ref[i]
        x = xs_ref[...].astype(BF16)
        g = _dot(x, wg_buf[slot].astype(BF16))
        u = _dot(x, wu_buf[slot].astype(BF16))
        a = (g * jax.nn.sigmoid(g) * u).astype(BF16)
        y = _dot(a, wd_buf[slot].astype(BF16))
        row = lax.broadcasted_iota(jnp.int32, (MOE_TILE, 1), 0)
        merged = jnp.where(row >= lo_ref[i], y, ytile_ref[...])
        ytile_ref[...] = merged
        y_ref[...] = merged


def _moe_experts(items, xs, wg, wu, wd):
    rows = xs.shape[0]
    tile_map = lambda i, t, *_: (t[i], 0)
    grid_spec = pltpu.PrefetchScalarGridSpec(
        num_scalar_prefetch=len(items),
        grid=(MAX_ITEMS,),
        in_specs=[
            pl.BlockSpec((MOE_TILE, D_MODEL), tile_map),
            pl.BlockSpec(memory_space=pl.ANY),
            pl.BlockSpec(memory_space=pl.ANY),
            pl.BlockSpec(memory_space=pl.ANY),
        ],
        out_specs=pl.BlockSpec((MOE_TILE, D_MODEL), tile_map),
        scratch_shapes=[
            pltpu.VMEM((W_SLOTS, D_MODEL, D_EXPERT), F32),
            pltpu.VMEM((W_SLOTS, D_MODEL, D_EXPERT), F32),
            pltpu.VMEM((W_SLOTS, D_EXPERT, D_MODEL), F32),
            pltpu.VMEM((MOE_TILE, D_MODEL), F32),
            pltpu.SemaphoreType.DMA((W_SLOTS, 3)),
        ],
    )
    return pl.pallas_call(
        _moe_kernel,
        grid_spec=grid_spec,
        out_shape=jax.ShapeDtypeStruct((rows, D_MODEL), F32),
        compiler_params=_params("arbitrary"),
        name="moe_experts",
    )(*items, xs, wg, wu, wd)


def _work_items(counts):
    i32 = jnp.int32
    off = jnp.cumsum(counts) - counts
    end = off + counts
    first_tile = off // MOE_TILE
    n_e = jnp.where(counts > 0, (end - 1) // MOE_TILE - first_tile + 1, 0)
    item_end = jnp.cumsum(n_e)
    n_items = item_end[-1]
    idx = jnp.minimum(jnp.arange(MAX_ITEMS, dtype=i32), n_items - 1)
    exp = jnp.sum((item_end[None, :] <= idx[:, None]).astype(i32), axis=1)
    item0 = item_end[exp] - n_e[exp]
    tile = first_tile[exp] + idx - item0
    lo = jnp.maximum(off[exp] - tile * MOE_TILE, 0)
    present = counts > 0
    ordinal = jnp.cumsum(present.astype(i32)) - 1
    eid = jnp.arange(N_EXPERTS, dtype=i32)
    is_kth = present[None, :] & (ordinal[None, :] == eid[:, None])
    kth = jnp.where(jnp.any(is_kth, axis=1), jnp.sum(jnp.where(is_kth, eid[None, :], 0), axis=1), -1)
    kth = jnp.concatenate([kth, jnp.full((W_LOOKAHEAD,), -1, i32)])
    items = (tile, exp, lo, idx == item0, ordinal[exp] % W_SLOTS, kth[ordinal[exp] + W_LOOKAHEAD],
             kth[:W_LOOKAHEAD], n_items.reshape(1))
    return tuple(v.astype(i32) for v in items)


def _combine_kernel(pos_ref, pos_next_ref, h_ref, wt_ref, g_ref, b_ref, y_ref, o_ref, buf_ref, sem):
    i = pl.program_id(0)
    slot = i % 2

    def gather(tile_pos_ref, into):
        for r in range(TOK_TILE):
            for k in range(2):
                _row_copy(y_ref, tile_pos_ref[0, k, r], buf_ref.at[into, k], r, sem.at[into]).start(priority=k)

    @pl.when(i == 0)
    def _():
        gather(pos_ref, 0)

    for other in range(2):
        @pl.when((i + 1 < pl.num_programs(0)) & (slot == 1 - other))
        def _(other=other):
            gather(pos_next_ref, other)

    for k in range(2):
        pltpu.make_async_copy(y_ref.at[pl.ds(0, TOK_TILE)], buf_ref.at[slot, k], sem.at[slot]).wait()
    f = wt_ref[:, 0:1] * buf_ref[slot, 0] + wt_ref[:, 1:2] * buf_ref[slot, 1]
    o_ref[...] = _layer_norm(ALPHA * h_ref[...] + f, g_ref[...], b_ref[...])


def _combine_ln(pos_tiles, h1, wt, g, b, y):
    n = h1.shape[0]
    steps = n // TOK_TILE
    return pl.pallas_call(
        _combine_kernel,
        grid=(steps,),
        in_specs=[
            pl.BlockSpec((1, 2, TOK_TILE), lambda i: (i, 0, 0), memory_space=pltpu.SMEM),
            pl.BlockSpec((1, 2, TOK_TILE), lambda i: (jnp.minimum(i + 1, steps - 1), 0, 0),
                         memory_space=pltpu.SMEM),
            pl.BlockSpec((TOK_TILE, D_MODEL), lambda i: (i, 0)),
            pl.BlockSpec((TOK_TILE, 2), lambda i: (i, 0)),
            pl.BlockSpec((1, D_MODEL), lambda i: (0, 0)),
            pl.BlockSpec((1, D_MODEL), lambda i: (0, 0)),
            pl.BlockSpec(memory_space=pl.ANY),
        ],
        out_specs=pl.BlockSpec((TOK_TILE, D_MODEL), lambda i: (i, 0)),
        out_shape=jax.ShapeDtypeStruct((n, D_MODEL), F32),
        scratch_shapes=[pltpu.VMEM((2, 2, TOK_TILE, D_MODEL), F32), pltpu.SemaphoreType.DMA((2,))],
        compiler_params=_params("arbitrary"),
        name="combine_ln",
    )(pos_tiles, pos_tiles, h1, wt, g, b, y)


def _dup_heads(w):
    d = w.shape[0]
    w = w.reshape(d, N_KV_HEADS, 1, HEAD_DIM)
    return jnp.broadcast_to(w, (d, N_KV_HEADS, 2, HEAD_DIM)).reshape(d, 2 * KV_WIDTH)


def kernel(x, meta_tokens, ln_in_g, ln_in_b, w_in, sinks, conv_w, w_branch_attn, w_branch_conv, w_out,
           ln1_g, ln1_b, router_group_w, router_group_b, router_expert_w, router_expert_b,
           w_gate, w_up, w_down, ln2_g, ln2_b):
    batch, seq, d = x.shape
    n = batch * seq
    assert d == D_MODEL and seq % ROW_TILE == 0 and w_in.shape[0] == DEPTH == 1
    row = lambda v: v.reshape(1, -1).astype(F32)
    x2d = x.reshape(n, d)
    g0, b0 = row(ln_in_g), row(ln_in_b)

    win = w_in[0]
    kcol, vcol, ccol = D_MODEL, D_MODEL + KV_WIDTH, D_MODEL + 2 * KV_WIDTH
    wkv_dup = jnp.concatenate([_dup_heads(win[:, kcol:vcol]), _dup_heads(win[:, vcol:ccol])], axis=1).astype(BF16)
    wall = _pack_weights(win, w_branch_attn[0], w_branch_conv[0], w_out[0])
    rw = jnp.zeros((ROUTE_ROWS, d), F32)
    rw = rw.at[:N_GROUPS].set(router_group_w[0].T).at[EXPERT_ROW0:EXPERT_ROW0 + N_EXPERTS].set(router_expert_w[0].T)
    rb = jnp.zeros((ROUTE_ROWS,), F32)
    rb = rb.at[:N_GROUPS].set(router_group_b[0]).at[EXPERT_ROW0:EXPERT_ROW0 + N_EXPERTS].set(router_expert_b[0])
    rb = jnp.broadcast_to(rb[:, None], (ROUTE_ROWS, LANE))

    kvm, um = _meta_prep(meta_tokens.astype(F32), g0, b0, wkv_dup, wall)
    hb, q, kd, vd = _ln_qkv(x2d, g0, b0, wall, wkv_dup)
    attn = _swa_attn(sinks[0].astype(F32), q, kd, vd, kvm, batch, seq // BLOCK)
    h1, logits_t = _mixer_tail(x2d, hb, attn, g0, b0, um, conv_w[0], wall,
                               row(ln1_g[0]), row(ln1_b[0]), rw.astype(BF16), rb, seq)
    _, top_w, pos, counts = _route(logits_t)
    pos_tiles = pos.reshape(2, n // TOK_TILE, TOK_TILE).transpose(1, 0, 2)
    xs = _dispatch(pos_tiles, h1)
    y = _moe_experts(_work_items(counts[:, 0]), xs, w_gate[0], w_up[0], w_down[0])
    out = _combine_ln(pos_tiles, h1, top_w.T, row(ln2_g[0]), row(ln2_b[0]), y)
    return out.reshape(batch, seq, d)
```

```python
import functools

import jax
import jax.numpy as jnp
from jax import lax
from jax.experimental import pallas as pl
from jax.experimental.pallas import tpu as pltpu

D_MODEL = 2048
N_META = 16
HEAD_DIM = 64
N_Q_HEADS = 32
N_KV_HEADS = 4
BLOCK = 128
KV_WIDTH = N_KV_HEADS * HEAD_DIM
CONV_K = 3
N_GROUPS = 4
EXPERTS_PER_GROUP = 8
N_EXPERTS = N_GROUPS * EXPERTS_PER_GROUP
D_EXPERT = D_MODEL // 4
LN_EPS = 1e-5
DEPTH = 1
ALPHA = (2.0 * DEPTH) ** 0.25
NEG_INF = -1e30
Q_SCALE = HEAD_DIM ** -0.5

V7X_VMEM_LIMIT = 56 * 1024 * 1024
LANE = 128
CHUNK = 512
N_CHUNKS = D_MODEL // CHUNK
ROW_TILE = 512
MOE_TILE = 256
TOK_TILE = 256
MAX_ITEMS = 2 * 8192 // MOE_TILE + N_EXPERTS
ROUTE_ROWS = 128
EXPERT_ROW0 = 8

BF16 = jnp.bfloat16
F32 = jnp.float32


def _layer_norm(x, g, b):
    mu = jnp.mean(x, axis=-1, keepdims=True)
    xc = x - mu
    var = jnp.mean(xc * xc, axis=-1, keepdims=True)
    return xc * lax.rsqrt(var + LN_EPS) * g + b


def _dot(a, b):
    return jnp.dot(a, b, preferred_element_type=F32)


def _dot_nt(a, b):
    return lax.dot_general(a, b, (((1,), (1,)), ((), ())), preferred_element_type=F32)


def _params(*sem):
    return pltpu.CompilerParams(dimension_semantics=sem, vmem_limit_bytes=V7X_VMEM_LIMIT)


W_B, W_C, W_H, W_GA, W_GC, W_BA, W_BC, W_OUT = (k * N_CHUNKS for k in range(8))
N_WBLOCKS = 8 * N_CHUNKS
IN_FIRST_MIX = (D_MODEL + 2 * KV_WIDTH) // CHUNK


def _meta_kernel(meta_ref, g_ref, b_ref, wkv_ref, wcc_ref, wch_ref, kv_ref, u_ref):
    hm = _layer_norm(meta_ref[...], g_ref[...], b_ref[...]).astype(BF16)
    kv_ref[...] = _dot(hm, wkv_ref[...]).astype(BF16)
    u_ref[...] = _dot(hm, wcc_ref[...]) * _dot(hm, wch_ref[...])


def _meta_prep(meta, g, b, wkv_dup, wall):
    n_kv = wkv_dup.shape[1] // N_CHUNKS
    return pl.pallas_call(
        _meta_kernel,
        grid=(N_CHUNKS,),
        in_specs=[
            pl.BlockSpec((N_META, D_MODEL), lambda j: (0, 0)),
            pl.BlockSpec((1, D_MODEL), lambda j: (0, 0)),
            pl.BlockSpec((1, D_MODEL), lambda j: (0, 0)),
            pl.BlockSpec((D_MODEL, n_kv), lambda j: (0, j)),
            pl.BlockSpec((D_MODEL, CHUNK), lambda j: (0, W_C + j)),
            pl.BlockSpec((D_MODEL, CHUNK), lambda j: (0, W_H + j)),
        ],
        out_specs=[
            pl.BlockSpec((N_META, n_kv), lambda j: (0, j)),
            pl.BlockSpec((N_META, CHUNK), lambda j: (0, j)),
        ],
        out_shape=[
            jax.ShapeDtypeStruct((N_META, wkv_dup.shape[1]), BF16),
            jax.ShapeDtypeStruct((N_META, D_MODEL), F32),
        ],
        compiler_params=_params("arbitrary"),
        name="meta_prep",
    )(meta, g, b, wkv_dup, wall, wall)


def _qkv_kernel(x_ref, g_ref, b_ref, wq_ref, wkv_ref, win_hbm, ba_hbm, bc_hbm, out_hbm,
                hb_ref, q_ref, k_ref, v_ref, wall_ref, wq_bf, stage, sem):
    i = pl.program_id(0)
    kw = 2 * KV_WIDTH

    def start_block(j, slot):
        def copy(src_hbm, block):
            col = pl.multiple_of(block * CHUNK, CHUNK)
            pltpu.make_async_copy(src_hbm.at[:, pl.ds(col, CHUNK)], stage.at[slot], sem.at[slot]).start()

        @pl.when(j < W_BA)
        def _():
            copy(win_hbm, IN_FIRST_MIX + j)

        for src_hbm, first in ((ba_hbm, W_BA), (bc_hbm, W_BC), (out_hbm, W_OUT)):
            @pl.when((j >= first) & (j < first + N_CHUNKS))
            def _(src_hbm=src_hbm, first=first):
                copy(src_hbm, j - first)

    @pl.when(i == 0)
    def _():
        start_block(0, 0)
        wq_bf[...] = wq_ref[...].astype(BF16)

    slot = i % 2
    pltpu.make_async_copy(ba_hbm.at[:, pl.ds(0, CHUNK)], stage.at[slot], sem.at[slot]).wait()

    @pl.when(i + 1 < N_WBLOCKS)
    def _():
        start_block(i + 1, 1 - slot)

    wall_ref[...] = stage[slot].astype(BF16)
    hb = _layer_norm(x_ref[...], g_ref[...], b_ref[...]).astype(BF16)
    hb_ref[...] = hb
    q_ref[...] = (_dot(hb, wq_bf[...]) * Q_SCALE).astype(BF16)
    k_ref[...] = _dot(hb, wkv_ref[:, :kw]).astype(BF16)
    v_ref[...] = _dot(hb, wkv_ref[:, kw:]).astype(BF16)


def _ln_qkv(x2d, g, b, win, wkv_dup, wba, wbc, wout):
    n = x2d.shape[0]
    tile = n // N_WBLOCKS
    assert tile * N_WBLOCKS == n and tile % 16 == 0
    kw = 2 * KV_WIDTH
    row = lambda i: (i, 0)
    const = lambda i: (0, 0)
    once = dict(pipeline_mode=pl.Buffered(1))
    any_spec = pl.BlockSpec(memory_space=pl.ANY)
    return pl.pallas_call(
        _qkv_kernel,
        grid=(N_WBLOCKS,),
        in_specs=[
            pl.BlockSpec((tile, D_MODEL), row),
            pl.BlockSpec((1, D_MODEL), const),
            pl.BlockSpec((1, D_MODEL), const),
            pl.BlockSpec((D_MODEL, D_MODEL), const, **once),
            pl.BlockSpec(wkv_dup.shape, const, **once),
            any_spec, any_spec, any_spec, any_spec,
        ],
        out_specs=[
            pl.BlockSpec((tile, D_MODEL), row),
            pl.BlockSpec((tile, D_MODEL), row),
            pl.BlockSpec((tile, kw), row),
            pl.BlockSpec((tile, kw), row),
            pl.BlockSpec((D_MODEL, CHUNK), lambda i: (0, i)),
        ],
        out_shape=[
            jax.ShapeDtypeStruct((n, D_MODEL), BF16),
            jax.ShapeDtypeStruct((n, D_MODEL), BF16),
            jax.ShapeDtypeStruct((n, kw), BF16),
            jax.ShapeDtypeStruct((n, kw), BF16),
            jax.ShapeDtypeStruct((D_MODEL, N_WBLOCKS * CHUNK), BF16),
        ],
        scratch_shapes=[
            pltpu.VMEM((D_MODEL, D_MODEL), BF16),
            pltpu.VMEM((2, D_MODEL, CHUNK), F32),
            pltpu.SemaphoreType.DMA((2,)),
        ],
        compiler_params=_params("arbitrary"),
        name="ln_qkv",
    )(x2d, g, b, win, wkv_dup, win, wba, wbc, wout)


PAIRS = N_Q_HEADS // N_KV_HEADS // 2
QROWS = PAIRS * BLOCK


def _attn_kernel(sink_ref, q_ref, kp_ref, kc_ref, vp_ref, vc_ref, kvm_ref, o_ref):
    blk = pl.program_id(1)
    lane = lax.broadcasted_iota(jnp.int32, (BLOCK, LANE), 1)
    qi = lax.broadcasted_iota(jnp.int32, (QROWS, BLOCK), 0) % BLOCK
    kj = lax.broadcasted_iota(jnp.int32, (QROWS, BLOCK), 1)
    from_prev = kj > qi
    has_prev = blk > 0
    is_meta = kj < N_META
    rowpair = lax.broadcasted_iota(jnp.int32, (QROWS, 1), 0) // BLOCK
    zero = jnp.zeros((), BF16)
    zpad = jnp.zeros((BLOCK - N_META, LANE), BF16)

    for g in range(N_KV_HEADS):
        gs = slice(g * LANE, (g + 1) * LANE)
        vs = slice((N_KV_HEADS + g) * LANE, (N_KV_HEADS + g + 1) * LANE)
        qs = jnp.concatenate(
            [q_ref[:, (g * PAIRS + p) * LANE:(g * PAIRS + p + 1) * LANE] for p in range(PAIRS)], axis=0)
        keys = (kp_ref[:, gs], kc_ref[:, gs], jnp.concatenate([kvm_ref[:, gs], zpad], axis=0))
        vals = (vp_ref[:, gs], vc_ref[:, gs], jnp.concatenate([kvm_ref[:, vs], zpad], axis=0))
        acc = None
        for half in range(2):
            keep = (lane >= HEAD_DIM) if half else (lane < HEAD_DIM)
            s = _dot_nt(qs, jnp.concatenate([jnp.where(keep, k, zero) for k in keys], axis=0))
            s_prev = jnp.where(has_prev, s[:, :BLOCK], NEG_INF)
            s_band = jnp.where(from_prev, s_prev, s[:, BLOCK:2 * BLOCK])
            s_meta = jnp.where(is_meta, s[:, 2 * BLOCK:], NEG_INF)
            sink = jnp.zeros((QROWS, 1), F32)
            for p in range(PAIRS):
                sink = jnp.where(rowpair == p, sink_ref[g * 2 * PAIRS + 2 * p + half], sink)
            m = jnp.maximum(jnp.max(jnp.maximum(s_band, s_meta), axis=-1, keepdims=True), sink)
            e_band = jnp.exp(s_band - m)
            e_meta = jnp.exp(s_meta - m)
            den = jnp.sum(e_band + e_meta, axis=-1, keepdims=True) + jnp.exp(sink - m)
            p_band = e_band / den
            probs = jnp.concatenate([jnp.where(from_prev, p_band, 0.0).astype(BF16),
                                     jnp.where(from_prev, 0.0, p_band).astype(BF16),
                                     (e_meta / den).astype(BF16)], axis=1)
            part = _dot(probs, jnp.concatenate([jnp.where(keep, v, zero) for v in vals], axis=0))
            acc = part if acc is None else acc + part
        for p in range(PAIRS):
            col = (g * PAIRS + p) * LANE
            o_ref[:, col:col + LANE] = acc[p * BLOCK:(p + 1) * BLOCK].astype(BF16)


def _swa_attn(sinks, q, kd, vd, kvm, batch, nblk):
    n = q.shape[0]
    kw = 2 * KV_WIDTH
    cur = lambda b, i: (b * nblk + i, 0)
    prev = lambda b, i: (b * nblk + jnp.maximum(i - 1, 0), 0)
    return pl.pallas_call(
        _attn_kernel,
        grid=(batch, nblk),
        in_specs=[
            pl.BlockSpec(memory_space=pltpu.SMEM),
            pl.BlockSpec((BLOCK, D_MODEL), cur),
            pl.BlockSpec((BLOCK, kw), prev),
            pl.BlockSpec((BLOCK, kw), cur),
            pl.BlockSpec((BLOCK, kw), prev),
            pl.BlockSpec((BLOCK, kw), cur),
            pl.BlockSpec((N_META, 2 * kw), lambda b, i: (0, 0)),
        ],
        out_specs=pl.BlockSpec((BLOCK, D_MODEL), cur),
        out_shape=jax.ShapeDtypeStruct((n, D_MODEL), BF16),
        compiler_params=_params("arbitrary", "arbitrary"),
        name="swa_attn",
    )(sinks, q, kd, kd, vd, vd, kvm)


TAIL_STEPS = 2 * N_CHUNKS + 1


def _lane_concat(ref):
    return jnp.concatenate([ref[k] for k in range(ref.shape[0])], axis=1)


def _tail_kernel(x_ref, hb_ref, a_ref, g0_ref, b0_ref, um_ref, cw_ref, wa_ref, wb_ref, wc_ref, wd_ref,
                 g1_ref, b1_ref, rw_ref, rb_ref, h1_ref, lg_ref,
                 uext_ref, carry_ref, z_ref, m_ref, *, tiles_per_seq):
    i = pl.program_id(0)
    s = pl.program_id(1)
    tm = ROW_TILE

    @pl.when((i == 0) & (s == 0))
    def _():
        carry_ref[...] = jnp.zeros_like(carry_ref)

    @pl.when(s < N_CHUNKS)
    def _():
        hb = hb_ref[...]
        u = _dot(hb, wa_ref[...]) * _dot(hb, wb_ref[...])
        first = i % tiles_per_seq == 0
        uext_ref[0:8, :] = jnp.where(first, um_ref[N_META - 8:N_META, :], carry_ref[s])
        uext_ref[8:tm + 8, :] = u
        conv = (cw_ref[0:1, :] * uext_ref[6:tm + 6, :] + cw_ref[1:2, :] * uext_ref[7:tm + 7, :]
                + cw_ref[2:3, :] * u)
        carry_ref[s] = u[tm - 8:tm, :]
        z_ref[s] = (_dot(hb, wc_ref[...]) * conv).astype(BF16)

    @pl.when((s >= N_CHUNKS) & (s < 2 * N_CHUNKS))
    def _():
        hb = hb_ref[...]
        conv_part = jax.nn.sigmoid(_dot(hb, wb_ref[...])) * _dot(_lane_concat(z_ref), wd_ref[...])
        gate_attn = jax.nn.sigmoid(_dot(hb, wa_ref[...]))
        m_ref[s - N_CHUNKS] = (gate_attn * _dot(a_ref[...], wc_ref[...]) + conv_part).astype(BF16)

    @pl.when(s == 2 * N_CHUNKS)
    def _():
        h = _layer_norm(x_ref[...], g0_ref[...], b0_ref[...])
        m = _lane_concat(m_ref)
        for k, w_ref in enumerate((wa_ref, wb_ref, wc_ref, wd_ref)):
            cs = slice(k * CHUNK, (k + 1) * CHUNK)
            h1_ref[:, cs] = ALPHA * h[:, cs] + _dot(m, w_ref[...])
        h1 = _layer_norm(h1_ref[...], g1_ref[...], b1_ref[...])
        h1_ref[...] = h1
        lg_ref[...] = _dot_nt(rw_ref[...], h1.astype(BF16)) + rb_ref[...][:, 0:1]


def _mixer_tail(x2d, hb, attn, g0, b0, um, cw, wall, g1, b1, rw, rb, seq):
    n = x2d.shape[0]
    nc = N_CHUNKS

    def slot(conv_first, merge_first, out_block, hold=False):
        def index(i, s):
            conv = conv_first if hold else conv_first + s
            return 0, jnp.where(s < nc, conv, jnp.where(s < 2 * nc, merge_first + s - nc, W_OUT + out_block))
        return pl.BlockSpec((D_MODEL, CHUNK), index)

    lo = lambda i, s: (0, jnp.minimum(s, nc - 1))
    const = lambda i, s: (0, 0)
    row = lambda i, s: (i, 0)
    return pl.pallas_call(
        functools.partial(_tail_kernel, tiles_per_seq=seq // ROW_TILE),
        grid=(n // ROW_TILE, TAIL_STEPS),
        in_specs=[
            pl.BlockSpec((ROW_TILE, D_MODEL), row),
            pl.BlockSpec((ROW_TILE, D_MODEL), row),
            pl.BlockSpec((ROW_TILE, D_MODEL), row),
            pl.BlockSpec((1, D_MODEL), const),
            pl.BlockSpec((1, D_MODEL), const),
            pl.BlockSpec((N_META, CHUNK), lo),
            pl.BlockSpec((CONV_K, CHUNK), lo),
            slot(W_C, W_GA, 0),
            slot(W_H, W_GC, 1),
            slot(W_B, W_BA, 2),
            slot(W_BC, W_BC, 3, hold=True),
            pl.BlockSpec((1, D_MODEL), const),
            pl.BlockSpec((1, D_MODEL), const),
            pl.BlockSpec((ROUTE_ROWS, D_MODEL), const),
            pl.BlockSpec((ROUTE_ROWS, LANE), const),
        ],
        out_specs=[
            pl.BlockSpec((ROW_TILE, D_MODEL), row),
            pl.BlockSpec((ROUTE_ROWS, ROW_TILE), lambda i, s: (0, i)),
        ],
        out_shape=[
            jax.ShapeDtypeStruct((n, D_MODEL), F32),
            jax.ShapeDtypeStruct((ROUTE_ROWS, n), F32),
        ],
        scratch_shapes=[
            pltpu.VMEM((ROW_TILE + 8, CHUNK), F32),
            pltpu.VMEM((nc, 8, CHUNK), F32),
            pltpu.VMEM((nc, ROW_TILE, CHUNK), BF16),
            pltpu.VMEM((nc, ROW_TILE, CHUNK), BF16),
        ],
        compiler_params=_params("arbitrary", "arbitrary"),
        name="mixer_tail",
    )(x2d, hb, attn, g0, b0, um, cw, wall, wall, wall, wall, g1, b1, rw, rb)


RCHUNK = 1024
PBLK = 256


def _route_kernel(lg_ref, e_ref, w_ref, pos_ref, cnt_ref, rank_ref):
    n = lg_ref.shape[1]
    epg = EXPERTS_PER_GROUP

    def route_chunk(c, carry):
        sl = pl.ds(pl.multiple_of(c * RCHUNK, RCHUNK), RCHUNK)
        gl = [lg_ref[gi:gi + 1, sl] for gi in range(N_GROUPS)]
        gmax = functools.reduce(jnp.maximum, gl)
        ge = [jnp.exp(v - gmax) for v in gl]
        gsum = functools.reduce(jnp.add, ge)
        gp = [v / gsum for v in ge]
        p_group = functools.reduce(jnp.maximum, gp)
        gsel = jnp.full_like(p_group, float(N_GROUPS - 1))
        for gi in range(N_GROUPS - 2, -1, -1):
            gsel = jnp.where(gp[gi] >= p_group, float(gi), gsel)
        e_in = lg_ref[EXPERT_ROW0 + (N_GROUPS - 1) * epg:EXPERT_ROW0 + N_GROUPS * epg, sl]
        for gi in range(N_GROUPS - 2, -1, -1):
            e_in = jnp.where(gsel == float(gi),
                             lg_ref[EXPERT_ROW0 + gi * epg:EXPERT_ROW0 + (gi + 1) * epg, sl], e_in)
        ee = jnp.exp(e_in - jnp.max(e_in, axis=0, keepdims=True))
        pe = ee / jnp.sum(ee, axis=0, keepdims=True)
        ridx = lax.broadcasted_iota(jnp.int32, pe.shape, 0).astype(F32)
        p0 = jnp.max(pe, axis=0, keepdims=True)
        i0 = jnp.min(jnp.where(pe >= p0, ridx, float(epg)), axis=0, keepdims=True)
        pe2 = jnp.where(ridx == i0, -1.0, pe)
        p1 = jnp.max(pe2, axis=0, keepdims=True)
        i1 = jnp.min(jnp.where(pe2 >= p1, ridx, float(epg)), axis=0, keepdims=True)
        den = p0 + p1
        e_ref[0:1, sl] = (gsel * epg + i0).astype(jnp.int32)
        e_ref[1:2, sl] = (gsel * epg + i1).astype(jnp.int32)
        w_ref[0:1, sl] = p0 / den * p_group
        w_ref[1:2, sl] = p1 / den * p_group
        return carry

    lax.fori_loop(0, n // RCHUNK, route_chunk, 0)

    eid = lax.broadcasted_iota(jnp.int32, (N_EXPERTS, PBLK), 0)
    before = jnp.where(lax.broadcasted_iota(jnp.int32, (PBLK, PBLK), 0)
                       < lax.broadcasted_iota(jnp.int32, (PBLK, PBLK), 1), 1.0, 0.0).astype(BF16)
    ones = jnp.ones((PBLK, LANE), BF16)
    nblk = n // PBLK

    def onehot(k, blk):
        sl = pl.ds(pl.multiple_of(blk * PBLK, PBLK), PBLK)
        return sl, eid == e_ref[pl.ds(k, 1), sl]

    run = jnp.zeros((N_EXPERTS, LANE), F32)
    for k in range(2):
        def count_block(blk, run, k=k):
            sl, hit = onehot(k, blk)
            hit_bf = jnp.where(hit, 1.0, 0.0).astype(BF16)
            prior = _dot(hit_bf, before) + jnp.concatenate([run] * (PBLK // LANE), axis=1)
            rank_ref[pl.ds(k, 1), sl] = jnp.sum(jnp.where(hit, prior, 0.0), axis=0, keepdims=True)
            return run + _dot(hit_bf, ones)
        run = lax.fori_loop(0, nblk, count_block, run)

    cnt_ref[...] = run.astype(jnp.int32)
    hi = jnp.floor(run * (1.0 / LANE))
    lo = run - hi * LANE
    below = jnp.where(lax.broadcasted_iota(jnp.int32, (N_EXPERTS, N_EXPERTS), 1)
                      < lax.broadcasted_iota(jnp.int32, (N_EXPERTS, N_EXPERTS), 0), 1.0, 0.0).astype(BF16)
    off = _dot(below, hi.astype(BF16)) * LANE + _dot(below, lo.astype(BF16))
    off = jnp.concatenate([off] * (PBLK // LANE), axis=1)

    for k in range(2):
        def place_block(blk, carry, k=k):
            sl, hit = onehot(k, blk)
            base = jnp.sum(jnp.where(hit, off, 0.0), axis=0, keepdims=True)
            pos_ref[pl.ds(k, 1), sl] = (base + rank_ref[pl.ds(k, 1), sl]).astype(jnp.int32)
            return carry
        lax.fori_loop(0, nblk, place_block, 0)


def _route(logits_t):
    n = logits_t.shape[1]
    return pl.pallas_call(
        _route_kernel,
        out_shape=[
            jax.ShapeDtypeStruct((2, n), jnp.int32),
            jax.ShapeDtypeStruct((2, n), F32),
            jax.ShapeDtypeStruct((2, n), jnp.int32),
            jax.ShapeDtypeStruct((N_EXPERTS, LANE), jnp.int32),
        ],
        scratch_shapes=[pltpu.VMEM((2, n), F32)],
        compiler_params=pltpu.CompilerParams(vmem_limit_bytes=V7X_VMEM_LIMIT),
        name="route",
    )(logits_t)


def _row_copy(src_ref, src_row, dst_ref, dst_row, sem):
    return pltpu.make_async_copy(src_ref.at[pl.ds(src_row, 1)], dst_ref.at[pl.ds(dst_row, 1)], sem)


def _dispatch_kernel(pos_ref, h_ref, xs_ref, sem):
    for r in range(TOK_TILE):
        for k in range(2):
            _row_copy(h_ref, r, xs_ref, pos_ref[0, k, r], sem).start(priority=k)
    for k in range(2):
        pltpu.make_async_copy(h_ref, xs_ref.at[pl.ds(0, TOK_TILE)], sem).wait()


def _dispatch(pos_tiles, h1):
    n = h1.shape[0]
    return pl.pallas_call(
        _dispatch_kernel,
        grid=(n // TOK_TILE,),
        in_specs=[
            pl.BlockSpec((1, 2, TOK_TILE), lambda i: (i, 0, 0), memory_space=pltpu.SMEM),
            pl.BlockSpec((TOK_TILE, D_MODEL), lambda i: (i, 0)),
        ],
        out_specs=pl.BlockSpec(memory_space=pl.ANY),
        out_shape=jax.ShapeDtypeStruct((2 * n, D_MODEL), F32),
        scratch_shapes=[pltpu.SemaphoreType.DMA],
        compiler_params=_params("arbitrary"),
        name="dispatch",
    )(pos_tiles, h1)


W_SLOTS = 2
W_LOOKAHEAD = W_SLOTS - 1
def _moe_kernel(tile_ref, exp_ref, lo_ref, first_ref, slot_ref, ahead_ref, head_ref, n_ref,
                xs_ref, wg_hbm, wu_hbm, wd_hbm, y_ref,
                wg_buf, wu_buf, wd_buf, ytile_ref, sem):
    i = pl.program_id(0)

    def weight_copies(expert, slot):
        return (pltpu.make_async_copy(wg_hbm.at[expert], wg_buf.at[slot], sem.at[slot, 0]),
                pltpu.make_async_copy(wu_hbm.at[expert], wu_buf.at[slot], sem.at[slot, 1]),
                pltpu.make_async_copy(wd_hbm.at[expert], wd_buf.at[slot], sem.at[slot, 2]))

    @pl.when(i == 0)
    def _():
        ytile_ref[...] = jnp.zeros_like(ytile_ref)
        for k in range(W_LOOKAHEAD):
            @pl.when(head_ref[k] >= 0)
            def _(k=k):
                for copy in weight_copies(head_ref[k], k):
                    copy.start()

    valid = i < n_ref[0]

    @pl.when(valid & (first_ref[i] == 1))
    def _():
        slot = slot_ref[i]
        for copy in weight_copies(exp_ref[i], slot):
            copy.wait()

        @pl.when(ahead_ref[i] >= 0)
        def _():
            for copy in weight_copies(ahead_ref[i], (slot + W_LOOKAHEAD) % W_SLOTS):
                copy.start()

    @pl.when(valid)
    def _():
        slot = slot_ref[i]
        x = xs_ref[...].astype(BF16)
        g = _dot(x, wg_buf[slot].astype(BF16))
        u = _dot(x, wu_buf[slot].astype(BF16))
        a = (g * jax.nn.sigmoid(g) * u).astype(BF16)
        y = _dot(a, wd_buf[slot].astype(BF16))
        row = lax.broadcasted_iota(jnp.int32, (MOE_TILE, 1), 0)
        merged = jnp.where(row >= lo_ref[i], y, ytile_ref[...])
        ytile_ref[...] = merged
        y_ref[...] = merged


def _moe_experts(items, xs, wg, wu, wd):
    rows = xs.shape[0]
    tile_map = lambda i, t, *_: (t[i], 0)
    grid_spec = pltpu.PrefetchScalarGridSpec(
        num_scalar_prefetch=len(items),
        grid=(MAX_ITEMS,),
        in_specs=[
            pl.BlockSpec((MOE_TILE, D_MODEL), tile_map),
            pl.BlockSpec(memory_space=pl.ANY),
            pl.BlockSpec(memory_space=pl.ANY),
            pl.BlockSpec(memory_space=pl.ANY),
        ],
        out_specs=pl.BlockSpec((MOE_TILE, D_MODEL), tile_map),
        scratch_shapes=[
            pltpu.VMEM((W_SLOTS, D_MODEL, D_EXPERT), F32),
            pltpu.VMEM((W_SLOTS, D_MODEL, D_EXPERT), F32),
            pltpu.VMEM((W_SLOTS, D_EXPERT, D_MODEL), F32),
            pltpu.VMEM((MOE_TILE, D_MODEL), F32),
            pltpu.SemaphoreType.DMA((W_SLOTS, 3)),
        ],
    )
    return pl.pallas_call(
        _moe_kernel,
        grid_spec=grid_spec,
        out_shape=jax.ShapeDtypeStruct((rows, D_MODEL), F32),
        compiler_params=_params("arbitrary"),
        name="moe_experts",
    )(*items, xs, wg, wu, wd)


def _work_items(counts):
    i32 = jnp.int32
    off = jnp.cumsum(counts) - counts
    end = off + counts
    first_tile = off // MOE_TILE
    n_e = jnp.where(counts > 0, (end - 1) // MOE_TILE - first_tile + 1, 0)
    item_end = jnp.cumsum(n_e)
    n_items = item_end[-1]
    idx = jnp.minimum(jnp.arange(MAX_ITEMS, dtype=i32), n_items - 1)
    exp = jnp.sum((item_end[None, :] <= idx[:, None]).astype(i32), axis=1)
    item0 = item_end[exp] - n_e[exp]
    tile = first_tile[exp] + idx - item0
    lo = jnp.maximum(off[exp] - tile * MOE_TILE, 0)
    present = counts > 0
    ordinal = jnp.cumsum(present.astype(i32)) - 1
    eid = jnp.arange(N_EXPERTS, dtype=i32)
    is_kth = present[None, :] & (ordinal[None, :] == eid[:, None])
    kth = jnp.where(jnp.any(is_kth, axis=1), jnp.sum(jnp.where(is_kth, eid[None, :], 0), axis=1), -1)
    kth = jnp.concatenate([kth, jnp.full((W_LOOKAHEAD,), -1, i32)])
    items = (tile, exp, lo, idx == item0, ordinal[exp] % W_SLOTS, kth[ordinal[exp] + W_LOOKAHEAD],
             kth[:W_LOOKAHEAD], n_items.reshape(1))
    return tuple(v.astype(i32) for v in items)


def _combine_kernel(pos_ref, pos_next_ref, h_ref, wt_ref, g_ref, b_ref, y_ref, o_ref, buf_ref, sem):
    i = pl.program_id(0)
    slot = i % 2

    def gather(tile_pos_ref, into):
        for r in range(TOK_TILE):
            for k in range(2):
                _row_copy(y_ref, tile_pos_ref[0, k, r], buf_ref.at[into, k], r, sem.at[into]).start(priority=k)

    @pl.when(i == 0)
    def _():
        gather(pos_ref, 0)

    for other in range(2):
        @pl.when((i + 1 < pl.num_programs(0)) & (slot == 1 - other))
        def _(other=other):
            gather(pos_next_ref, other)

    for k in range(2):
        pltpu.make_async_copy(y_ref.at[pl.ds(0, TOK_TILE)], buf_ref.at[slot, k], sem.at[slot]).wait()
    f = wt_ref[:, 0:1] * buf_ref[slot, 0] + wt_ref[:, 1:2] * buf_ref[slot, 1]
    o_ref[...] = _layer_norm(ALPHA * h_ref[...] + f, g_ref[...], b_ref[...])


def _combine_ln(pos_tiles, h1, wt, g, b, y):
    n = h1.shape[0]
    steps = n // TOK_TILE
    return pl.pallas_call(
        _combine_kernel,
        grid=(steps,),
        in_specs=[
            pl.BlockSpec((1, 2, TOK_TILE), lambda i: (i, 0, 0), memory_space=pltpu.SMEM),
            pl.BlockSpec((1, 2, TOK_TILE), lambda i: (jnp.minimum(i + 1, steps - 1), 0, 0),
                         memory_space=pltpu.SMEM),
            pl.BlockSpec((TOK_TILE, D_MODEL), lambda i: (i, 0)),
            pl.BlockSpec((TOK_TILE, 2), lambda i: (i, 0)),
            pl.BlockSpec((1, D_MODEL), lambda i: (0, 0)),
            pl.BlockSpec((1, D_MODEL), lambda i: (0, 0)),
            pl.BlockSpec(memory_space=pl.ANY),
        ],
        out_specs=pl.BlockSpec((TOK_TILE, D_MODEL), lambda i: (i, 0)),
        out_shape=jax.ShapeDtypeStruct((n, D_MODEL), F32),
        scratch_shapes=[pltpu.VMEM((2, 2, TOK_TILE, D_MODEL), F32), pltpu.SemaphoreType.DMA((2,))],
        compiler_params=_params("arbitrary"),
        name="combine_ln",
    )(pos_tiles, pos_tiles, h1, wt, g, b, y)


def _dup_heads(w):
    d = w.shape[0]
    w = w.reshape(d, N_KV_HEADS, 1, HEAD_DIM)
    return jnp.broadcast_to(w, (d, N_KV_HEADS, 2, HEAD_DIM)).reshape(d, 2 * KV_WIDTH)


def kernel(x, meta_tokens, ln_in_g, ln_in_b, w_in, sinks, conv_w, w_branch_attn, w_branch_conv, w_out,
           ln1_g, ln1_b, router_group_w, router_group_b, router_expert_w, router_expert_b,
           w_gate, w_up, w_down, ln2_g, ln2_b):
    batch, seq, d = x.shape
    n = batch * seq
    assert d == D_MODEL and seq % ROW_TILE == 0 and w_in.shape[0] == DEPTH == 1
    row = lambda v: v.reshape(1, -1).astype(F32)
    x2d = x.reshape(n, d)
    g0, b0 = row(ln_in_g), row(ln_in_b)

    win = w_in[0]
    kcol, vcol, ccol = D_MODEL, D_MODEL + KV_WIDTH, D_MODEL + 2 * KV_WIDTH
    wkv_dup = jnp.concatenate([_dup_heads(win[:, kcol:vcol]), _dup_heads(win[:, vcol:ccol])], axis=1).astype(BF16)
    rw = jnp.zeros((ROUTE_ROWS, d), F32)
    rw = rw.at[:N_GROUPS].set(router_group_w[0].T).at[EXPERT_ROW0:EXPERT_ROW0 + N_EXPERTS].set(router_expert_w[0].T)
    rb = jnp.zeros((ROUTE_ROWS,), F32)
    rb = rb.at[:N_GROUPS].set(router_group_b[0]).at[EXPERT_ROW0:EXPERT_ROW0 + N_EXPERTS].set(router_expert_b[0])
    rb = jnp.broadcast_to(rb[:, None], (ROUTE_ROWS, LANE))

    hb, q, kd, vd, wall = _ln_qkv(x2d, g0, b0, win, wkv_dup, w_branch_attn[0], w_branch_conv[0], w_out[0])
    kvm, um = _meta_prep(meta_tokens.astype(F32), g0, b0, wkv_dup, wall)
    attn = _swa_attn(sinks[0].astype(F32), q, kd, vd, kvm, batch, seq // BLOCK)
    h1, logits_t = _mixer_tail(x2d, hb, attn, g0, b0, um, conv_w[0], wall,
                               row(ln1_g[0]), row(ln1_b[0]), rw.astype(BF16), rb, seq)
    _, top_w, pos, counts = _route(logits_t)
    pos_tiles = pos.reshape(2, n // TOK_TILE, TOK_TILE).transpose(1, 0, 2)
    xs = _dispatch(pos_tiles, h1)
    y = _moe_experts(_work_items(counts[:, 0]), xs, w_gate[0], w_up[0], w_down[0])
    out = _combine_ln(pos_tiles, h1, top_w.T, row(ln2_g[0]), row(ln2_b[0]), y)
    return out.reshape(batch, seq, d)
```

```python
import functools

import jax
import jax.numpy as jnp
from jax import lax
from jax.experimental import pallas as pl
from jax.experimental.pallas import tpu as pltpu

D_MODEL = 2048
N_META = 16
HEAD_DIM = 64
N_Q_HEADS = 32
N_KV_HEADS = 4
BLOCK = 128
KV_WIDTH = N_KV_HEADS * HEAD_DIM
CONV_K = 3
N_GROUPS = 4
EXPERTS_PER_GROUP = 8
N_EXPERTS = N_GROUPS * EXPERTS_PER_GROUP
D_EXPERT = D_MODEL // 4
LN_EPS = 1e-5
DEPTH = 1
ALPHA = (2.0 * DEPTH) ** 0.25
NEG_INF = -1e30
Q_SCALE = HEAD_DIM ** -0.5

V7X_VMEM_LIMIT = 56 * 1024 * 1024
LANE = 128
CHUNK = 512
N_CHUNKS = D_MODEL // CHUNK
ROW_TILE = 512
MOE_TILE = 256
TOK_TILE = 256
MAX_ITEMS = 2 * 8192 // MOE_TILE + N_EXPERTS
ROUTE_ROWS = 128
EXPERT_ROW0 = 8

BF16 = jnp.bfloat16
F32 = jnp.float32


def _layer_norm(x, g, b):
    mu = jnp.mean(x, axis=-1, keepdims=True)
    xc = x - mu
    var = jnp.mean(xc * xc, axis=-1, keepdims=True)
    return xc * lax.rsqrt(var + LN_EPS) * g + b


def _dot(a, b):
    return jnp.dot(a, b, preferred_element_type=F32)


def _dot_nt(a, b):
    return lax.dot_general(a, b, (((1,), (1,)), ((), ())), preferred_element_type=F32)


def _params(*sem):
    return pltpu.CompilerParams(dimension_semantics=sem, vmem_limit_bytes=V7X_VMEM_LIMIT)


W_B, W_C, W_H, W_GA, W_GC, W_BA, W_BC, W_OUT = (k * N_CHUNKS for k in range(8))
N_WBLOCKS = 8 * N_CHUNKS
IN_FIRST_MIX = (D_MODEL + 2 * KV_WIDTH) // CHUNK


def _meta_kernel(meta_ref, g_ref, b_ref, wkv_ref, wcc_ref, wch_ref, kv_ref, u_ref):
    hm = _layer_norm(meta_ref[...], g_ref[...], b_ref[...]).astype(BF16)
    kv_ref[...] = _dot(hm, wkv_ref[...]).astype(BF16)
    u_ref[...] = _dot(hm, wcc_ref[...]) * _dot(hm, wch_ref[...])


def _meta_prep(meta, g, b, wkv_dup, wall):
    n_kv = wkv_dup.shape[1] // N_CHUNKS
    return pl.pallas_call(
        _meta_kernel,
        grid=(N_CHUNKS,),
        in_specs=[
            pl.BlockSpec((N_META, D_MODEL), lambda j: (0, 0)),
            pl.BlockSpec((1, D_MODEL), lambda j: (0, 0)),
            pl.BlockSpec((1, D_MODEL), lambda j: (0, 0)),
            pl.BlockSpec((D_MODEL, n_kv), lambda j: (0, j)),
            pl.BlockSpec((D_MODEL, CHUNK), lambda j: (0, W_C + j)),
            pl.BlockSpec((D_MODEL, CHUNK), lambda j: (0, W_H + j)),
        ],
        out_specs=[
            pl.BlockSpec((N_META, n_kv), lambda j: (0, j)),
            pl.BlockSpec((N_META, CHUNK), lambda j: (0, j)),
        ],
        out_shape=[
            jax.ShapeDtypeStruct((N_META, wkv_dup.shape[1]), BF16),
            jax.ShapeDtypeStruct((N_META, D_MODEL), F32),
        ],
        compiler_params=_params("arbitrary"),
        name="meta_prep",
    )(meta, g, b, wkv_dup, wall, wall)


def _qkv_kernel(x_ref, g_ref, b_ref, wq_ref, wkv_ref, win_hbm, ba_hbm, bc_hbm, out_hbm,
                hb_ref, q_ref, k_ref, v_ref, wall_ref, wq_bf, stage, sem):
    i = pl.program_id(0)
    kw = 2 * KV_WIDTH

    def start_block(j, slot):
        def copy(src_hbm, block):
            col = pl.multiple_of(block * CHUNK, CHUNK)
            pltpu.make_async_copy(src_hbm.at[:, pl.ds(col, CHUNK)], stage.at[slot], sem.at[slot]).start()

        @pl.when(j < W_BA)
        def _():
            copy(win_hbm, IN_FIRST_MIX + j)

        for src_hbm, first in ((ba_hbm, W_BA), (bc_hbm, W_BC), (out_hbm, W_OUT)):
            @pl.when((j >= first) & (j < first + N_CHUNKS))
            def _(src_hbm=src_hbm, first=first):
                copy(src_hbm, j - first)

    @pl.when(i == 0)
    def _():
        start_block(0, 0)
        wq_bf[...] = wq_ref[...].astype(BF16)

    slot = i % 2
    pltpu.make_async_copy(ba_hbm.at[:, pl.ds(0, CHUNK)], stage.at[slot], sem.at[slot]).wait()

    @pl.when(i + 1 < N_WBLOCKS)
    def _():
        start_block(i + 1, 1 - slot)

    wall_ref[...] = stage[slot].astype(BF16)
    hb = _layer_norm(x_ref[...], g_ref[...], b_ref[...]).astype(BF16)
    hb_ref[...] = hb
    q_ref[...] = (_dot(hb, wq_bf[...]) * Q_SCALE).astype(BF16)
    k_ref[...] = _dot(hb, wkv_ref[:, :kw]).astype(BF16)
    v_ref[...] = _dot(hb, wkv_ref[:, kw:]).astype(BF16)


def _ln_qkv(x2d, g, b, win, wkv_dup, wba, wbc, wout):
    n = x2d.shape[0]
    tile = n // N_WBLOCKS
    assert tile * N_WBLOCKS == n and tile % 16 == 0
    kw = 2 * KV_WIDTH
    row = lambda i: (i, 0)
    const = lambda i: (0, 0)
    once = dict(pipeline_mode=pl.Buffered(1))
    any_spec = pl.BlockSpec(memory_space=pl.ANY)
    return pl.pallas_call(
        _qkv_kernel,
        grid=(N_WBLOCKS,),
        in_specs=[
            pl.BlockSpec((tile, D_MODEL), row),
            pl.BlockSpec((1, D_MODEL), const),
            pl.BlockSpec((1, D_MODEL), const),
            pl.BlockSpec((D_MODEL, D_MODEL), const, **once),
            pl.BlockSpec(wkv_dup.shape, const, **once),
            any_spec, any_spec, any_spec, any_spec,
        ],
        out_specs=[
            pl.BlockSpec((tile, D_MODEL), row),
            pl.BlockSpec((tile, D_MODEL), row),
            pl.BlockSpec((tile, kw), row),
            pl.BlockSpec((tile, kw), row),
            pl.BlockSpec((D_MODEL, CHUNK), lambda i: (0, i)),
        ],
        out_shape=[
            jax.ShapeDtypeStruct((n, D_MODEL), BF16),
            jax.ShapeDtypeStruct((n, D_MODEL), BF16),
            jax.ShapeDtypeStruct((n, kw), BF16),
            jax.ShapeDtypeStruct((n, kw), BF16),
            jax.ShapeDtypeStruct((D_MODEL, N_WBLOCKS * CHUNK), BF16),
        ],
        scratch_shapes=[
            pltpu.VMEM((D_MODEL, D_MODEL), BF16),
            pltpu.VMEM((2, D_MODEL, CHUNK), F32),
            pltpu.SemaphoreType.DMA((2,)),
        ],
        compiler_params=_params("arbitrary"),
        name="ln_qkv",
    )(x2d, g, b, win, wkv_dup, win, wba, wbc, wout)


PAIRS = N_Q_HEADS // N_KV_HEADS // 2
QROWS = PAIRS * BLOCK


class _AttnBlock:
    def __init__(self, sink_ref, q_ref, kp_ref, kc_ref, vp_ref, vc_ref, kvm_ref, has_prev):
        self.sink_ref, self.q_ref, self.kvm_ref, self.has_prev = sink_ref, q_ref, kvm_ref, has_prev
        self.k_refs, self.v_refs = (kp_ref, kc_ref), (vp_ref, vc_ref)
        self.lane = lax.broadcasted_iota(jnp.int32, (BLOCK, LANE), 1)
        qi = lax.broadcasted_iota(jnp.int32, (QROWS, BLOCK), 0) % BLOCK
        kj = lax.broadcasted_iota(jnp.int32, (QROWS, BLOCK), 1)
        self.from_prev = kj > qi
        self.is_meta = kj < N_META
        self.rowpair = lax.broadcasted_iota(jnp.int32, (QROWS, 1), 0) // BLOCK
        self.zpad = jnp.zeros((BLOCK - N_META, LANE), BF16)

    def _masked_rows(self, refs, meta_cols, g, half):
        keep = (self.lane >= HEAD_DIM) if half else (self.lane < HEAD_DIM)
        gs = slice(g * LANE, (g + 1) * LANE)
        blocks = [r[:, gs] for r in refs] + [jnp.concatenate([self.kvm_ref[:, meta_cols], self.zpad], axis=0)]
        return jnp.concatenate([jnp.where(keep, b, jnp.zeros((), BF16)) for b in blocks], axis=0)

    def scores(self, g, half):
        qs = jnp.concatenate([self.q_ref[:, (g * PAIRS + p) * LANE:(g * PAIRS + p + 1) * LANE]
                              for p in range(PAIRS)], axis=0)
        return _dot_nt(qs, self._masked_rows(self.k_refs, slice(g * LANE, (g + 1) * LANE), g, half))

    def probs(self, s, g, half):
        s_prev = jnp.where(self.has_prev, s[:, :BLOCK], NEG_INF)
        s_band = jnp.where(self.from_prev, s_prev, s[:, BLOCK:2 * BLOCK])
        s_meta = jnp.where(self.is_meta, s[:, 2 * BLOCK:], NEG_INF)
        sink = jnp.zeros((QROWS, 1), F32)
        for p in range(PAIRS):
            sink = jnp.where(self.rowpair == p, self.sink_ref[g * 2 * PAIRS + 2 * p + half], sink)
        m = jnp.maximum(jnp.max(jnp.maximum(s_band, s_meta), axis=-1, keepdims=True), sink)
        e_band = jnp.exp(s_band - m)
        e_meta = jnp.exp(s_meta - m)
        den = jnp.sum(e_band + e_meta, axis=-1, keepdims=True) + jnp.exp(sink - m)
        p_band = e_band / den
        return jnp.concatenate([jnp.where(self.from_prev, p_band, 0.0).astype(BF16),
                                jnp.where(self.from_prev, 0.0, p_band).astype(BF16),
                                (e_meta / den).astype(BF16)], axis=1)

    def values(self, probs, g, half):
        meta_cols = slice((N_KV_HEADS + g) * LANE, (N_KV_HEADS + g + 1) * LANE)
        return _dot(probs, self._masked_rows(self.v_refs, meta_cols, g, half))

    @staticmethod
    def store(o_ref, g, acc):
        for p in range(PAIRS):
            col = (g * PAIRS + p) * LANE
            o_ref[:, col:col + LANE] = acc[p * BLOCK:(p + 1) * BLOCK].astype(BF16)


TAIL_STEPS = 2 * N_CHUNKS + 1


def _lane_concat(ref):
    return jnp.concatenate([ref[k] for k in range(ref.shape[0])], axis=1)


def _tail_kernel(sink_ref, x_ref, hb_ref, q_ref, kp_ref, kc_ref, vp_ref, vc_ref, kvm_ref,
                 g0_ref, b0_ref, um_ref, cw_ref, wa_ref, wb_ref, wc_ref, wd_ref,
                 g1_ref, b1_ref, rw_ref, rb_ref, h1_ref, lg_ref,
                 uext_ref, carry_ref, z_ref, m_ref, attn_ref, *, tiles_per_seq):
    i = pl.program_id(0)
    s = pl.program_id(1)
    tm = ROW_TILE

    @pl.when((i == 0) & (s == 0))
    def _():
        carry_ref[...] = jnp.zeros_like(carry_ref)

    @pl.when(s < N_CHUNKS)
    def _():
        hb = hb_ref[...]
        first = i % tiles_per_seq == 0
        attn_out = attn_ref.at[pl.ds(pl.multiple_of(s * BLOCK, BLOCK), BLOCK)]
        attn = _AttnBlock(sink_ref, q_ref, kp_ref, kc_ref, vp_ref, vc_ref, kvm_ref,
                          has_prev=(i % tiles_per_seq) * N_CHUNKS + s > 0)
        units = [(g, half) for g in range(N_KV_HEADS) for half in range(2)]
        half_cols = CHUNK // 2
        dots = [(w_ref, slice(h * half_cols, (h + 1) * half_cols))
                for h in range(2) for w_ref in (wa_ref, wb_ref, wc_ref)]

        def conv_half(cs, c_h, h_h, b_h):
            u = c_h * h_h
            uext_ref[0:8, cs] = jnp.where(first, um_ref[N_META - 8:N_META, cs], carry_ref[s, :, cs])
            uext_ref[8:tm + 8, cs] = u
            conv = (cw_ref[0:1, cs] * uext_ref[6:tm + 6, cs] + cw_ref[1:2, cs] * uext_ref[7:tm + 7, cs]
                    + cw_ref[2:3, cs] * u)
            carry_ref[s, :, cs] = u[tm - 8:tm, :]
            z_ref[s, :, cs] = (b_h * conv).astype(BF16)

        scores = attn.scores(*units[0])
        outs, acc = [], None
        for k, unit in enumerate(units):
            if k < len(dots):
                w_ref, cs = dots[k]
                outs.append(_dot(hb, w_ref[:, cs]))
            probs = attn.probs(scores, *unit)
            part = attn.values(probs, *unit)
            if k + 1 < len(units):
                scores = attn.scores(*units[k + 1])
            acc = part if unit[1] == 0 else acc + part
            if unit[1] == 1:
                attn.store(attn_out, unit[0], acc)
            if k % 3 == 2 and k < len(dots):
                conv_half(dots[k][1], *outs[k - 2:k + 1])

    @pl.when((s >= N_CHUNKS) & (s < 2 * N_CHUNKS))
    def _():
        hb = hb_ref[...]
        conv_part = jax.nn.sigmoid(_dot(hb, wb_ref[...])) * _dot(_lane_concat(z_ref), wd_ref[...])
        gate_attn = jax.nn.sigmoid(_dot(hb, wa_ref[...]))
        m_ref[s - N_CHUNKS] = (gate_attn * _dot(attn_ref[...], wc_ref[...]) + conv_part).astype(BF16)

    @pl.when(s == 2 * N_CHUNKS)
    def _():
        h = _layer_norm(x_ref[...], g0_ref[...], b0_ref[...])
        m = _lane_concat(m_ref)
        for k, w_ref in enumerate((wa_ref, wb_ref, wc_ref, wd_ref)):
            cs = slice(k * CHUNK, (k + 1) * CHUNK)
            h1_ref[:, cs] = ALPHA * h[:, cs] + _dot(m, w_ref[...])
        h1 = _layer_norm(h1_ref[...], g1_ref[...], b1_ref[...])
        h1_ref[...] = h1
        lg_ref[...] = _dot_nt(rw_ref[...], h1.astype(BF16)) + rb_ref[...][:, 0:1]


def _mixer_tail(sinks, x2d, hb, q, kd, vd, kvm, g0, b0, um, cw, wall, g1, b1, rw, rb, seq):
    n = x2d.shape[0]
    nc = N_CHUNKS
    assert ROW_TILE == nc * BLOCK
    tiles_per_seq = seq // ROW_TILE
    kw = 2 * KV_WIDTH
    cur = lambda i, s: (i * nc + jnp.minimum(s, nc - 1), 0)
    prev = lambda i, s: (jnp.maximum(i * nc + jnp.minimum(s, nc - 1) - 1, (i // tiles_per_seq) * (seq // BLOCK)), 0)

    def slot(conv_first, merge_first, out_block, hold=False):
        def index(i, s):
            conv = conv_first if hold else conv_first + s
            return 0, jnp.where(s < nc, conv, jnp.where(s < 2 * nc, merge_first + s - nc, W_OUT + out_block))
        return pl.BlockSpec((D_MODEL, CHUNK), index)

    lo = lambda i, s: (0, jnp.minimum(s, nc - 1))
    const = lambda i, s: (0, 0)
    row = lambda i, s: (i, 0)
    return pl.pallas_call(
        functools.partial(_tail_kernel, tiles_per_seq=tiles_per_seq),
        grid=(n // ROW_TILE, TAIL_STEPS),
        in_specs=[
            pl.BlockSpec(memory_space=pltpu.SMEM),
            pl.BlockSpec((ROW_TILE, D_MODEL), row),
            pl.BlockSpec((ROW_TILE, D_MODEL), row),
            pl.BlockSpec((BLOCK, D_MODEL), cur),
            pl.BlockSpec((BLOCK, kw), prev),
            pl.BlockSpec((BLOCK, kw), cur),
            pl.BlockSpec((BLOCK, kw), prev),
            pl.BlockSpec((BLOCK, kw), cur),
            pl.BlockSpec((N_META, 2 * kw), const),
            pl.BlockSpec((1, D_MODEL), const),
            pl.BlockSpec((1, D_MODEL), const),
            pl.BlockSpec((N_META, CHUNK), lo),
            pl.BlockSpec((CONV_K, CHUNK), lo),
            slot(W_C, W_GA, 0),
            slot(W_H, W_GC, 1),
            slot(W_B, W_BA, 2),
            slot(W_BC, W_BC, 3, hold=True),
            pl.BlockSpec((1, D_MODEL), const),
            pl.BlockSpec((1, D_MODEL), const),
            pl.BlockSpec((ROUTE_ROWS, D_MODEL), const),
            pl.BlockSpec((ROUTE_ROWS, LANE), const),
        ],
        out_specs=[
            pl.BlockSpec((ROW_TILE, D_MODEL), row),
            pl.BlockSpec((ROUTE_ROWS, ROW_TILE), lambda i, s: (0, i)),
        ],
        out_shape=[
            jax.ShapeDtypeStruct((n, D_MODEL), F32),
            jax.ShapeDtypeStruct((ROUTE_ROWS, n), F32),
        ],
        scratch_shapes=[
            pltpu.VMEM((ROW_TILE + 8, CHUNK), F32),
            pltpu.VMEM((nc, 8, CHUNK), F32),
            pltpu.VMEM((nc, ROW_TILE, CHUNK), BF16),
            pltpu.VMEM((nc, ROW_TILE, CHUNK), BF16),
            pltpu.VMEM((ROW_TILE, D_MODEL), BF16),
        ],
        compiler_params=_params("arbitrary", "arbitrary"),
        name="mixer_tail",
    )(sinks, x2d, hb, q, kd, kd, vd, vd, kvm, g0, b0, um, cw, wall, wall, wall, wall, g1, b1, rw, rb)


RCHUNK = 1024
PBLK = 256


def _route_kernel(lg_ref, e_ref, w_ref, pos_ref, cnt_ref, rank_ref):
    n = lg_ref.shape[1]
    epg = EXPERTS_PER_GROUP

    def route_chunk(c, carry):
        sl = pl.ds(pl.multiple_of(c * RCHUNK, RCHUNK), RCHUNK)
        gl = [lg_ref[gi:gi + 1, sl] for gi in range(N_GROUPS)]
        gmax = functools.reduce(jnp.maximum, gl)
        ge = [jnp.exp(v - gmax) for v in gl]
        gsum = functools.reduce(jnp.add, ge)
        gp = [v / gsum for v in ge]
        p_group = functools.reduce(jnp.maximum, gp)
        gsel = jnp.full_like(p_group, float(N_GROUPS - 1))
        for gi in range(N_GROUPS - 2, -1, -1):
            gsel = jnp.where(gp[gi] >= p_group, float(gi), gsel)
        e_in = lg_ref[EXPERT_ROW0 + (N_GROUPS - 1) * epg:EXPERT_ROW0 + N_GROUPS * epg, sl]
        for gi in range(N_GROUPS - 2, -1, -1):
            e_in = jnp.where(gsel == float(gi),
                             lg_ref[EXPERT_ROW0 + gi * epg:EXPERT_ROW0 + (gi + 1) * epg, sl], e_in)
        ee = jnp.exp(e_in - jnp.max(e_in, axis=0, keepdims=True))
        pe = ee / jnp.sum(ee, axis=0, keepdims=True)
        ridx = lax.broadcasted_iota(jnp.int32, pe.shape, 0).astype(F32)
        p0 = jnp.max(pe, axis=0, keepdims=True)
        i0 = jnp.min(jnp.where(pe >= p0, ridx, float(epg)), axis=0, keepdims=True)
        pe2 = jnp.where(ridx == i0, -1.0, pe)
        p1 = jnp.max(pe2, axis=0, keepdims=True)
        i1 = jnp.min(jnp.where(pe2 >= p1, ridx, float(epg)), axis=0, keepdims=True)
        den = p0 + p1
        e_ref[0:1, sl] = (gsel * epg + i0).astype(jnp.int32)
        e_ref[1:2, sl] = (gsel * epg + i1).astype(jnp.int32)
        w_ref[0:1, sl] = p0 / den * p_group
        w_ref[1:2, sl] = p1 / den * p_group
        return carry

    lax.fori_loop(0, n // RCHUNK, route_chunk, 0)

    eid = lax.broadcasted_iota(jnp.int32, (N_EXPERTS, PBLK), 0)
    before = jnp.where(lax.broadcasted_iota(jnp.int32, (PBLK, PBLK), 0)
                       < lax.broadcasted_iota(jnp.int32, (PBLK, PBLK), 1), 1.0, 0.0).astype(BF16)
    ones = jnp.ones((PBLK, LANE), BF16)
    nblk = n // PBLK

    def onehot(k, blk):
        sl = pl.ds(pl.multiple_of(blk * PBLK, PBLK), PBLK)
        return sl, eid == e_ref[pl.ds(k, 1), sl]

    run = jnp.zeros((N_EXPERTS, LANE), F32)
    for k in range(2):
        def count_block(blk, run, k=k):
            sl, hit = onehot(k, blk)
            hit_bf = jnp.where(hit, 1.0, 0.0).astype(BF16)
            prior = _dot(hit_bf, before) + jnp.concatenate([run] * (PBLK // LANE), axis=1)
            rank_ref[pl.ds(k, 1), sl] = jnp.sum(jnp.where(hit, prior, 0.0), axis=0, keepdims=True)
            return run + _dot(hit_bf, ones)
        run = lax.fori_loop(0, nblk, count_block, run)

    cnt_ref[...] = run.astype(jnp.int32)
    hi = jnp.floor(run * (1.0 / LANE))
    lo = run - hi * LANE
    below = jnp.where(lax.broadcasted_iota(jnp.int32, (N_EXPERTS, N_EXPERTS), 1)
                      < lax.broadcasted_iota(jnp.int32, (N_EXPERTS, N_EXPERTS), 0), 1.0, 0.0).astype(BF16)
    off = _dot(below, hi.astype(BF16)) * LANE + _dot(below, lo.astype(BF16))
    off = jnp.concatenate([off] * (PBLK // LANE), axis=1)

    for k in range(2):
        def place_block(blk, carry, k=k):
            sl, hit = onehot(k, blk)
            base = jnp.sum(jnp.where(hit, off, 0.0), axis=0, keepdims=True)
            pos_ref[pl.ds(k, 1), sl] = (base + rank_ref[pl.ds(k, 1), sl]).astype(jnp.int32)
            return carry
        lax.fori_loop(0, nblk, place_block, 0)


def _route(logits_t):
    n = logits_t.shape[1]
    return pl.pallas_call(
        _route_kernel,
        out_shape=[
            jax.ShapeDtypeStruct((2, n), jnp.int32),
            jax.ShapeDtypeStruct((2, n), F32),
            jax.ShapeDtypeStruct((2, n), jnp.int32),
            jax.ShapeDtypeStruct((N_EXPERTS, LANE), jnp.int32),
        ],
        scratch_shapes=[pltpu.VMEM((2, n), F32)],
        compiler_params=pltpu.CompilerParams(vmem_limit_bytes=V7X_VMEM_LIMIT),
        name="route",
    )(logits_t)


def _row_copy(src_ref, src_row, dst_ref, dst_row, sem):
    return pltpu.make_async_copy(src_ref.at[pl.ds(src_row, 1)], dst_ref.at[pl.ds(dst_row, 1)], sem)


def _dispatch_kernel(pos_ref, h_ref, xs_ref, sem):
    for r in range(TOK_TILE):
        for k in range(2):
            _row_copy(h_ref, r, xs_ref, pos_ref[0, k, r], sem).start(priority=k)
    for k in range(2):
        pltpu.make_async_copy(h_ref, xs_ref.at[pl.ds(0, TOK_TILE)], sem).wait()


def _dispatch(pos_tiles, h1):
    n = h1.shape[0]
    return pl.pallas_call(
        _dispatch_kernel,
        grid=(n // TOK_TILE,),
        in_specs=[
            pl.BlockSpec((1, 2, TOK_TILE), lambda i: (i, 0, 0), memory_space=pltpu.SMEM),
            pl.BlockSpec((TOK_TILE, D_MODEL), lambda i: (i, 0)),
        ],
        out_specs=pl.BlockSpec(memory_space=pl.ANY),
        out_shape=jax.ShapeDtypeStruct((2 * n, D_MODEL), F32),
        scratch_shapes=[pltpu.SemaphoreType.DMA],
        compiler_params=_params("arbitrary"),
        name="dispatch",
    )(pos_tiles, h1)


W_SLOTS = 2
W_LOOKAHEAD = W_SLOTS - 1
def _moe_kernel(tile_ref, exp_ref, lo_ref, first_ref, slot_ref, ahead_ref, head_ref, n_ref,
                xs_ref, wg_hbm, wu_hbm, wd_hbm, y_ref,
                wg_buf, wu_buf, wd_buf, ytile_ref, sem):
    i = pl.program_id(0)

    def weight_copies(expert, slot):
        return (pltpu.make_async_copy(wg_hbm.at[expert], wg_buf.at[slot], sem.at[slot, 0]),
                pltpu.make_async_copy(wu_hbm.at[expert], wu_buf.at[slot], sem.at[slot, 1]),
                pltpu.make_async_copy(wd_hbm.at[expert], wd_buf.at[slot], sem.at[slot, 2]))

    @pl.when(i == 0)
    def _():
        ytile_ref[...] = jnp.zeros_like(ytile_ref)
        for k in range(W_LOOKAHEAD):
            @pl.when(head_ref[k] >= 0)
            def _(k=k):
                for copy in weight_copies(head_ref[k], k):
                    copy.start()

    valid = i < n_ref[0]

    @pl.when(valid & (first_ref[i] == 1))
    def _():
        slot = slot_ref[i]
        for copy in weight_copies(exp_ref[i], slot):
            copy.wait()

        @pl.when(ahead_ref[i] >= 0)
        def _():
            for copy in weight_copies(ahead_ref[i], (slot + W_LOOKAHEAD) % W_SLOTS):
                copy.start()

    @pl.when(valid)
    def _():
        slot = slot_ref[i]
        x = xs_ref[...].astype(BF16)
        g = _dot(x, wg_buf[slot].astype(BF16))
        u = _dot(x, wu_buf[slot].astype(BF16))
        a = (g * jax.nn.sigmoid(g) * u).astype(BF16)
        y = _dot(a, wd_buf[slot].astype(BF16))
        row = lax.broadcasted_iota(jnp.int32, (MOE_TILE, 1), 0)
        merged = jnp.where(row >= lo_ref[i], y, ytile_ref[...])
        ytile_ref[...] = merged
        y_ref[...] = merged


def _moe_experts(items, xs, wg, wu, wd):
    rows = xs.shape[0]
    tile_map = lambda i, t, *_: (t[i], 0)
    grid_spec = pltpu.PrefetchScalarGridSpec(
        num_scalar_prefetch=len(items),
        grid=(MAX_ITEMS,),
        in_specs=[
            pl.BlockSpec((MOE_TILE, D_MODEL), tile_map),
            pl.BlockSpec(memory_space=pl.ANY),
            pl.BlockSpec(memory_space=pl.ANY),
            pl.BlockSpec(memory_space=pl.ANY),
        ],
        out_specs=pl.BlockSpec((MOE_TILE, D_MODEL), tile_map),
        scratch_shapes=[
            pltpu.VMEM((W_SLOTS, D_MODEL, D_EXPERT), F32),
            pltpu.VMEM((W_SLOTS, D_MODEL, D_EXPERT), F32),
            pltpu.VMEM((W_SLOTS, D_EXPERT, D_MODEL), F32),
            pltpu.VMEM((MOE_TILE, D_MODEL), F32),
            pltpu.SemaphoreType.DMA((W_SLOTS, 3)),
        ],
    )
    return pl.pallas_call(
        _moe_kernel,
        grid_spec=grid_spec,
        out_shape=jax.ShapeDtypeStruct((rows, D_MODEL), F32),
        compiler_params=_params("arbitrary"),
        name="moe_experts",
    )(*items, xs, wg, wu, wd)


def _work_items(counts):
    i32 = jnp.int32
    off = jnp.cumsum(counts) - counts
    end = off + counts
    first_tile = off // MOE_TILE
    n_e = jnp.where(counts > 0, (end - 1) // MOE_TILE - first_tile + 1, 0)
    item_end = jnp.cumsum(n_e)
    n_items = item_end[-1]
    idx = jnp.minimum(jnp.arange(MAX_ITEMS, dtype=i32), n_items - 1)
    exp = jnp.sum((item_end[None, :] <= idx[:, None]).astype(i32), axis=1)
    item0 = item_end[exp] - n_e[exp]
    tile = first_tile[exp] + idx - item0
    lo = jnp.maximum(off[exp] - tile * MOE_TILE, 0)
    present = counts > 0
    ordinal = jnp.cumsum(present.astype(i32)) - 1
    eid = jnp.arange(N_EXPERTS, dtype=i32)
    is_kth = present[None, :] & (ordinal[None, :] == eid[:, None])
    kth = jnp.where(jnp.any(is_kth, axis=1), jnp.sum(jnp.where(is_kth, eid[None, :], 0), axis=1), -1)
    kth = jnp.concatenate([kth, jnp.full((W_LOOKAHEAD,), -1, i32)])
    items = (tile, exp, lo, idx == item0, ordinal[exp] % W_SLOTS, kth[ordinal[exp] + W_LOOKAHEAD],
             kth[:W_LOOKAHEAD], n_items.reshape(1))
    return tuple(v.astype(i32) for v in items)


def _combine_kernel(pos_ref, pos_next_ref, h_ref, wt_ref, g_ref, b_ref, y_ref, o_ref, buf_ref, sem):
    i = pl.program_id(0)
    slot = i % 2

    def gather(tile_pos_ref, into):
        for r in range(TOK_TILE):
            for k in range(2):
                _row_copy(y_ref, tile_pos_ref[0, k, r], buf_ref.at[into, k], r, sem.at[into]).start(priority=k)

    @pl.when(i == 0)
    def _():
        gather(pos_ref, 0)

    for other in range(2):
        @pl.when((i + 1 < pl.num_programs(0)) & (slot == 1 - other))
        def _(other=other):
            gather(pos_next_ref, other)

    for k in range(2):
        pltpu.make_async_copy(y_ref.at[pl.ds(0, TOK_TILE)], buf_ref.at[slot, k], sem.at[slot]).wait()
    f = wt_ref[:, 0:1] * buf_ref[slot, 0] + wt_ref[:, 1:2] * buf_ref[slot, 1]
    o_ref[...] = _layer_norm(ALPHA * h_ref[...] + f, g_ref[...], b_ref[...])


def _combine_ln(pos_tiles, h1, wt, g, b, y):
    n = h1.shape[0]
    steps = n // TOK_TILE
    return pl.pallas_call(
        _combine_kernel,
        grid=(steps,),
        in_specs=[
            pl.BlockSpec((1, 2, TOK_TILE), lambda i: (i, 0, 0), memory_space=pltpu.SMEM),
            pl.BlockSpec((1, 2, TOK_TILE), lambda i: (jnp.minimum(i + 1, steps - 1), 0, 0),
                         memory_space=pltpu.SMEM),
            pl.BlockSpec((TOK_TILE, D_MODEL), lambda i: (i, 0)),
            pl.BlockSpec((TOK_TILE, 2), lambda i: (i, 0)),
            pl.BlockSpec((1, D_MODEL), lambda i: (0, 0)),
            pl.BlockSpec((1, D_MODEL), lambda i: (0, 0)),
            pl.BlockSpec(memory_space=pl.ANY),
        ],
        out_specs=pl.BlockSpec((TOK_TILE, D_MODEL), lambda i: (i, 0)),
        out_shape=jax.ShapeDtypeStruct((n, D_MODEL), F32),
        scratch_shapes=[pltpu.VMEM((2, 2, TOK_TILE, D_MODEL), F32), pltpu.SemaphoreType.DMA((2,))],
        compiler_params=_params("arbitrary"),
        name="combine_ln",
    )(pos_tiles, pos_tiles, h1, wt, g, b, y)


def _dup_heads(w):
    d = w.shape[0]
    w = w.reshape(d, N_KV_HEADS, 1, HEAD_DIM)
    return jnp.broadcast_to(w, (d, N_KV_HEADS, 2, HEAD_DIM)).reshape(d, 2 * KV_WIDTH)


def kernel(x, meta_tokens, ln_in_g, ln_in_b, w_in, sinks, conv_w, w_branch_attn, w_branch_conv, w_out,
           ln1_g, ln1_b, router_group_w, router_group_b, router_expert_w, router_expert_b,
           w_gate, w_up, w_down, ln2_g, ln2_b):
    batch, seq, d = x.shape
    n = batch * seq
    assert d == D_MODEL and seq % ROW_TILE == 0 and w_in.shape[0] == DEPTH == 1
    row = lambda v: v.reshape(1, -1).astype(F32)
    x2d = x.reshape(n, d)
    g0, b0 = row(ln_in_g), row(ln_in_b)

    win = w_in[0]
    kcol, vcol, ccol = D_MODEL, D_MODEL + KV_WIDTH, D_MODEL + 2 * KV_WIDTH
    wkv_dup = jnp.concatenate([_dup_heads(win[:, kcol:vcol]), _dup_heads(win[:, vcol:ccol])], axis=1).astype(BF16)
    rw = jnp.zeros((ROUTE_ROWS, d), F32)
    rw = rw.at[:N_GROUPS].set(router_group_w[0].T).at[EXPERT_ROW0:EXPERT_ROW0 + N_EXPERTS].set(router_expert_w[0].T)
    rb = jnp.zeros((ROUTE_ROWS,), F32)
    rb = rb.at[:N_GROUPS].set(router_group_b[0]).at[EXPERT_ROW0:EXPERT_ROW0 + N_EXPERTS].set(router_expert_b[0])
    rb = jnp.broadcast_to(rb[:, None], (ROUTE_ROWS, LANE))

    hb, q, kd, vd, wall = _ln_qkv(x2d, g0, b0, win, wkv_dup, w_branch_attn[0], w_branch_conv[0], w_out[0])
    kvm, um = _meta_prep(meta_tokens.astype(F32), g0, b0, wkv_dup, wall)
    h1, logits_t = _mixer_tail(sinks[0].astype(F32), x2d, hb, q, kd, vd, kvm, g0, b0, um, conv_w[0], wall,
                               row(ln1_g[0]), row(ln1_b[0]), rw.astype(BF16), rb, seq)
    _, top_w, pos, counts = _route(logits_t)
    pos_tiles = pos.reshape(2, n // TOK_TILE, TOK_TILE).transpose(1, 0, 2)
    xs = _dispatch(pos_tiles, h1)
    y = _moe_experts(_work_items(counts[:, 0]), xs, w_gate[0], w_up[0], w_down[0])
    out = _combine_ln(pos_tiles, h1, top_w.T, row(ln2_g[0]), row(ln2_b[0]), y)
    return out.reshape(batch, seq, d)
```

```python
import functools

import jax
import jax.numpy as jnp
from jax import lax
from jax.experimental import pallas as pl
from jax.experimental.pallas import tpu as pltpu

D_MODEL = 2048
N_META = 16
HEAD_DIM = 64
N_Q_HEADS = 32
N_KV_HEADS = 4
BLOCK = 128
KV_WIDTH = N_KV_HEADS * HEAD_DIM
CONV_K = 3
N_GROUPS = 4
EXPERTS_PER_GROUP = 8
N_EXPERTS = N_GROUPS * EXPERTS_PER_GROUP
D_EXPERT = D_MODEL // 4
LN_EPS = 1e-5
DEPTH = 1
ALPHA = (2.0 * DEPTH) ** 0.25
NEG_INF = -1e30
Q_SCALE = HEAD_DIM ** -0.5

V7X_VMEM_LIMIT = 56 * 1024 * 1024
V7X_VMEM_LIMIT_TAIL = 58 * 1024 * 1024
LANE = 128
CHUNK = 512
N_CHUNKS = D_MODEL // CHUNK
ROW_TILE = 512
MOE_TILE = 256
TOK_TILE = 256
MAX_ITEMS = 2 * 8192 // MOE_TILE + N_EXPERTS
ROUTE_ROWS = 128
EXPERT_ROW0 = 8

BF16 = jnp.bfloat16
F32 = jnp.float32


def _layer_norm(x, g, b):
    mu = jnp.mean(x, axis=-1, keepdims=True)
    xc = x - mu
    var = jnp.mean(xc * xc, axis=-1, keepdims=True)
    return xc * lax.rsqrt(var + LN_EPS) * g + b


def _dot(a, b):
    return jnp.dot(a, b, preferred_element_type=F32)


def _dot_nt(a, b):
    return lax.dot_general(a, b, (((1,), (1,)), ((), ())), preferred_element_type=F32)


def _params(*sem, vmem_limit=V7X_VMEM_LIMIT):
    return pltpu.CompilerParams(dimension_semantics=sem, vmem_limit_bytes=vmem_limit)


W_B, W_C, W_H, W_GA, W_GC, W_BA, W_BC, W_OUT = (k * N_CHUNKS for k in range(8))
N_WBLOCKS = 8 * N_CHUNKS
IN_FIRST_MIX = (D_MODEL + 2 * KV_WIDTH) // CHUNK


def _meta_kernel(meta_ref, g_ref, b_ref, wkv_ref, wcc_ref, wch_ref, kv_ref, u_ref):
    hm = _layer_norm(meta_ref[...], g_ref[...], b_ref[...]).astype(BF16)
    kv_ref[...] = _dot(hm, wkv_ref[...]).astype(BF16)
    u_ref[...] = _dot(hm, wcc_ref[...]) * _dot(hm, wch_ref[...])


def _meta_prep(meta, g, b, wkv_dup, wall):
    n_kv = wkv_dup.shape[1] // N_CHUNKS
    return pl.pallas_call(
        _meta_kernel,
        grid=(N_CHUNKS,),
        in_specs=[
            pl.BlockSpec((N_META, D_MODEL), lambda j: (0, 0)),
            pl.BlockSpec((1, D_MODEL), lambda j: (0, 0)),
            pl.BlockSpec((1, D_MODEL), lambda j: (0, 0)),
            pl.BlockSpec((D_MODEL, n_kv), lambda j: (0, j)),
            pl.BlockSpec((D_MODEL, CHUNK), lambda j: (0, W_C + j)),
            pl.BlockSpec((D_MODEL, CHUNK), lambda j: (0, W_H + j)),
        ],
        out_specs=[
            pl.BlockSpec((N_META, n_kv), lambda j: (0, j)),
            pl.BlockSpec((N_META, CHUNK), lambda j: (0, j)),
        ],
        out_shape=[
            jax.ShapeDtypeStruct((N_META, wkv_dup.shape[1]), BF16),
            jax.ShapeDtypeStruct((N_META, D_MODEL), F32),
        ],
        compiler_params=_params("arbitrary"),
        name="meta_prep",
    )(meta, g, b, wkv_dup, wall, wall)


def _qkv_kernel(x_ref, g_ref, b_ref, wq_ref, wkv_ref, win_hbm, ba_hbm, bc_hbm, out_hbm,
                hb_ref, q_ref, k_ref, v_ref, wall_ref, wq_bf, stage, sem):
    i = pl.program_id(0)
    kw = 2 * KV_WIDTH

    def start_block(j, slot):
        def copy(src_hbm, block):
            col = pl.multiple_of(block * CHUNK, CHUNK)
            pltpu.make_async_copy(src_hbm.at[:, pl.ds(col, CHUNK)], stage.at[slot], sem.at[slot]).start()

        @pl.when(j < W_BA)
        def _():
            copy(win_hbm, IN_FIRST_MIX + j)

        for src_hbm, first in ((ba_hbm, W_BA), (bc_hbm, W_BC), (out_hbm, W_OUT)):
            @pl.when((j >= first) & (j < first + N_CHUNKS))
            def _(src_hbm=src_hbm, first=first):
                copy(src_hbm, j - first)

    @pl.when(i == 0)
    def _():
        start_block(0, 0)
        wq_bf[...] = wq_ref[...].astype(BF16)

    slot = i % 2
    pltpu.make_async_copy(ba_hbm.at[:, pl.ds(0, CHUNK)], stage.at[slot], sem.at[slot]).wait()

    @pl.when(i + 1 < N_WBLOCKS)
    def _():
        start_block(i + 1, 1 - slot)

    wall_ref[...] = stage[slot].astype(BF16)
    hb = _layer_norm(x_ref[...], g_ref[...], b_ref[...]).astype(BF16)
    hb_ref[...] = hb
    q_ref[...] = (_dot(hb, wq_bf[...]) * Q_SCALE).astype(BF16)
    k_ref[...] = _dot(hb, wkv_ref[:, :kw]).astype(BF16)
    v_ref[...] = _dot(hb, wkv_ref[:, kw:]).astype(BF16)


def _ln_qkv(x2d, g, b, win, wkv_dup, wba, wbc, wout):
    n = x2d.shape[0]
    tile = n // N_WBLOCKS
    assert tile * N_WBLOCKS == n and tile % 16 == 0
    kw = 2 * KV_WIDTH
    row = lambda i: (i, 0)
    const = lambda i: (0, 0)
    once = dict(pipeline_mode=pl.Buffered(1))
    any_spec = pl.BlockSpec(memory_space=pl.ANY)
    return pl.pallas_call(
        _qkv_kernel,
        grid=(N_WBLOCKS,),
        in_specs=[
            pl.BlockSpec((tile, D_MODEL), row),
            pl.BlockSpec((1, D_MODEL), const),
            pl.BlockSpec((1, D_MODEL), const),
            pl.BlockSpec((D_MODEL, D_MODEL), const, **once),
            pl.BlockSpec(wkv_dup.shape, const, **once),
            any_spec, any_spec, any_spec, any_spec,
        ],
        out_specs=[
            pl.BlockSpec((tile, D_MODEL), row),
            pl.BlockSpec((tile, D_MODEL), row),
            pl.BlockSpec((tile, kw), row),
            pl.BlockSpec((tile, kw), row),
            pl.BlockSpec((D_MODEL, CHUNK), lambda i: (0, i)),
        ],
        out_shape=[
            jax.ShapeDtypeStruct((n, D_MODEL), BF16),
            jax.ShapeDtypeStruct((n, D_MODEL), BF16),
            jax.ShapeDtypeStruct((n, kw), BF16),
            jax.ShapeDtypeStruct((n, kw), BF16),
            jax.ShapeDtypeStruct((D_MODEL, N_WBLOCKS * CHUNK), BF16),
        ],
        scratch_shapes=[
            pltpu.VMEM((D_MODEL, D_MODEL), BF16),
            pltpu.VMEM((2, D_MODEL, CHUNK), F32),
            pltpu.SemaphoreType.DMA((2,)),
        ],
        compiler_params=_params("arbitrary"),
        name="ln_qkv",
    )(x2d, g, b, win, wkv_dup, win, wba, wbc, wout)


PAIRS = N_Q_HEADS // N_KV_HEADS // 2
QROWS = PAIRS * BLOCK


class _AttnBlock:
    def __init__(self, sink_ref, q_ref, kp_ref, kc_ref, vp_ref, vc_ref, kvm_ref, has_prev):
        self.sink_ref, self.q_ref, self.kvm_ref, self.has_prev = sink_ref, q_ref, kvm_ref, has_prev
        self.k_refs, self.v_refs = (kp_ref, kc_ref), (vp_ref, vc_ref)
        self.lane = lax.broadcasted_iota(jnp.int32, (BLOCK, LANE), 1)
        qi = lax.broadcasted_iota(jnp.int32, (QROWS, BLOCK), 0) % BLOCK
        kj = lax.broadcasted_iota(jnp.int32, (QROWS, BLOCK), 1)
        self.from_prev = kj > qi
        self.is_meta = kj < N_META
        self.rowpair = lax.broadcasted_iota(jnp.int32, (QROWS, 1), 0) // BLOCK
        self.zpad = jnp.zeros((BLOCK - N_META, LANE), BF16)

    def _masked_rows(self, refs, meta_cols, g, half):
        keep = (self.lane >= HEAD_DIM) if half else (self.lane < HEAD_DIM)
        gs = slice(g * LANE, (g + 1) * LANE)
        blocks = [r[:, gs] for r in refs] + [jnp.concatenate([self.kvm_ref[:, meta_cols], self.zpad], axis=0)]
        return jnp.concatenate([jnp.where(keep, b, jnp.zeros((), BF16)) for b in blocks], axis=0)

    def scores(self, g, half):
        qs = jnp.concatenate([self.q_ref[:, (g * PAIRS + p) * LANE:(g * PAIRS + p + 1) * LANE]
                              for p in range(PAIRS)], axis=0)
        return _dot_nt(qs, self._masked_rows(self.k_refs, slice(g * LANE, (g + 1) * LANE), g, half))

    def probs(self, s, g, half):
        s_prev = jnp.where(self.has_prev, s[:, :BLOCK], NEG_INF)
        s_band = jnp.where(self.from_prev, s_prev, s[:, BLOCK:2 * BLOCK])
        s_meta = jnp.where(self.is_meta, s[:, 2 * BLOCK:], NEG_INF)
        sink = jnp.zeros((QROWS, 1), F32)
        for p in range(PAIRS):
            sink = jnp.where(self.rowpair == p, self.sink_ref[g * 2 * PAIRS + 2 * p + half], sink)
        m = jnp.maximum(jnp.max(jnp.maximum(s_band, s_meta), axis=-1, keepdims=True), sink)
        e_band = jnp.exp(s_band - m)
        e_meta = jnp.exp(s_meta - m)
        den = jnp.sum(e_band + e_meta, axis=-1, keepdims=True) + jnp.exp(sink - m)
        p_band = e_band / den
        return jnp.concatenate([jnp.where(self.from_prev, p_band, 0.0).astype(BF16),
                                jnp.where(self.from_prev, 0.0, p_band).astype(BF16),
                                (e_meta / den).astype(BF16)], axis=1)

    def values(self, probs, g, half):
        meta_cols = slice((N_KV_HEADS + g) * LANE, (N_KV_HEADS + g + 1) * LANE)
        return _dot(probs, self._masked_rows(self.v_refs, meta_cols, g, half))

    @staticmethod
    def store(o_ref, g, acc):
        for p in range(PAIRS):
            col = (g * PAIRS + p) * LANE
            o_ref[:, col:col + LANE] = acc[p * BLOCK:(p + 1) * BLOCK].astype(BF16)


TAIL_STEPS = 2 * N_CHUNKS + 1
HALF_ROWS = D_MODEL // 2


def _lane_concat(ref):
    return jnp.concatenate([ref[k] for k in range(ref.shape[0])], axis=1)


def _tail_kernel(sink_ref, x_ref, hb_ref, q_ref, kp_ref, kc_ref, vp_ref, vc_ref, kvm_ref,
                 g0_ref, b0_ref, um_ref, cw_ref, wa_ref, wb_ref, wc_ref, wd_ref,
                 g1_ref, b1_ref, rw_ref, rb_ref, wg_hbm, wu_hbm, h1_ref, lg_ref, wgu_ref,
                 uext_ref, carry_ref, z_ref, m_ref, attn_ref, stage_ref, wsem, *, tiles_per_seq):
    i = pl.program_id(0)
    s = pl.program_id(1)
    tm = ROW_TILE

    cast_step = i * (2 * N_CHUNKS) + s
    cast_slot = cast_step % 2

    def start_half(t, slot):
        mat, half = t // 2, t % 2
        rows = pl.ds(pl.multiple_of(half * HALF_ROWS, HALF_ROWS), HALF_ROWS)

        @pl.when(mat < N_EXPERTS)
        def _():
            pltpu.make_async_copy(wg_hbm.at[mat, rows], stage_ref.at[slot], wsem.at[slot]).start()

        @pl.when(mat >= N_EXPERTS)
        def _():
            pltpu.make_async_copy(wu_hbm.at[mat - N_EXPERTS, rows], stage_ref.at[slot], wsem.at[slot]).start()

    @pl.when((i == 0) & (s == 0))
    def _():
        carry_ref[...] = jnp.zeros_like(carry_ref)
        start_half(0, 0)

    @pl.when(s < 2 * N_CHUNKS)
    def _():
        pltpu.make_async_copy(wg_hbm.at[0, pl.ds(0, HALF_ROWS)], stage_ref.at[cast_slot], wsem.at[cast_slot]).wait()

        @pl.when(cast_step + 1 < pl.num_programs(0) * 2 * N_CHUNKS)
        def _():
            start_half(cast_step + 1, 1 - cast_slot)

    @pl.when(s < N_CHUNKS)
    def _():
        wgu_ref[0] = stage_ref[cast_slot].astype(BF16)
        hb = hb_ref[...]
        first = i % tiles_per_seq == 0
        attn_out = attn_ref.at[pl.ds(pl.multiple_of(s * BLOCK, BLOCK), BLOCK)]
        attn = _AttnBlock(sink_ref, q_ref, kp_ref, kc_ref, vp_ref, vc_ref, kvm_ref,
                          has_prev=(i % tiles_per_seq) * N_CHUNKS + s > 0)
        units = [(g, half) for g in range(N_KV_HEADS) for half in range(2)]
        half_cols = CHUNK // 2
        dots = [(w_ref, slice(h * half_cols, (h + 1) * half_cols))
                for h in range(2) for w_ref in (wa_ref, wb_ref, wc_ref)]

        def conv_half(cs, c_h, h_h, b_h):
            u = c_h * h_h
            uext_ref[0:8, cs] = jnp.where(first, um_ref[N_META - 8:N_META, cs], carry_ref[s, :, cs])
            uext_ref[8:tm + 8, cs] = u
            conv = (cw_ref[0:1, cs] * uext_ref[6:tm + 6, cs] + cw_ref[1:2, cs] * uext_ref[7:tm + 7, cs]
                    + cw_ref[2:3, cs] * u)
            carry_ref[s, :, cs] = u[tm - 8:tm, :]
            z_ref[s, :, cs] = (b_h * conv).astype(BF16)

        scores = attn.scores(*units[0])
        outs, acc = [], None
        for k, unit in enumerate(units):
            if k < len(dots):
                w_ref, cs = dots[k]
                outs.append(_dot(hb, w_ref[:, cs]))
            probs = attn.probs(scores, *unit)
            part = attn.values(probs, *unit)
            if k + 1 < len(units):
                scores = attn.scores(*units[k + 1])
            acc = part if unit[1] == 0 else acc + part
            if unit[1] == 1:
                attn.store(attn_out, unit[0], acc)
            if k % 3 == 2 and k < len(dots):
                conv_half(dots[k][1], *outs[k - 2:k + 1])

    @pl.when((s >= N_CHUNKS) & (s < 2 * N_CHUNKS))
    def _():
        wgu_ref[0] = stage_ref[cast_slot].astype(BF16)
        hb = hb_ref[...]
        conv_part = jax.nn.sigmoid(_dot(hb, wb_ref[...])) * _dot(_lane_concat(z_ref), wd_ref[...])
        gate_attn = jax.nn.sigmoid(_dot(hb, wa_ref[...]))
        m_ref[s - N_CHUNKS] = (gate_attn * _dot(attn_ref[...], wc_ref[...]) + conv_part).astype(BF16)

    @pl.when(s == 2 * N_CHUNKS)
    def _():
        h = _layer_norm(x_ref[...], g0_ref[...], b0_ref[...])
        m = _lane_concat(m_ref)
        for k, w_ref in enumerate((wa_ref, wb_ref, wc_ref, wd_ref)):
            cs = slice(k * CHUNK, (k + 1) * CHUNK)
            h1_ref[:, cs] = ALPHA * h[:, cs] + _dot(m, w_ref[...])
        h1 = _layer_norm(h1_ref[...], g1_ref[...], b1_ref[...])
        h1_ref[...] = h1
        lg_ref[...] = _dot_nt(rw_ref[...], h1.astype(BF16)) + rb_ref[...][:, 0:1]


def _mixer_tail(sinks, x2d, hb, q, kd, vd, kvm, g0, b0, um, cw, wall, g1, b1, rw, rb, wg, wu, seq):
    n = x2d.shape[0]
    nc = N_CHUNKS
    assert ROW_TILE == nc * BLOCK
    assert (n // ROW_TILE) * 2 * nc == 2 * 2 * N_EXPERTS
    cast_map = lambda i, s: ((i * 2 * nc + jnp.minimum(s, 2 * nc - 1)) // 2, (i * 2 * nc + jnp.minimum(s, 2 * nc - 1)) % 2, 0)
    any_spec = pl.BlockSpec(memory_space=pl.ANY)
    tiles_per_seq = seq // ROW_TILE
    kw = 2 * KV_WIDTH
    cur = lambda i, s: (i * nc + jnp.minimum(s, nc - 1), 0)
    prev = lambda i, s: (jnp.maximum(i * nc + jnp.minimum(s, nc - 1) - 1, (i // tiles_per_seq) * (seq // BLOCK)), 0)

    def slot(conv_first, merge_first, out_block, hold=False):
        def index(i, s):
            conv = conv_first if hold else conv_first + s
            return 0, jnp.where(s < nc, conv, jnp.where(s < 2 * nc, merge_first + s - nc, W_OUT + out_block))
        return pl.BlockSpec((D_MODEL, CHUNK), index)

    lo = lambda i, s: (0, jnp.minimum(s, nc - 1))
    const = lambda i, s: (0, 0)
    row = lambda i, s: (i, 0)
    return pl.pallas_call(
        functools.partial(_tail_kernel, tiles_per_seq=tiles_per_seq),
        grid=(n // ROW_TILE, TAIL_STEPS),
        in_specs=[
            pl.BlockSpec(memory_space=pltpu.SMEM),
            pl.BlockSpec((ROW_TILE, D_MODEL), row),
            pl.BlockSpec((ROW_TILE, D_MODEL), row),
            pl.BlockSpec((BLOCK, D_MODEL), cur),
            pl.BlockSpec((BLOCK, kw), prev),
            pl.BlockSpec((BLOCK, kw), cur),
            pl.BlockSpec((BLOCK, kw), prev),
            pl.BlockSpec((BLOCK, kw), cur),
            pl.BlockSpec((N_META, 2 * kw), const),
            pl.BlockSpec((1, D_MODEL), const),
            pl.BlockSpec((1, D_MODEL), const),
            pl.BlockSpec((N_META, CHUNK), lo),
            pl.BlockSpec((CONV_K, CHUNK), lo),
            slot(W_C, W_GA, 0),
            slot(W_H, W_GC, 1),
            slot(W_B, W_BA, 2),
            slot(W_BC, W_BC, 3, hold=True),
            pl.BlockSpec((1, D_MODEL), const),
            pl.BlockSpec((1, D_MODEL), const),
            pl.BlockSpec((ROUTE_ROWS, D_MODEL), const),
            pl.BlockSpec((ROUTE_ROWS, LANE), const),
            any_spec, any_spec,
        ],
        out_specs=[
            pl.BlockSpec((ROW_TILE, D_MODEL), row),
            pl.BlockSpec((ROUTE_ROWS, ROW_TILE), lambda i, s: (0, i)),
            pl.BlockSpec((1, HALF_ROWS, D_EXPERT), cast_map),
        ],
        out_shape=[
            jax.ShapeDtypeStruct((n, D_MODEL), F32),
            jax.ShapeDtypeStruct((ROUTE_ROWS, n), F32),
            jax.ShapeDtypeStruct((2 * N_EXPERTS, D_MODEL, D_EXPERT), BF16),
        ],
        scratch_shapes=[
            pltpu.VMEM((ROW_TILE + 8, CHUNK), F32),
            pltpu.VMEM((nc, 8, CHUNK), F32),
            pltpu.VMEM((nc, ROW_TILE, CHUNK), BF16),
            pltpu.VMEM((nc, ROW_TILE, CHUNK), BF16),
            pltpu.VMEM((ROW_TILE, D_MODEL), BF16),
            pltpu.VMEM((2, HALF_ROWS, D_EXPERT), F32),
            pltpu.SemaphoreType.DMA((2,)),
        ],
        compiler_params=_params("arbitrary", "arbitrary", vmem_limit=V7X_VMEM_LIMIT_TAIL),
        name="mixer_tail",
    )(sinks, x2d, hb, q, kd, kd, vd, vd, kvm, g0, b0, um, cw, wall, wall, wall, wall, g1, b1, rw, rb, wg, wu)


RCHUNK = 1024
PBLK = 256


def _route_kernel(lg_ref, e_ref, w_ref, pos_ref, cnt_ref, rank_ref):
    n = lg_ref.shape[1]
    epg = EXPERTS_PER_GROUP

    def route_chunk(c, carry):
        sl = pl.ds(pl.multiple_of(c * RCHUNK, RCHUNK), RCHUNK)
        gl = [lg_ref[gi:gi + 1, sl] for gi in range(N_GROUPS)]
        gmax = functools.reduce(jnp.maximum, gl)
        ge = [jnp.exp(v - gmax) for v in gl]
        gsum = functools.reduce(jnp.add, ge)
        gp = [v / gsum for v in ge]
        p_group = functools.reduce(jnp.maximum, gp)
        gsel = jnp.full_like(p_group, float(N_GROUPS - 1))
        for gi in range(N_GROUPS - 2, -1, -1):
            gsel = jnp.where(gp[gi] >= p_group, float(gi), gsel)
        e_in = lg_ref[EXPERT_ROW0 + (N_GROUPS - 1) * epg:EXPERT_ROW0 + N_GROUPS * epg, sl]
        for gi in range(N_GROUPS - 2, -1, -1):
            e_in = jnp.where(gsel == float(gi),
                             lg_ref[EXPERT_ROW0 + gi * epg:EXPERT_ROW0 + (gi + 1) * epg, sl], e_in)
        ee = jnp.exp(e_in - jnp.max(e_in, axis=0, keepdims=True))
        pe = ee / jnp.sum(ee, axis=0, keepdims=True)
        ridx = lax.broadcasted_iota(jnp.int32, pe.shape, 0).astype(F32)
        p0 = jnp.max(pe, axis=0, keepdims=True)
        i0 = jnp.min(jnp.where(pe >= p0, ridx, float(epg)), axis=0, keepdims=True)
        pe2 = jnp.where(ridx == i0, -1.0, pe)
        p1 = jnp.max(pe2, axis=0, keepdims=True)
        i1 = jnp.min(jnp.where(pe2 >= p1, ridx, float(epg)), axis=0, keepdims=True)
        den = p0 + p1
        e_ref[0:1, sl] = (gsel * epg + i0).astype(jnp.int32)
        e_ref[1:2, sl] = (gsel * epg + i1).astype(jnp.int32)
        w_ref[0:1, sl] = p0 / den * p_group
        w_ref[1:2, sl] = p1 / den * p_group
        return carry

    lax.fori_loop(0, n // RCHUNK, route_chunk, 0)

    eid = lax.broadcasted_iota(jnp.int32, (N_EXPERTS, PBLK), 0)
    before = jnp.where(lax.broadcasted_iota(jnp.int32, (PBLK, PBLK), 0)
                       < lax.broadcasted_iota(jnp.int32, (PBLK, PBLK), 1), 1.0, 0.0).astype(BF16)
    ones = jnp.ones((PBLK, LANE), BF16)
    nblk = n // PBLK

    def onehot(k, blk):
        sl = pl.ds(pl.multiple_of(blk * PBLK, PBLK), PBLK)
        return sl, eid == e_ref[pl.ds(k, 1), sl]

    run = jnp.zeros((N_EXPERTS, LANE), F32)
    for k in range(2):
        def count_block(blk, run, k=k):
            sl, hit = onehot(k, blk)
            hit_bf = jnp.where(hit, 1.0, 0.0).astype(BF16)
            prior = _dot(hit_bf, before) + jnp.concatenate([run] * (PBLK // LANE), axis=1)
            rank_ref[pl.ds(k, 1), sl] = jnp.sum(jnp.where(hit, prior, 0.0), axis=0, keepdims=True)
            return run + _dot(hit_bf, ones)
        run = lax.fori_loop(0, nblk, count_block, run)

    cnt_ref[...] = run.astype(jnp.int32)
    hi = jnp.floor(run * (1.0 / LANE))
    lo = run - hi * LANE
    below = jnp.where(lax.broadcasted_iota(jnp.int32, (N_EXPERTS, N_EXPERTS), 1)
                      < lax.broadcasted_iota(jnp.int32, (N_EXPERTS, N_EXPERTS), 0), 1.0, 0.0).astype(BF16)
    off = _dot(below, hi.astype(BF16)) * LANE + _dot(below, lo.astype(BF16))
    off = jnp.concatenate([off] * (PBLK // LANE), axis=1)

    for k in range(2):
        def place_block(blk, carry, k=k):
            sl, hit = onehot(k, blk)
            base = jnp.sum(jnp.where(hit, off, 0.0), axis=0, keepdims=True)
            pos_ref[pl.ds(k, 1), sl] = (base + rank_ref[pl.ds(k, 1), sl]).astype(jnp.int32)
            return carry
        lax.fori_loop(0, nblk, place_block, 0)


def _route(logits_t):
    n = logits_t.shape[1]
    return pl.pallas_call(
        _route_kernel,
        out_shape=[
            jax.ShapeDtypeStruct((2, n), jnp.int32),
            jax.ShapeDtypeStruct((2, n), F32),
            jax.ShapeDtypeStruct((2, n), jnp.int32),
            jax.ShapeDtypeStruct((N_EXPERTS, LANE), jnp.int32),
        ],
        scratch_shapes=[pltpu.VMEM((2, n), F32)],
        compiler_params=pltpu.CompilerParams(vmem_limit_bytes=V7X_VMEM_LIMIT),
        name="route",
    )(logits_t)


def _row_copy(src_ref, src_row, dst_ref, dst_row, sem):
    return pltpu.make_async_copy(src_ref.at[pl.ds(src_row, 1)], dst_ref.at[pl.ds(dst_row, 1)], sem)


def _dispatch_kernel(pos_ref, h_ref, xs_ref, sem):
    for r in range(TOK_TILE):
        for k in range(2):
            _row_copy(h_ref, r, xs_ref, pos_ref[0, k, r], sem).start(priority=k)
    for k in range(2):
        pltpu.make_async_copy(h_ref, xs_ref.at[pl.ds(0, TOK_TILE)], sem).wait()


def _dispatch(pos_tiles, h1):
    n = h1.shape[0]
    return pl.pallas_call(
        _dispatch_kernel,
        grid=(n // TOK_TILE,),
        in_specs=[
            pl.BlockSpec((1, 2, TOK_TILE), lambda i: (i, 0, 0), memory_space=pltpu.SMEM),
            pl.BlockSpec((TOK_TILE, D_MODEL), lambda i: (i, 0)),
        ],
        out_specs=pl.BlockSpec(memory_space=pl.ANY),
        out_shape=jax.ShapeDtypeStruct((2 * n, D_MODEL), F32),
        scratch_shapes=[pltpu.SemaphoreType.DMA],
        compiler_params=_params("arbitrary"),
        name="dispatch",
    )(pos_tiles, h1)


W_SLOTS = 2
W_LOOKAHEAD = W_SLOTS - 1
def _moe_kernel(tile_ref, exp_ref, lo_ref, first_ref, slot_ref, ahead_ref, head_ref, n_ref,
                xs_ref, wgu_hbm, wd_hbm, y_ref,
                wg_buf, wu_buf, wd_buf, ytile_ref, sem):
    i = pl.program_id(0)

    def weight_copies(expert, slot):
        return (pltpu.make_async_copy(wgu_hbm.at[expert], wg_buf.at[slot], sem.at[slot, 0]),
                pltpu.make_async_copy(wgu_hbm.at[N_EXPERTS + expert], wu_buf.at[slot], sem.at[slot, 1]),
                pltpu.make_async_copy(wd_hbm.at[expert], wd_buf.at[slot], sem.at[slot, 2]))

    @pl.when(i == 0)
    def _():
        ytile_ref[...] = jnp.zeros_like(ytile_ref)
        for k in range(W_LOOKAHEAD):
            @pl.when(head_ref[k] >= 0)
            def _(k=k):
                for copy in weight_copies(head_ref[k], k):
                    copy.start()

    valid = i < n_ref[0]

    @pl.when(valid & (first_ref[i] == 1))
    def _():
        slot = slot_ref[i]
        for copy in weight_copies(exp_ref[i], slot):
            copy.wait()

        @pl.when(ahead_ref[i] >= 0)
        def _():
            for copy in weight_copies(ahead_ref[i], (slot + W_LOOKAHEAD) % W_SLOTS):
                copy.start()

    @pl.when(valid)
    def _():
        slot = slot_ref[i]
        x = xs_ref[...].astype(BF16)
        g = _dot(x, wg_buf[slot])
        u = _dot(x, wu_buf[slot])
        a = (g * jax.nn.sigmoid(g) * u).astype(BF16)
        y = _dot(a, wd_buf[slot].astype(BF16))
        row = lax.broadcasted_iota(jnp.int32, (MOE_TILE, 1), 0)
        merged = jnp.where(row >= lo_ref[i], y, ytile_ref[...])
        ytile_ref[...] = merged
        y_ref[...] = merged


def _moe_experts(items, xs, wgu, wd):
    rows = xs.shape[0]
    tile_map = lambda i, t, *_: (t[i], 0)
    grid_spec = pltpu.PrefetchScalarGridSpec(
        num_scalar_prefetch=len(items),
        grid=(MAX_ITEMS,),
        in_specs=[
            pl.BlockSpec((MOE_TILE, D_MODEL), tile_map),
            pl.BlockSpec(memory_space=pl.ANY),
            pl.BlockSpec(memory_space=pl.ANY),
        ],
        out_specs=pl.BlockSpec((MOE_TILE, D_MODEL), tile_map),
        scratch_shapes=[
            pltpu.VMEM((W_SLOTS, D_MODEL, D_EXPERT), BF16),
            pltpu.VMEM((W_SLOTS, D_MODEL, D_EXPERT), BF16),
            pltpu.VMEM((W_SLOTS, D_EXPERT, D_MODEL), F32),
            pltpu.VMEM((MOE_TILE, D_MODEL), F32),
            pltpu.SemaphoreType.DMA((W_SLOTS, 3)),
        ],
    )
    return pl.pallas_call(
        _moe_kernel,
        grid_spec=grid_spec,
        out_shape=jax.ShapeDtypeStruct((rows, D_MODEL), F32),
        compiler_params=_params("arbitrary"),
        name="moe_experts",
    )(*items, xs, wgu, wd)


def _work_items(counts):
    i32 = jnp.int32
    off = jnp.cumsum(counts) - counts
    end = off + counts
    first_tile = off // MOE_TILE
    n_e = jnp.where(counts > 0, (end - 1) // MOE_TILE - first_tile + 1, 0)
    item_end = jnp.cumsum(n_e)
    n_items = item_end[-1]
    idx = jnp.minimum(jnp.arange(MAX_ITEMS, dtype=i32), n_items - 1)
    exp = jnp.sum((item_end[None, :] <= idx[:, None]).astype(i32), axis=1)
    item0 = item_end[exp] - n_e[exp]
    tile = first_tile[exp] + idx - item0
    lo = jnp.maximum(off[exp] - tile * MOE_TILE, 0)
    present = counts > 0
    ordinal = jnp.cumsum(present.astype(i32)) - 1
    eid = jnp.arange(N_EXPERTS, dtype=i32)
    is_kth = present[None, :] & (ordinal[None, :] == eid[:, None])
    kth = jnp.where(jnp.any(is_kth, axis=1), jnp.sum(jnp.where(is_kth, eid[None, :], 0), axis=1), -1)
    kth = jnp.concatenate([kth, jnp.full((W_LOOKAHEAD,), -1, i32)])
    items = (tile, exp, lo, idx == item0, ordinal[exp] % W_SLOTS, kth[ordinal[exp] + W_LOOKAHEAD],
             kth[:W_LOOKAHEAD], n_items.reshape(1))
    return tuple(v.astype(i32) for v in items)


def _combine_kernel(pos_ref, pos_next_ref, h_ref, wt_ref, g_ref, b_ref, y_ref, o_ref, buf_ref, sem):
    i = pl.program_id(0)
    slot = i % 2

    def gather(tile_pos_ref, into):
        for r in range(TOK_TILE):
            for k in range(2):
                _row_copy(y_ref, tile_pos_ref[0, k, r], buf_ref.at[into, k], r, sem.at[into]).start(priority=k)

    @pl.when(i == 0)
    def _():
        gather(pos_ref, 0)

    for other in range(2):
        @pl.when((i + 1 < pl.num_programs(0)) & (slot == 1 - other))
        def _(other=other):
            gather(pos_next_ref, other)

    for k in range(2):
        pltpu.make_async_copy(y_ref.at[pl.ds(0, TOK_TILE)], buf_ref.at[slot, k], sem.at[slot]).wait()
    f = wt_ref[:, 0:1] * buf_ref[slot, 0] + wt_ref[:, 1:2] * buf_ref[slot, 1]
    o_ref[...] = _layer_norm(ALPHA * h_ref[...] + f, g_ref[...], b_ref[...])


def _combine_ln(pos_tiles, h1, wt, g, b, y):
    n = h1.shape[0]
    steps = n // TOK_TILE
    return pl.pallas_call(
        _combine_kernel,
        grid=(steps,),
        in_specs=[
            pl.BlockSpec((1, 2, TOK_TILE), lambda i: (i, 0, 0), memory_space=pltpu.SMEM),
            pl.BlockSpec((1, 2, TOK_TILE), lambda i: (jnp.minimum(i + 1, steps - 1), 0, 0),
                         memory_space=pltpu.SMEM),
            pl.BlockSpec((TOK_TILE, D_MODEL), lambda i: (i, 0)),
            pl.BlockSpec((TOK_TILE, 2), lambda i: (i, 0)),
            pl.BlockSpec((1, D_MODEL), lambda i: (0, 0)),
            pl.BlockSpec((1, D_MODEL), lambda i: (0, 0)),
            pl.BlockSpec(memory_space=pl.ANY),
        ],
        out_specs=pl.BlockSpec((TOK_TILE, D_MODEL), lambda i: (i, 0)),
        out_shape=jax.ShapeDtypeStruct((n, D_MODEL), F32),
        scratch_shapes=[pltpu.VMEM((2, 2, TOK_TILE, D_MODEL), F32), pltpu.SemaphoreType.DMA((2,))],
        compiler_params=_params("arbitrary"),
        name="combine_ln",
    )(pos_tiles, pos_tiles, h1, wt, g, b, y)


def _dup_heads(w):
    d = w.shape[0]
    w = w.reshape(d, N_KV_HEADS, 1, HEAD_DIM)
    return jnp.broadcast_to(w, (d, N_KV_HEADS, 2, HEAD_DIM)).reshape(d, 2 * KV_WIDTH)


def kernel(x, meta_tokens, ln_in_g, ln_in_b, w_in, sinks, conv_w, w_branch_attn, w_branch_conv, w_out,
           ln1_g, ln1_b, router_group_w, router_group_b, router_expert_w, router_expert_b,
           w_gate, w_up, w_down, ln2_g, ln2_b):
    batch, seq, d = x.shape
    n = batch * seq
    assert d == D_MODEL and seq % ROW_TILE == 0 and w_in.shape[0] == DEPTH == 1
    row = lambda v: v.reshape(1, -1).astype(F32)
    x2d = x.reshape(n, d)
    g0, b0 = row(ln_in_g), row(ln_in_b)

    win = w_in[0]
    kcol, vcol, ccol = D_MODEL, D_MODEL + KV_WIDTH, D_MODEL + 2 * KV_WIDTH
    wkv_dup = jnp.concatenate([_dup_heads(win[:, kcol:vcol]), _dup_heads(win[:, vcol:ccol])], axis=1).astype(BF16)
    rw = jnp.zeros((ROUTE_ROWS, d), F32)
    rw = rw.at[:N_GROUPS].set(router_group_w[0].T).at[EXPERT_ROW0:EXPERT_ROW0 + N_EXPERTS].set(router_expert_w[0].T)
    rb = jnp.zeros((ROUTE_ROWS,), F32)
    rb = rb.at[:N_GROUPS].set(router_group_b[0]).at[EXPERT_ROW0:EXPERT_ROW0 + N_EXPERTS].set(router_expert_b[0])
    rb = jnp.broadcast_to(rb[:, None], (ROUTE_ROWS, LANE))

    hb, q, kd, vd, wall = _ln_qkv(x2d, g0, b0, win, wkv_dup, w_branch_attn[0], w_branch_conv[0], w_out[0])
    kvm, um = _meta_prep(meta_tokens.astype(F32), g0, b0, wkv_dup, wall)
    h1, logits_t, wgu = _mixer_tail(sinks[0].astype(F32), x2d, hb, q, kd, vd, kvm, g0, b0, um, conv_w[0], wall,
                                    row(ln1_g[0]), row(ln1_b[0]), rw.astype(BF16), rb, w_gate[0], w_up[0], seq)
    _, top_w, pos, counts = _route(logits_t)
    pos_tiles = pos.reshape(2, n // TOK_TILE, TOK_TILE).transpose(1, 0, 2)
    xs = _dispatch(pos_tiles, h1)
    y = _moe_experts(_work_items(counts[:, 0]), xs, wgu, w_down[0])
    out = _combine_ln(pos_tiles, h1, top_w.T, row(ln2_g[0]), row(ln2_b[0]), y)
    return out.reshape(batch, seq, d)
```

```python
import functools

import jax
import jax.numpy as jnp
from jax import lax
from jax.experimental import pallas as pl
from jax.experimental.pallas import tpu as pltpu

D_MODEL = 2048
N_META = 16
HEAD_DIM = 64
N_Q_HEADS = 32
N_KV_HEADS = 4
BLOCK = 128
KV_WIDTH = N_KV_HEADS * HEAD_DIM
CONV_K = 3
N_GROUPS = 4
EXPERTS_PER_GROUP = 8
N_EXPERTS = N_GROUPS * EXPERTS_PER_GROUP
D_EXPERT = D_MODEL // 4
LN_EPS = 1e-5
DEPTH = 1
ALPHA = (2.0 * DEPTH) ** 0.25
NEG_INF = -1e30
Q_SCALE = HEAD_DIM ** -0.5

V7X_VMEM_LIMIT = 56 * 1024 * 1024
V7X_VMEM_LIMIT_TAIL = 58 * 1024 * 1024
LANE = 128
CHUNK = 512
N_CHUNKS = D_MODEL // CHUNK
ROW_TILE = 512
MOE_TILE = 256
TOK_TILE = 512
MAX_ITEMS = 2 * 8192 // MOE_TILE + N_EXPERTS
ROUTE_ROWS = 128
EXPERT_ROW0 = 8

BF16 = jnp.bfloat16
F32 = jnp.float32


def _layer_norm(x, g, b):
    mu = jnp.mean(x, axis=-1, keepdims=True)
    xc = x - mu
    var = jnp.mean(xc * xc, axis=-1, keepdims=True)
    return xc * lax.rsqrt(var + LN_EPS) * g + b


def _dot(a, b):
    return jnp.dot(a, b, preferred_element_type=F32)


def _dot_nt(a, b):
    return lax.dot_general(a, b, (((1,), (1,)), ((), ())), preferred_element_type=F32)


def _params(*sem, vmem_limit=V7X_VMEM_LIMIT):
    return pltpu.CompilerParams(dimension_semantics=sem, vmem_limit_bytes=vmem_limit)


W_B, W_C, W_H, W_GA, W_GC, W_BA, W_BC, W_OUT = (k * N_CHUNKS for k in range(8))
N_WBLOCKS = 8 * N_CHUNKS
IN_FIRST_MIX = (D_MODEL + 2 * KV_WIDTH) // CHUNK


def _meta_kernel(meta_ref, g_ref, b_ref, wkv_ref, wcc_ref, wch_ref, kv_ref, u_ref):
    hm = _layer_norm(meta_ref[...], g_ref[...], b_ref[...]).astype(BF16)
    kv_ref[...] = _dot(hm, wkv_ref[...]).astype(BF16)
    u_ref[...] = _dot(hm, wcc_ref[...]) * _dot(hm, wch_ref[...])


def _meta_prep(meta, g, b, wkv_dup, wall):
    n_kv = wkv_dup.shape[1] // N_CHUNKS
    return pl.pallas_call(
        _meta_kernel,
        grid=(N_CHUNKS,),
        in_specs=[
            pl.BlockSpec((N_META, D_MODEL), lambda j: (0, 0)),
            pl.BlockSpec((1, D_MODEL), lambda j: (0, 0)),
            pl.BlockSpec((1, D_MODEL), lambda j: (0, 0)),
            pl.BlockSpec((D_MODEL, n_kv), lambda j: (0, j)),
            pl.BlockSpec((D_MODEL, CHUNK), lambda j: (0, W_C + j)),
            pl.BlockSpec((D_MODEL, CHUNK), lambda j: (0, W_H + j)),
        ],
        out_specs=[
            pl.BlockSpec((N_META, n_kv), lambda j: (0, j)),
            pl.BlockSpec((N_META, CHUNK), lambda j: (0, j)),
        ],
        out_shape=[
            jax.ShapeDtypeStruct((N_META, wkv_dup.shape[1]), BF16),
            jax.ShapeDtypeStruct((N_META, D_MODEL), F32),
        ],
        compiler_params=_params("arbitrary"),
        name="meta_prep",
    )(meta, g, b, wkv_dup, wall, wall)


def _qkv_kernel(x_ref, g_ref, b_ref, wq_ref, wkv_ref, win_hbm, ba_hbm, bc_hbm, out_hbm,
                hb_ref, q_ref, k_ref, v_ref, wall_ref, wq_bf, stage, sem):
    i = pl.program_id(0)
    kw = 2 * KV_WIDTH

    def start_block(j, slot):
        def copy(src_hbm, block):
            col = pl.multiple_of(block * CHUNK, CHUNK)
            pltpu.make_async_copy(src_hbm.at[:, pl.ds(col, CHUNK)], stage.at[slot], sem.at[slot]).start()

        @pl.when(j < W_BA)
        def _():
            copy(win_hbm, IN_FIRST_MIX + j)

        for src_hbm, first in ((ba_hbm, W_BA), (bc_hbm, W_BC), (out_hbm, W_OUT)):
            @pl.when((j >= first) & (j < first + N_CHUNKS))
            def _(src_hbm=src_hbm, first=first):
                copy(src_hbm, j - first)

    @pl.when(i == 0)
    def _():
        start_block(0, 0)
        wq_bf[...] = wq_ref[...].astype(BF16)

    slot = i % 2
    pltpu.make_async_copy(ba_hbm.at[:, pl.ds(0, CHUNK)], stage.at[slot], sem.at[slot]).wait()

    @pl.when(i + 1 < N_WBLOCKS)
    def _():
        start_block(i + 1, 1 - slot)

    wall_ref[...] = stage[slot].astype(BF16)
    hb = _layer_norm(x_ref[...], g_ref[...], b_ref[...]).astype(BF16)
    hb_ref[...] = hb
    q_ref[...] = (_dot(hb, wq_bf[...]) * Q_SCALE).astype(BF16)
    k_ref[...] = _dot(hb, wkv_ref[:, :kw]).astype(BF16)
    v_ref[...] = _dot(hb, wkv_ref[:, kw:]).astype(BF16)


def _ln_qkv(x2d, g, b, win, wkv_dup, wba, wbc, wout):
    n = x2d.shape[0]
    tile = n // N_WBLOCKS
    assert tile * N_WBLOCKS == n and tile % 16 == 0
    kw = 2 * KV_WIDTH
    row = lambda i: (i, 0)
    const = lambda i: (0, 0)
    once = dict(pipeline_mode=pl.Buffered(1))
    any_spec = pl.BlockSpec(memory_space=pl.ANY)
    return pl.pallas_call(
        _qkv_kernel,
        grid=(N_WBLOCKS,),
        in_specs=[
            pl.BlockSpec((tile, D_MODEL), row),
            pl.BlockSpec((1, D_MODEL), const),
            pl.BlockSpec((1, D_MODEL), const),
            pl.BlockSpec((D_MODEL, D_MODEL), const, **once),
            pl.BlockSpec(wkv_dup.shape, const, **once),
            any_spec, any_spec, any_spec, any_spec,
        ],
        out_specs=[
            pl.BlockSpec((tile, D_MODEL), row),
            pl.BlockSpec((tile, D_MODEL), row),
            pl.BlockSpec((tile, kw), row),
            pl.BlockSpec((tile, kw), row),
            pl.BlockSpec((D_MODEL, CHUNK), lambda i: (0, i)),
        ],
        out_shape=[
            jax.ShapeDtypeStruct((n, D_MODEL), BF16),
            jax.ShapeDtypeStruct((n, D_MODEL), BF16),
            jax.ShapeDtypeStruct((n, kw), BF16),
            jax.ShapeDtypeStruct((n, kw), BF16),
            jax.ShapeDtypeStruct((D_MODEL, N_WBLOCKS * CHUNK), BF16),
        ],
        scratch_shapes=[
            pltpu.VMEM((D_MODEL, D_MODEL), BF16),
            pltpu.VMEM((2, D_MODEL, CHUNK), F32),
            pltpu.SemaphoreType.DMA((2,)),
        ],
        compiler_params=_params("arbitrary"),
        name="ln_qkv",
    )(x2d, g, b, win, wkv_dup, win, wba, wbc, wout)


PAIRS = N_Q_HEADS // N_KV_HEADS // 2
QROWS = PAIRS * BLOCK


class _AttnBlock:
    def __init__(self, sink_ref, q_ref, kp_ref, kc_ref, vp_ref, vc_ref, kvm_ref, has_prev):
        self.sink_ref, self.q_ref, self.kvm_ref, self.has_prev = sink_ref, q_ref, kvm_ref, has_prev
        self.k_refs, self.v_refs = (kp_ref, kc_ref), (vp_ref, vc_ref)
        self.lane = lax.broadcasted_iota(jnp.int32, (BLOCK, LANE), 1)
        qi = lax.broadcasted_iota(jnp.int32, (QROWS, BLOCK), 0) % BLOCK
        kj = lax.broadcasted_iota(jnp.int32, (QROWS, BLOCK), 1)
        self.from_prev = kj > qi
        self.is_meta = kj < N_META
        self.rowpair = lax.broadcasted_iota(jnp.int32, (QROWS, 1), 0) // BLOCK
        self.zpad = jnp.zeros((BLOCK - N_META, LANE), BF16)

    def _masked_rows(self, refs, meta_cols, g, half):
        keep = (self.lane >= HEAD_DIM) if half else (self.lane < HEAD_DIM)
        gs = slice(g * LANE, (g + 1) * LANE)
        blocks = [r[:, gs] for r in refs] + [jnp.concatenate([self.kvm_ref[:, meta_cols], self.zpad], axis=0)]
        return jnp.concatenate([jnp.where(keep, b, jnp.zeros((), BF16)) for b in blocks], axis=0)

    def scores(self, g, half):
        qs = jnp.concatenate([self.q_ref[:, (g * PAIRS + p) * LANE:(g * PAIRS + p + 1) * LANE]
                              for p in range(PAIRS)], axis=0)
        return _dot_nt(qs, self._masked_rows(self.k_refs, slice(g * LANE, (g + 1) * LANE), g, half))

    def probs(self, s, g, half):
        s_prev = jnp.where(self.has_prev, s[:, :BLOCK], NEG_INF)
        s_band = jnp.where(self.from_prev, s_prev, s[:, BLOCK:2 * BLOCK])
        s_meta = jnp.where(self.is_meta, s[:, 2 * BLOCK:], NEG_INF)
        sink = jnp.zeros((QROWS, 1), F32)
        for p in range(PAIRS):
            sink = jnp.where(self.rowpair == p, self.sink_ref[g * 2 * PAIRS + 2 * p + half], sink)
        m = jnp.maximum(jnp.max(jnp.maximum(s_band, s_meta), axis=-1, keepdims=True), sink)
        e_band = jnp.exp(s_band - m)
        e_meta = jnp.exp(s_meta - m)
        den = jnp.sum(e_band + e_meta, axis=-1, keepdims=True) + jnp.exp(sink - m)
        p_band = e_band / den
        return jnp.concatenate([jnp.where(self.from_prev, p_band, 0.0).astype(BF16),
                                jnp.where(self.from_prev, 0.0, p_band).astype(BF16),
                                (e_meta / den).astype(BF16)], axis=1)

    def values(self, probs, g, half):
        meta_cols = slice((N_KV_HEADS + g) * LANE, (N_KV_HEADS + g + 1) * LANE)
        return _dot(probs, self._masked_rows(self.v_refs, meta_cols, g, half))

    @staticmethod
    def store(o_ref, g, acc):
        for p in range(PAIRS):
            col = (g * PAIRS + p) * LANE
            o_ref[:, col:col + LANE] = acc[p * BLOCK:(p + 1) * BLOCK].astype(BF16)


TAIL_STEPS = 2 * N_CHUNKS + 1
HALF_ROWS = D_MODEL // 2


def _lane_concat(ref):
    return jnp.concatenate([ref[k] for k in range(ref.shape[0])], axis=1)


def _tail_kernel(sink_ref, x_ref, hb_ref, q_ref, kp_ref, kc_ref, vp_ref, vc_ref, kvm_ref,
                 g0_ref, b0_ref, um_ref, cw_ref, wa_ref, wb_ref, wc_ref, wd_ref,
                 g1_ref, b1_ref, rw_ref, rb_ref, wg_hbm, wu_hbm, h1_ref, lg_ref, wgu_ref,
                 uext_ref, carry_ref, z_ref, m_ref, attn_ref, stage_ref, wsem, *, tiles_per_seq):
    i = pl.program_id(0)
    s = pl.program_id(1)
    tm = ROW_TILE

    cast_step = i * (2 * N_CHUNKS) + s
    cast_slot = cast_step % 2

    def start_half(t, slot):
        mat, half = t // 2, t % 2
        rows = pl.ds(pl.multiple_of(half * HALF_ROWS, HALF_ROWS), HALF_ROWS)

        @pl.when(mat < N_EXPERTS)
        def _():
            pltpu.make_async_copy(wg_hbm.at[mat, rows], stage_ref.at[slot], wsem.at[slot]).start()

        @pl.when(mat >= N_EXPERTS)
        def _():
            pltpu.make_async_copy(wu_hbm.at[mat - N_EXPERTS, rows], stage_ref.at[slot], wsem.at[slot]).start()

    @pl.when((i == 0) & (s == 0))
    def _():
        carry_ref[...] = jnp.zeros_like(carry_ref)
        start_half(0, 0)

    @pl.when(s < 2 * N_CHUNKS)
    def _():
        pltpu.make_async_copy(wg_hbm.at[0, pl.ds(0, HALF_ROWS)], stage_ref.at[cast_slot], wsem.at[cast_slot]).wait()

        @pl.when(cast_step + 1 < pl.num_programs(0) * 2 * N_CHUNKS)
        def _():
            start_half(cast_step + 1, 1 - cast_slot)

    @pl.when(s < N_CHUNKS)
    def _():
        wgu_ref[0] = stage_ref[cast_slot].astype(BF16)
        hb = hb_ref[...]
        first = i % tiles_per_seq == 0
        attn_out = attn_ref.at[pl.ds(pl.multiple_of(s * BLOCK, BLOCK), BLOCK)]
        attn = _AttnBlock(sink_ref, q_ref, kp_ref, kc_ref, vp_ref, vc_ref, kvm_ref,
                          has_prev=(i % tiles_per_seq) * N_CHUNKS + s > 0)
        units = [(g, half) for g in range(N_KV_HEADS) for half in range(2)]
        half_cols = CHUNK // 2
        dots = [(w_ref, slice(h * half_cols, (h + 1) * half_cols))
                for h in range(2) for w_ref in (wa_ref, wb_ref, wc_ref)]

        def conv_half(cs, c_h, h_h, b_h):
            u = c_h * h_h
            uext_ref[0:8, cs] = jnp.where(first, um_ref[N_META - 8:N_META, cs], carry_ref[s, :, cs])
            uext_ref[8:tm + 8, cs] = u
            conv = (cw_ref[0:1, cs] * uext_ref[6:tm + 6, cs] + cw_ref[1:2, cs] * uext_ref[7:tm + 7, cs]
                    + cw_ref[2:3, cs] * u)
            carry_ref[s, :, cs] = u[tm - 8:tm, :]
            z_ref[s, :, cs] = (b_h * conv).astype(BF16)

        scores = attn.scores(*units[0])
        outs, acc = [], None
        for k, unit in enumerate(units):
            if k < len(dots):
                w_ref, cs = dots[k]
                outs.append(_dot(hb, w_ref[:, cs]))
            probs = attn.probs(scores, *unit)
            part = attn.values(probs, *unit)
            if k + 1 < len(units):
                scores = attn.scores(*units[k + 1])
            acc = part if unit[1] == 0 else acc + part
            if unit[1] == 1:
                attn.store(attn_out, unit[0], acc)
            if k % 3 == 2 and k < len(dots):
                conv_half(dots[k][1], *outs[k - 2:k + 1])

    @pl.when((s >= N_CHUNKS) & (s < 2 * N_CHUNKS))
    def _():
        wgu_ref[0] = stage_ref[cast_slot].astype(BF16)
        hb = hb_ref[...]
        conv_part = jax.nn.sigmoid(_dot(hb, wb_ref[...])) * _dot(_lane_concat(z_ref), wd_ref[...])
        gate_attn = jax.nn.sigmoid(_dot(hb, wa_ref[...]))
        m_ref[s - N_CHUNKS] = (gate_attn * _dot(attn_ref[...], wc_ref[...]) + conv_part).astype(BF16)

    @pl.when(s == 2 * N_CHUNKS)
    def _():
        h = _layer_norm(x_ref[...], g0_ref[...], b0_ref[...])
        m = _lane_concat(m_ref)
        for k, w_ref in enumerate((wa_ref, wb_ref, wc_ref, wd_ref)):
            cs = slice(k * CHUNK, (k + 1) * CHUNK)
            h1_ref[:, cs] = ALPHA * h[:, cs] + _dot(m, w_ref[...])
        h1 = _layer_norm(h1_ref[...], g1_ref[...], b1_ref[...])
        h1_ref[...] = h1
        lg_ref[...] = _dot_nt(rw_ref[...], h1.astype(BF16)) + rb_ref[...][:, 0:1]


def _mixer_tail(sinks, x2d, hb, q, kd, vd, kvm, g0, b0, um, cw, wall, g1, b1, rw, rb, wg, wu, seq):
    n = x2d.shape[0]
    nc = N_CHUNKS
    assert ROW_TILE == nc * BLOCK
    assert (n // ROW_TILE) * 2 * nc == 2 * 2 * N_EXPERTS
    cast_map = lambda i, s: ((i * 2 * nc + jnp.minimum(s, 2 * nc - 1)) // 2, (i * 2 * nc + jnp.minimum(s, 2 * nc - 1)) % 2, 0)
    any_spec = pl.BlockSpec(memory_space=pl.ANY)
    tiles_per_seq = seq // ROW_TILE
    kw = 2 * KV_WIDTH
    cur = lambda i, s: (i * nc + jnp.minimum(s, nc - 1), 0)
    prev = lambda i, s: (jnp.maximum(i * nc + jnp.minimum(s, nc - 1) - 1, (i // tiles_per_seq) * (seq // BLOCK)), 0)

    def slot(conv_first, merge_first, out_block, hold=False):
        def index(i, s):
            conv = conv_first if hold else conv_first + s
            return 0, jnp.where(s < nc, conv, jnp.where(s < 2 * nc, merge_first + s - nc, W_OUT + out_block))
        return pl.BlockSpec((D_MODEL, CHUNK), index)

    lo = lambda i, s: (0, jnp.minimum(s, nc - 1))
    const = lambda i, s: (0, 0)
    row = lambda i, s: (i, 0)
    return pl.pallas_call(
        functools.partial(_tail_kernel, tiles_per_seq=tiles_per_seq),
        grid=(n // ROW_TILE, TAIL_STEPS),
        in_specs=[
            pl.BlockSpec(memory_space=pltpu.SMEM),
            pl.BlockSpec((ROW_TILE, D_MODEL), row),
            pl.BlockSpec((ROW_TILE, D_MODEL), row),
            pl.BlockSpec((BLOCK, D_MODEL), cur),
            pl.BlockSpec((BLOCK, kw), prev),
            pl.BlockSpec((BLOCK, kw), cur),
            pl.BlockSpec((BLOCK, kw), prev),
            pl.BlockSpec((BLOCK, kw), cur),
            pl.BlockSpec((N_META, 2 * kw), const),
            pl.BlockSpec((1, D_MODEL), const),
            pl.BlockSpec((1, D_MODEL), const),
            pl.BlockSpec((N_META, CHUNK), lo),
            pl.BlockSpec((CONV_K, CHUNK), lo),
            slot(W_C, W_GA, 0),
            slot(W_H, W_GC, 1),
            slot(W_B, W_BA, 2),
            slot(W_BC, W_BC, 3, hold=True),
            pl.BlockSpec((1, D_MODEL), const),
            pl.BlockSpec((1, D_MODEL), const),
            pl.BlockSpec((ROUTE_ROWS, D_MODEL), const),
            pl.BlockSpec((ROUTE_ROWS, LANE), const),
            any_spec, any_spec,
        ],
        out_specs=[
            pl.BlockSpec((ROW_TILE, D_MODEL), row),
            pl.BlockSpec((ROUTE_ROWS, ROW_TILE), lambda i, s: (0, i)),
            pl.BlockSpec((1, HALF_ROWS, D_EXPERT), cast_map),
        ],
        out_shape=[
            jax.ShapeDtypeStruct((n, D_MODEL), F32),
            jax.ShapeDtypeStruct((ROUTE_ROWS, n), F32),
            jax.ShapeDtypeStruct((2 * N_EXPERTS, D_MODEL, D_EXPERT), BF16),
        ],
        scratch_shapes=[
            pltpu.VMEM((ROW_TILE + 8, CHUNK), F32),
            pltpu.VMEM((nc, 8, CHUNK), F32),
            pltpu.VMEM((nc, ROW_TILE, CHUNK), BF16),
            pltpu.VMEM((nc, ROW_TILE, CHUNK), BF16),
            pltpu.VMEM((ROW_TILE, D_MODEL), BF16),
            pltpu.VMEM((2, HALF_ROWS, D_EXPERT), F32),
            pltpu.SemaphoreType.DMA((2,)),
        ],
        compiler_params=_params("arbitrary", "arbitrary", vmem_limit=V7X_VMEM_LIMIT_TAIL),
        name="mixer_tail",
    )(sinks, x2d, hb, q, kd, kd, vd, vd, kvm, g0, b0, um, cw, wall, wall, wall, wall, g1, b1, rw, rb, wg, wu)


RCHUNK = 1024
PBLK = 256


def _route_kernel(lg_ref, e_ref, w_ref, pos_ref, cnt_ref, rank_ref):
    n = lg_ref.shape[1]
    epg = EXPERTS_PER_GROUP

    def route_chunk(c, carry):
        sl = pl.ds(pl.multiple_of(c * RCHUNK, RCHUNK), RCHUNK)
        gl = [lg_ref[gi:gi + 1, sl] for gi in range(N_GROUPS)]
        gmax = functools.reduce(jnp.maximum, gl)
        ge = [jnp.exp(v - gmax) for v in gl]
        gsum = functools.reduce(jnp.add, ge)
        gp = [v / gsum for v in ge]
        p_group = functools.reduce(jnp.maximum, gp)
        gsel = jnp.full_like(p_group, float(N_GROUPS - 1))
        for gi in range(N_GROUPS - 2, -1, -1):
            gsel = jnp.where(gp[gi] >= p_group, float(gi), gsel)
        e_in = lg_ref[EXPERT_ROW0 + (N_GROUPS - 1) * epg:EXPERT_ROW0 + N_GROUPS * epg, sl]
        for gi in range(N_GROUPS - 2, -1, -1):
            e_in = jnp.where(gsel == float(gi),
                             lg_ref[EXPERT_ROW0 + gi * epg:EXPERT_ROW0 + (gi + 1) * epg, sl], e_in)
        ee = jnp.exp(e_in - jnp.max(e_in, axis=0, keepdims=True))
        pe = ee / jnp.sum(ee, axis=0, keepdims=True)
        ridx = lax.broadcasted_iota(jnp.int32, pe.shape, 0).astype(F32)
        p0 = jnp.max(pe, axis=0, keepdims=True)
        i0 = jnp.min(jnp.where(pe >= p0, ridx, float(epg)), axis=0, keepdims=True)
        pe2 = jnp.where(ridx == i0, -1.0, pe)
        p1 = jnp.max(pe2, axis=0, keepdims=True)
        i1 = jnp.min(jnp.where(pe2 >= p1, ridx, float(epg)), axis=0, keepdims=True)
        den = p0 + p1
        e_ref[0:1, sl] = (gsel * epg + i0).astype(jnp.int32)
        e_ref[1:2, sl] = (gsel * epg + i1).astype(jnp.int32)
        w_ref[0:1, sl] = p0 / den * p_group
        w_ref[1:2, sl] = p1 / den * p_group
        return carry

    lax.fori_loop(0, n // RCHUNK, route_chunk, 0)

    eid = lax.broadcasted_iota(jnp.int32, (N_EXPERTS, PBLK), 0)
    before = jnp.where(lax.broadcasted_iota(jnp.int32, (PBLK, PBLK), 0)
                       < lax.broadcasted_iota(jnp.int32, (PBLK, PBLK), 1), 1.0, 0.0).astype(BF16)
    ones = jnp.ones((PBLK, LANE), BF16)
    nblk = n // PBLK

    def onehot(k, blk):
        sl = pl.ds(pl.multiple_of(blk * PBLK, PBLK), PBLK)
        return sl, eid == e_ref[pl.ds(k, 1), sl]

    run = jnp.zeros((N_EXPERTS, LANE), F32)
    for k in range(2):
        def count_block(blk, run, k=k):
            sl, hit = onehot(k, blk)
            hit_bf = jnp.where(hit, 1.0, 0.0).astype(BF16)
            prior = _dot(hit_bf, before) + jnp.concatenate([run] * (PBLK // LANE), axis=1)
            rank_ref[pl.ds(k, 1), sl] = jnp.sum(jnp.where(hit, prior, 0.0), axis=0, keepdims=True)
            return run + _dot(hit_bf, ones)
        run = lax.fori_loop(0, nblk, count_block, run)

    cnt_ref[...] = run.astype(jnp.int32)
    hi = jnp.floor(run * (1.0 / LANE))
    lo = run - hi * LANE
    below = jnp.where(lax.broadcasted_iota(jnp.int32, (N_EXPERTS, N_EXPERTS), 1)
                      < lax.broadcasted_iota(jnp.int32, (N_EXPERTS, N_EXPERTS), 0), 1.0, 0.0).astype(BF16)
    off = _dot(below, hi.astype(BF16)) * LANE + _dot(below, lo.astype(BF16))
    off = jnp.concatenate([off] * (PBLK // LANE), axis=1)

    for k in range(2):
        def place_block(blk, carry, k=k):
            sl, hit = onehot(k, blk)
            base = jnp.sum(jnp.where(hit, off, 0.0), axis=0, keepdims=True)
            pos_ref[pl.ds(k, 1), sl] = (base + rank_ref[pl.ds(k, 1), sl]).astype(jnp.int32)
            return carry
        lax.fori_loop(0, nblk, place_block, 0)


def _route(logits_t):
    n = logits_t.shape[1]
    return pl.pallas_call(
        _route_kernel,
        out_shape=[
            jax.ShapeDtypeStruct((2, n), jnp.int32),
            jax.ShapeDtypeStruct((2, n), F32),
            jax.ShapeDtypeStruct((2, n), jnp.int32),
            jax.ShapeDtypeStruct((N_EXPERTS, LANE), jnp.int32),
        ],
        scratch_shapes=[pltpu.VMEM((2, n), F32)],
        compiler_params=pltpu.CompilerParams(vmem_limit_bytes=V7X_VMEM_LIMIT),
        name="route",
    )(logits_t)


def _row_copy(src_ref, src_row, dst_ref, dst_row, sem):
    return pltpu.make_async_copy(src_ref.at[pl.ds(src_row, 1)], dst_ref.at[pl.ds(dst_row, 1)], sem)


def _dispatch_kernel(pos_ref, h_ref, xs_ref, sem):
    for r in range(TOK_TILE):
        for k in range(2):
            _row_copy(h_ref, r, xs_ref, pos_ref[k, r], sem).start(priority=k)
    for k in range(2):
        pltpu.make_async_copy(h_ref, xs_ref.at[pl.ds(0, TOK_TILE)], sem).wait()


def _dispatch(pos, h1):
    n = h1.shape[0]
    return pl.pallas_call(
        _dispatch_kernel,
        grid=(n // TOK_TILE,),
        in_specs=[
            pl.BlockSpec((2, TOK_TILE), lambda i: (0, i), memory_space=pltpu.SMEM),
            pl.BlockSpec((TOK_TILE, D_MODEL), lambda i: (i, 0)),
        ],
        out_specs=pl.BlockSpec(memory_space=pl.ANY),
        out_shape=jax.ShapeDtypeStruct((2 * n, D_MODEL), F32),
        scratch_shapes=[pltpu.SemaphoreType.DMA],
        compiler_params=_params("arbitrary"),
        name="dispatch",
    )(pos, h1)


W_SLOTS = 2
W_LOOKAHEAD = W_SLOTS - 1
def _moe_kernel(tile_ref, exp_ref, lo_ref, first_ref, slot_ref, ahead_ref, head_ref, n_ref,
                xs_ref, wgu_hbm, wd_hbm, y_ref,
                wg_buf, wu_buf, wd_buf, ytile_ref, sem):
    i = pl.program_id(0)

    def weight_copies(expert, slot):
        return (pltpu.make_async_copy(wgu_hbm.at[expert], wg_buf.at[slot], sem.at[slot, 0]),
                pltpu.make_async_copy(wgu_hbm.at[N_EXPERTS + expert], wu_buf.at[slot], sem.at[slot, 1]),
                pltpu.make_async_copy(wd_hbm.at[expert], wd_buf.at[slot], sem.at[slot, 2]))

    @pl.when(i == 0)
    def _():
        ytile_ref[...] = jnp.zeros_like(ytile_ref)
        for k in range(W_LOOKAHEAD):
            @pl.when(head_ref[k] >= 0)
            def _(k=k):
                for copy in weight_copies(head_ref[k], k):
                    copy.start()

    valid = i < n_ref[0]

    @pl.when(valid & (first_ref[i] == 1))
    def _():
        slot = slot_ref[i]
        for copy in weight_copies(exp_ref[i], slot):
            copy.wait()

        @pl.when(ahead_ref[i] >= 0)
        def _():
            for copy in weight_copies(ahead_ref[i], (slot + W_LOOKAHEAD) % W_SLOTS):
                copy.start()

    @pl.when(valid)
    def _():
        slot = slot_ref[i]
        x = xs_ref[...].astype(BF16)
        g = _dot(x, wg_buf[slot])
        u = _dot(x, wu_buf[slot])
        a = (g * jax.nn.sigmoid(g) * u).astype(BF16)
        y = _dot(a, wd_buf[slot].astype(BF16))
        row = lax.broadcasted_iota(jnp.int32, (MOE_TILE, 1), 0)
        merged = jnp.where(row >= lo_ref[i], y, ytile_ref[...])
        ytile_ref[...] = merged
        y_ref[...] = merged


def _moe_experts(items, xs, wgu, wd):
    rows = xs.shape[0]
    tile_map = lambda i, t, *_: (t[i], 0)
    grid_spec = pltpu.PrefetchScalarGridSpec(
        num_scalar_prefetch=len(items),
        grid=(MAX_ITEMS,),
        in_specs=[
            pl.BlockSpec((MOE_TILE, D_MODEL), tile_map),
            pl.BlockSpec(memory_space=pl.ANY),
            pl.BlockSpec(memory_space=pl.ANY),
        ],
        out_specs=pl.BlockSpec((MOE_TILE, D_MODEL), tile_map),
        scratch_shapes=[
            pltpu.VMEM((W_SLOTS, D_MODEL, D_EXPERT), BF16),
            pltpu.VMEM((W_SLOTS, D_MODEL, D_EXPERT), BF16),
            pltpu.VMEM((W_SLOTS, D_EXPERT, D_MODEL), F32),
            pltpu.VMEM((MOE_TILE, D_MODEL), F32),
            pltpu.SemaphoreType.DMA((W_SLOTS, 3)),
        ],
    )
    return pl.pallas_call(
        _moe_kernel,
        grid_spec=grid_spec,
        out_shape=jax.ShapeDtypeStruct((rows, D_MODEL), F32),
        compiler_params=_params("arbitrary"),
        name="moe_experts",
    )(*items, xs, wgu, wd)


def _work_items(counts):
    i32 = jnp.int32
    off = jnp.cumsum(counts) - counts
    end = off + counts
    first_tile = off // MOE_TILE
    n_e = jnp.where(counts > 0, (end - 1) // MOE_TILE - first_tile + 1, 0)
    item_end = jnp.cumsum(n_e)
    n_items = item_end[-1]
    idx = jnp.minimum(jnp.arange(MAX_ITEMS, dtype=i32), n_items - 1)
    exp = jnp.sum((item_end[None, :] <= idx[:, None]).astype(i32), axis=1)
    eid = jnp.arange(N_EXPERTS, dtype=i32)
    is_exp = exp[:, None] == eid[None, :]
    at_exp = lambda table: jnp.sum(jnp.where(is_exp, table[None, :], 0), axis=1)
    item0 = at_exp(item_end - n_e)
    tile = at_exp(first_tile) + idx - item0
    lo = jnp.maximum(at_exp(off) - tile * MOE_TILE, 0)
    present = counts > 0
    ordinal = jnp.cumsum(present.astype(i32)) - 1
    is_kth = present[None, :] & (ordinal[None, :] == eid[:, None])
    kth = jnp.where(jnp.any(is_kth, axis=1), jnp.sum(jnp.where(is_kth, eid[None, :], 0), axis=1), -1)
    item_ord = at_exp(ordinal)
    ahead = jnp.sum(jnp.where(item_ord[:, None] + W_LOOKAHEAD == eid[None, :], kth[None, :], 0), axis=1)
    ahead = jnp.where(item_ord + W_LOOKAHEAD < N_EXPERTS, ahead, -1)
    items = (tile, exp, lo, idx == item0, item_ord % W_SLOTS, ahead, kth[:W_LOOKAHEAD], n_items.reshape(1))
    return tuple(v.astype(i32) for v in items)


def _combine_kernel(pos_ref, pos_next_ref, h_ref, wt_ref, g_ref, b_ref, y_ref, o_ref, buf_ref, sem):
    i = pl.program_id(0)
    slot = i % 2

    def gather(tile_pos_ref, into):
        for r in range(TOK_TILE):
            for k in range(2):
                _row_copy(y_ref, tile_pos_ref[k, r], buf_ref.at[into, k], r, sem.at[into]).start(priority=k)

    @pl.when(i == 0)
    def _():
        gather(pos_ref, 0)

    for other in range(2):
        @pl.when((i + 1 < pl.num_programs(0)) & (slot == 1 - other))
        def _(other=other):
            gather(pos_next_ref, other)

    for k in range(2):
        pltpu.make_async_copy(y_ref.at[pl.ds(0, TOK_TILE)], buf_ref.at[slot, k], sem.at[slot]).wait()
    f = wt_ref[:, 0:1] * buf_ref[slot, 0] + wt_ref[:, 1:2] * buf_ref[slot, 1]
    o_ref[...] = _layer_norm(ALPHA * h_ref[...] + f, g_ref[...], b_ref[...])


def _combine_ln(pos, h1, wt, g, b, y):
    n = h1.shape[0]
    steps = n // TOK_TILE
    return pl.pallas_call(
        _combine_kernel,
        grid=(steps,),
        in_specs=[
            pl.BlockSpec((2, TOK_TILE), lambda i: (0, i), memory_space=pltpu.SMEM),
            pl.BlockSpec((2, TOK_TILE), lambda i: (0, jnp.minimum(i + 1, steps - 1)), memory_space=pltpu.SMEM),
            pl.BlockSpec((TOK_TILE, D_MODEL), lambda i: (i, 0)),
            pl.BlockSpec((TOK_TILE, 2), lambda i: (i, 0)),
            pl.BlockSpec((1, D_MODEL), lambda i: (0, 0)),
            pl.BlockSpec((1, D_MODEL), lambda i: (0, 0)),
            pl.BlockSpec(memory_space=pl.ANY),
        ],
        out_specs=pl.BlockSpec((TOK_TILE, D_MODEL), lambda i: (i, 0)),
        out_shape=jax.ShapeDtypeStruct((n, D_MODEL), F32),
        scratch_shapes=[pltpu.VMEM((2, 2, TOK_TILE, D_MODEL), F32), pltpu.SemaphoreType.DMA((2,))],
        compiler_params=_params("arbitrary"),
        name="combine_ln",
    )(pos, pos, h1, wt, g, b, y)


def _dup_heads(w):
    d = w.shape[0]
    w = w.reshape(d, N_KV_HEADS, 1, HEAD_DIM)
    return jnp.broadcast_to(w, (d, N_KV_HEADS, 2, HEAD_DIM)).reshape(d, 2 * KV_WIDTH)


def kernel(x, meta_tokens, ln_in_g, ln_in_b, w_in, sinks, conv_w, w_branch_attn, w_branch_conv, w_out,
           ln1_g, ln1_b, router_group_w, router_group_b, router_expert_w, router_expert_b,
           w_gate, w_up, w_down, ln2_g, ln2_b):
    batch, seq, d = x.shape
    n = batch * seq
    assert d == D_MODEL and seq % ROW_TILE == 0 and w_in.shape[0] == DEPTH == 1
    row = lambda v: v.reshape(1, -1).astype(F32)
    x2d = x.reshape(n, d)
    g0, b0 = row(ln_in_g), row(ln_in_b)

    win = w_in[0]
    kcol, vcol, ccol = D_MODEL, D_MODEL + KV_WIDTH, D_MODEL + 2 * KV_WIDTH
    wkv_dup = jnp.concatenate([_dup_heads(win[:, kcol:vcol]), _dup_heads(win[:, vcol:ccol])], axis=1).astype(BF16)
    gap = jnp.zeros((d + 1, EXPERT_ROW0 - N_GROUPS), F32)
    tail = jnp.zeros((d + 1, ROUTE_ROWS - EXPERT_ROW0 - N_EXPERTS), F32)
    group_wb = jnp.concatenate([router_group_w[0], router_group_b[0][None, :]], axis=0)
    expert_wb = jnp.concatenate([router_expert_w[0], router_expert_b[0][None, :]], axis=0)
    rwb = jnp.concatenate([group_wb, gap, expert_wb, tail], axis=1).T
    rw = rwb[:, :d]
    rb = jnp.broadcast_to(rwb[:, d:], (ROUTE_ROWS, LANE))

    hb, q, kd, vd, wall = _ln_qkv(x2d, g0, b0, win, wkv_dup, w_branch_attn[0], w_branch_conv[0], w_out[0])
    kvm, um = _meta_prep(meta_tokens.astype(F32), g0, b0, wkv_dup, wall)
    h1, logits_t, wgu = _mixer_tail(sinks[0].astype(F32), x2d, hb, q, kd, vd, kvm, g0, b0, um, conv_w[0], wall,
                                    row(ln1_g[0]), row(ln1_b[0]), rw.astype(BF16), rb, w_gate[0], w_up[0], seq)
    _, top_w, pos, counts = _route(logits_t)
    xs = _dispatch(pos, h1)
    y = _moe_experts(_work_items(counts[:, 0]), xs, wgu, w_down[0])
    out = _combine_ln(pos, h1, top_w.T, row(ln2_g[0]), row(ln2_b[0]), y)
    return out.reshape(batch, seq, d)
```

```python
import functools

import jax
import jax.numpy as jnp
from jax import lax
from jax.experimental import pallas as pl
from jax.experimental.pallas import tpu as pltpu

D_MODEL = 2048
N_META = 16
HEAD_DIM = 64
N_Q_HEADS = 32
N_KV_HEADS = 4
BLOCK = 128
KV_WIDTH = N_KV_HEADS * HEAD_DIM
CONV_K = 3
N_GROUPS = 4
EXPERTS_PER_GROUP = 8
N_EXPERTS = N_GROUPS * EXPERTS_PER_GROUP
D_EXPERT = D_MODEL // 4
LN_EPS = 1e-5
DEPTH = 1
ALPHA = (2.0 * DEPTH) ** 0.25
NEG_INF = -1e30
Q_SCALE = HEAD_DIM ** -0.5

V7X_VMEM_LIMIT = 56 * 1024 * 1024
V7X_VMEM_LIMIT_TAIL = 58 * 1024 * 1024
LANE = 128
CHUNK = 512
N_CHUNKS = D_MODEL // CHUNK
ROW_TILE = 512
MOE_TILE = 256
DISPATCH_TILE = 1024
COMBINE_TILE = 256
MAX_ITEMS = 2 * 8192 // MOE_TILE + N_EXPERTS
ROUTE_ROWS = 128
EXPERT_ROW0 = 8

BF16 = jnp.bfloat16
F32 = jnp.float32


def _layer_norm(x, g, b):
    mu = jnp.mean(x, axis=-1, keepdims=True)
    xc = x - mu
    var = jnp.mean(xc * xc, axis=-1, keepdims=True)
    return xc * lax.rsqrt(var + LN_EPS) * g + b


def _dot(a, b):
    return jnp.dot(a, b, preferred_element_type=F32)


def _dot_nt(a, b):
    return lax.dot_general(a, b, (((1,), (1,)), ((), ())), preferred_element_type=F32)


def _params(*sem, vmem_limit=V7X_VMEM_LIMIT):
    return pltpu.CompilerParams(dimension_semantics=sem, vmem_limit_bytes=vmem_limit)


W_B, W_C, W_H, W_GA, W_GC, W_BA, W_BC, W_OUT = (k * N_CHUNKS for k in range(8))
N_WBLOCKS = 8 * N_CHUNKS
IN_FIRST_MIX = (D_MODEL + 2 * KV_WIDTH) // CHUNK


def _meta_kernel(meta_ref, g_ref, b_ref, wkv_ref, wcc_ref, wch_ref, kv_ref, u_ref):
    hm = _layer_norm(meta_ref[...], g_ref[...], b_ref[...]).astype(BF16)
    kv_ref[...] = _dot(hm, wkv_ref[...]).astype(BF16)
    u_ref[...] = _dot(hm, wcc_ref[...]) * _dot(hm, wch_ref[...])


def _meta_prep(meta, g, b, wkv_dup, wall):
    n_kv = wkv_dup.shape[1] // N_CHUNKS
    return pl.pallas_call(
        _meta_kernel,
        grid=(N_CHUNKS,),
        in_specs=[
            pl.BlockSpec((N_META, D_MODEL), lambda j: (0, 0)),
            pl.BlockSpec((1, D_MODEL), lambda j: (0, 0)),
            pl.BlockSpec((1, D_MODEL), lambda j: (0, 0)),
            pl.BlockSpec((D_MODEL, n_kv), lambda j: (0, j)),
            pl.BlockSpec((D_MODEL, CHUNK), lambda j: (0, W_C + j)),
            pl.BlockSpec((D_MODEL, CHUNK), lambda j: (0, W_H + j)),
        ],
        out_specs=[
            pl.BlockSpec((N_META, n_kv), lambda j: (0, j)),
            pl.BlockSpec((N_META, CHUNK), lambda j: (0, j)),
        ],
        out_shape=[
            jax.ShapeDtypeStruct((N_META, wkv_dup.shape[1]), BF16),
            jax.ShapeDtypeStruct((N_META, D_MODEL), F32),
        ],
        compiler_params=_params("arbitrary"),
        name="meta_prep",
    )(meta, g, b, wkv_dup, wall, wall)


def _qkv_kernel(x_ref, g_ref, b_ref, wq_ref, wkv_ref, win_hbm, ba_hbm, bc_hbm, out_hbm,
                hb_ref, q_ref, k_ref, v_ref, wall_ref, wq_bf, stage, sem):
    i = pl.program_id(0)
    kw = 2 * KV_WIDTH

    def start_block(j, slot):
        def copy(src_hbm, block):
            col = pl.multiple_of(block * CHUNK, CHUNK)
            pltpu.make_async_copy(src_hbm.at[:, pl.ds(col, CHUNK)], stage.at[slot], sem.at[slot]).start()

        @pl.when(j < W_BA)
        def _():
            copy(win_hbm, IN_FIRST_MIX + j)

        for src_hbm, first in ((ba_hbm, W_BA), (bc_hbm, W_BC), (out_hbm, W_OUT)):
            @pl.when((j >= first) & (j < first + N_CHUNKS))
            def _(src_hbm=src_hbm, first=first):
                copy(src_hbm, j - first)

    @pl.when(i == 0)
    def _():
        start_block(0, 0)
        wq_bf[...] = wq_ref[...].astype(BF16)

    slot = i % 2
    pltpu.make_async_copy(ba_hbm.at[:, pl.ds(0, CHUNK)], stage.at[slot], sem.at[slot]).wait()

    @pl.when(i + 1 < N_WBLOCKS)
    def _():
        start_block(i + 1, 1 - slot)

    wall_ref[...] = stage[slot].astype(BF16)
    hb = _layer_norm(x_ref[...], g_ref[...], b_ref[...]).astype(BF16)
    hb_ref[...] = hb
    q_ref[...] = (_dot(hb, wq_bf[...]) * Q_SCALE).astype(BF16)
    k_ref[...] = _dot(hb, wkv_ref[:, :kw]).astype(BF16)
    v_ref[...] = _dot(hb, wkv_ref[:, kw:]).astype(BF16)


def _ln_qkv(x2d, g, b, win, wkv_dup, wba, wbc, wout):
    n = x2d.shape[0]
    tile = n // N_WBLOCKS
    assert tile * N_WBLOCKS == n and tile % 16 == 0
    kw = 2 * KV_WIDTH
    row = lambda i: (i, 0)
    const = lambda i: (0, 0)
    once = dict(pipeline_mode=pl.Buffered(1))
    any_spec = pl.BlockSpec(memory_space=pl.ANY)
    return pl.pallas_call(
        _qkv_kernel,
        grid=(N_WBLOCKS,),
        in_specs=[
            pl.BlockSpec((tile, D_MODEL), row),
            pl.BlockSpec((1, D_MODEL), const),
            pl.BlockSpec((1, D_MODEL), const),
            pl.BlockSpec((D_MODEL, D_MODEL), const, **once),
            pl.BlockSpec(wkv_dup.shape, const, **once),
            any_spec, any_spec, any_spec, any_spec,
        ],
        out_specs=[
            pl.BlockSpec((tile, D_MODEL), row),
            pl.BlockSpec((tile, D_MODEL), row),
            pl.BlockSpec((tile, kw), row),
            pl.BlockSpec((tile, kw), row),
            pl.BlockSpec((D_MODEL, CHUNK), lambda i: (0, i)),
        ],
        out_shape=[
            jax.ShapeDtypeStruct((n, D_MODEL), BF16),
            jax.ShapeDtypeStruct((n, D_MODEL), BF16),
            jax.ShapeDtypeStruct((n, kw), BF16),
            jax.ShapeDtypeStruct((n, kw), BF16),
            jax.ShapeDtypeStruct((D_MODEL, N_WBLOCKS * CHUNK), BF16),
        ],
        scratch_shapes=[
            pltpu.VMEM((D_MODEL, D_MODEL), BF16),
            pltpu.VMEM((2, D_MODEL, CHUNK), F32),
            pltpu.SemaphoreType.DMA((2,)),
        ],
        compiler_params=_params("arbitrary"),
        name="ln_qkv",
    )(x2d, g, b, win, wkv_dup, win, wba, wbc, wout)


PAIRS = N_Q_HEADS // N_KV_HEADS // 2
QROWS = PAIRS * BLOCK


class _AttnBlock:
    def __init__(self, sink_ref, q_ref, kp_ref, kc_ref, vp_ref, vc_ref, kvm_ref, has_prev):
        self.sink_ref, self.q_ref, self.kvm_ref, self.has_prev = sink_ref, q_ref, kvm_ref, has_prev
        self.k_refs, self.v_refs = (kp_ref, kc_ref), (vp_ref, vc_ref)
        self.lane = lax.broadcasted_iota(jnp.int32, (BLOCK, LANE), 1)
        qi = lax.broadcasted_iota(jnp.int32, (QROWS, BLOCK), 0) % BLOCK
        kj = lax.broadcasted_iota(jnp.int32, (QROWS, BLOCK), 1)
        self.from_prev = kj > qi
        self.is_meta = kj < N_META
        self.rowpair = lax.broadcasted_iota(jnp.int32, (QROWS, 1), 0) // BLOCK
        self.zpad = jnp.zeros((BLOCK - N_META, LANE), BF16)

    def _masked_rows(self, refs, meta_cols, g, half):
        keep = (self.lane >= HEAD_DIM) if half else (self.lane < HEAD_DIM)
        gs = slice(g * LANE, (g + 1) * LANE)
        blocks = [r[:, gs] for r in refs] + [jnp.concatenate([self.kvm_ref[:, meta_cols], self.zpad], axis=0)]
        return jnp.concatenate([jnp.where(keep, b, jnp.zeros((), BF16)) for b in blocks], axis=0)

    def scores(self, g, half):
        qs = jnp.concatenate([self.q_ref[:, (g * PAIRS + p) * LANE:(g * PAIRS + p + 1) * LANE]
                              for p in range(PAIRS)], axis=0)
        return _dot_nt(qs, self._masked_rows(self.k_refs, slice(g * LANE, (g + 1) * LANE), g, half))

    def probs(self, s, g, half):
        s_prev = jnp.where(self.has_prev, s[:, :BLOCK], NEG_INF)
        s_band = jnp.where(self.from_prev, s_prev, s[:, BLOCK:2 * BLOCK])
        s_meta = jnp.where(self.is_meta, s[:, 2 * BLOCK:], NEG_INF)
        sink = jnp.zeros((QROWS, 1), F32)
        for p in range(PAIRS):
            sink = jnp.where(self.rowpair == p, self.sink_ref[g * 2 * PAIRS + 2 * p + half], sink)
        m = jnp.maximum(jnp.max(jnp.maximum(s_band, s_meta), axis=-1, keepdims=True), sink)
        e_band = jnp.exp(s_band - m)
        e_meta = jnp.exp(s_meta - m)
        den = jnp.sum(e_band + e_meta, axis=-1, keepdims=True) + jnp.exp(sink - m)
        p_band = e_band / den
        return jnp.concatenate([jnp.where(self.from_prev, p_band, 0.0).astype(BF16),
                                jnp.where(self.from_prev, 0.0, p_band).astype(BF16),
                                (e_meta / den).astype(BF16)], axis=1)

    def values(self, probs, g, half):
        meta_cols = slice((N_KV_HEADS + g) * LANE, (N_KV_HEADS + g + 1) * LANE)
        return _dot(probs, self._masked_rows(self.v_refs, meta_cols, g, half))

    @staticmethod
    def store(o_ref, g, acc):
        for p in range(PAIRS):
            col = (g * PAIRS + p) * LANE
            o_ref[:, col:col + LANE] = acc[p * BLOCK:(p + 1) * BLOCK].astype(BF16)


TAIL_STEPS = 2 * N_CHUNKS + 1
HALF_ROWS = D_MODEL // 2


def _lane_concat(ref):
    return jnp.concatenate([ref[k] for k in range(ref.shape[0])], axis=1)


def _tail_kernel(sink_ref, x_ref, hb_ref, q_ref, kp_ref, kc_ref, vp_ref, vc_ref, kvm_ref,
                 g0_ref, b0_ref, um_ref, cw_ref, wa_ref, wb_ref, wc_ref, wd_ref,
                 g1_ref, b1_ref, rw_ref, rb_ref, wg_hbm, wu_hbm, h1_ref, lg_ref, wgu_ref,
                 uext_ref, carry_ref, z_ref, m_ref, attn_ref, stage_ref, wsem, *, tiles_per_seq):
    i = pl.program_id(0)
    s = pl.program_id(1)
    tm = ROW_TILE

    cast_step = i * (2 * N_CHUNKS) + s
    cast_slot = cast_step % 2

    def start_half(t, slot):
        mat, half = t // 2, t % 2
        rows = pl.ds(pl.multiple_of(half * HALF_ROWS, HALF_ROWS), HALF_ROWS)

        @pl.when(mat < N_EXPERTS)
        def _():
            pltpu.make_async_copy(wg_hbm.at[mat, rows], stage_ref.at[slot], wsem.at[slot]).start()

        @pl.when(mat >= N_EXPERTS)
        def _():
            pltpu.make_async_copy(wu_hbm.at[mat - N_EXPERTS, rows], stage_ref.at[slot], wsem.at[slot]).start()

    @pl.when((i == 0) & (s == 0))
    def _():
        carry_ref[...] = jnp.zeros_like(carry_ref)
        start_half(0, 0)

    @pl.when(s < 2 * N_CHUNKS)
    def _():
        pltpu.make_async_copy(wg_hbm.at[0, pl.ds(0, HALF_ROWS)], stage_ref.at[cast_slot], wsem.at[cast_slot]).wait()

        @pl.when(cast_step + 1 < pl.num_programs(0) * 2 * N_CHUNKS)
        def _():
            start_half(cast_step + 1, 1 - cast_slot)

    @pl.when(s < N_CHUNKS)
    def _():
        wgu_ref[0] = stage_ref[cast_slot].astype(BF16)
        hb = hb_ref[...]
        first = i % tiles_per_seq == 0
        attn_out = attn_ref.at[pl.ds(pl.multiple_of(s * BLOCK, BLOCK), BLOCK)]
        attn = _AttnBlock(sink_ref, q_ref, kp_ref, kc_ref, vp_ref, vc_ref, kvm_ref,
                          has_prev=(i % tiles_per_seq) * N_CHUNKS + s > 0)
        units = [(g, half) for g in range(N_KV_HEADS) for half in range(2)]
        half_cols = CHUNK // 2
        dots = [(w_ref, slice(h * half_cols, (h + 1) * half_cols))
                for h in range(2) for w_ref in (wa_ref, wb_ref, wc_ref)]

        def conv_half(cs, c_h, h_h, b_h):
            u = c_h * h_h
            uext_ref[0:8, cs] = jnp.where(first, um_ref[N_META - 8:N_META, cs], carry_ref[s, :, cs])
            uext_ref[8:tm + 8, cs] = u
            conv = (cw_ref[0:1, cs] * uext_ref[6:tm + 6, cs] + cw_ref[1:2, cs] * uext_ref[7:tm + 7, cs]
                    + cw_ref[2:3, cs] * u)
            carry_ref[s, :, cs] = u[tm - 8:tm, :]
            z_ref[s, :, cs] = (b_h * conv).astype(BF16)

        scores = attn.scores(*units[0])
        outs, acc = [], None
        for k, unit in enumerate(units):
            if k < len(dots):
                w_ref, cs = dots[k]
                outs.append(_dot(hb, w_ref[:, cs]))
            probs = attn.probs(scores, *unit)
            part = attn.values(probs, *unit)
            if k + 1 < len(units):
                scores = attn.scores(*units[k + 1])
            acc = part if unit[1] == 0 else acc + part
            if unit[1] == 1:
                attn.store(attn_out, unit[0], acc)
            if k % 3 == 2 and k < len(dots):
                conv_half(dots[k][1], *outs[k - 2:k + 1])

    @pl.when((s >= N_CHUNKS) & (s < 2 * N_CHUNKS))
    def _():
        wgu_ref[0] = stage_ref[cast_slot].astype(BF16)
        hb = hb_ref[...]
        conv_part = jax.nn.sigmoid(_dot(hb, wb_ref[...])) * _dot(_lane_concat(z_ref), wd_ref[...])
        gate_attn = jax.nn.sigmoid(_dot(hb, wa_ref[...]))
        m_ref[s - N_CHUNKS] = (gate_attn * _dot(attn_ref[...], wc_ref[...]) + conv_part).astype(BF16)

    @pl.when(s == 2 * N_CHUNKS)
    def _():
        h = _layer_norm(x_ref[...], g0_ref[...], b0_ref[...])
        m = _lane_concat(m_ref)
        for k, w_ref in enumerate((wa_ref, wb_ref, wc_ref, wd_ref)):
            cs = slice(k * CHUNK, (k + 1) * CHUNK)
            h1_ref[:, cs] = ALPHA * h[:, cs] + _dot(m, w_ref[...])
        h1 = _layer_norm(h1_ref[...], g1_ref[...], b1_ref[...])
        h1_ref[...] = h1
        lg_ref[...] = _dot_nt(rw_ref[...], h1.astype(BF16)) + rb_ref[...][:, 0:1]


def _mixer_tail(sinks, x2d, hb, q, kd, vd, kvm, g0, b0, um, cw, wall, g1, b1, rw, rb, wg, wu, seq):
    n = x2d.shape[0]
    nc = N_CHUNKS
    assert ROW_TILE == nc * BLOCK
    assert (n // ROW_TILE) * 2 * nc == 2 * 2 * N_EXPERTS
    cast_map = lambda i, s: ((i * 2 * nc + jnp.minimum(s, 2 * nc - 1)) // 2, (i * 2 * nc + jnp.minimum(s, 2 * nc - 1)) % 2, 0)
    any_spec = pl.BlockSpec(memory_space=pl.ANY)
    tiles_per_seq = seq // ROW_TILE
    kw = 2 * KV_WIDTH
    cur = lambda i, s: (i * nc + jnp.minimum(s, nc - 1), 0)
    prev = lambda i, s: (jnp.maximum(i * nc + jnp.minimum(s, nc - 1) - 1, (i // tiles_per_seq) * (seq // BLOCK)), 0)

    def slot(conv_first, merge_first, out_block, hold=False):
        def index(i, s):
            conv = conv_first if hold else conv_first + s
            return 0, jnp.where(s < nc, conv, jnp.where(s < 2 * nc, merge_first + s - nc, W_OUT + out_block))
        return pl.BlockSpec((D_MODEL, CHUNK), index)

    lo = lambda i, s: (0, jnp.minimum(s, nc - 1))
    const = lambda i, s: (0, 0)
    row = lambda i, s: (i, 0)
    return pl.pallas_call(
        functools.partial(_tail_kernel, tiles_per_seq=tiles_per_seq),
        grid=(n // ROW_TILE, TAIL_STEPS),
        in_specs=[
            pl.BlockSpec(memory_space=pltpu.SMEM),
            pl.BlockSpec((ROW_TILE, D_MODEL), row),
            pl.BlockSpec((ROW_TILE, D_MODEL), row),
            pl.BlockSpec((BLOCK, D_MODEL), cur),
            pl.BlockSpec((BLOCK, kw), prev),
            pl.BlockSpec((BLOCK, kw), cur),
            pl.BlockSpec((BLOCK, kw), prev),
            pl.BlockSpec((BLOCK, kw), cur),
            pl.BlockSpec((N_META, 2 * kw), const),
            pl.BlockSpec((1, D_MODEL), const),
            pl.BlockSpec((1, D_MODEL), const),
            pl.BlockSpec((N_META, CHUNK), lo),
            pl.BlockSpec((CONV_K, CHUNK), lo),
            slot(W_C, W_GA, 0),
            slot(W_H, W_GC, 1),
            slot(W_B, W_BA, 2),
            slot(W_BC, W_BC, 3, hold=True),
            pl.BlockSpec((1, D_MODEL), const),
            pl.BlockSpec((1, D_MODEL), const),
            pl.BlockSpec((ROUTE_ROWS, D_MODEL), const),
            pl.BlockSpec((ROUTE_ROWS, LANE), const),
            any_spec, any_spec,
        ],
        out_specs=[
            pl.BlockSpec((ROW_TILE, D_MODEL), row),
            pl.BlockSpec((ROUTE_ROWS, ROW_TILE), lambda i, s: (0, i)),
            pl.BlockSpec((1, HALF_ROWS, D_EXPERT), cast_map),
        ],
        out_shape=[
            jax.ShapeDtypeStruct((n, D_MODEL), F32),
            jax.ShapeDtypeStruct((ROUTE_ROWS, n), F32),
            jax.ShapeDtypeStruct((2 * N_EXPERTS, D_MODEL, D_EXPERT), BF16),
        ],
        scratch_shapes=[
            pltpu.VMEM((ROW_TILE + 8, CHUNK), F32),
            pltpu.VMEM((nc, 8, CHUNK), F32),
            pltpu.VMEM((nc, ROW_TILE, CHUNK), BF16),
            pltpu.VMEM((nc, ROW_TILE, CHUNK), BF16),
            pltpu.VMEM((ROW_TILE, D_MODEL), BF16),
            pltpu.VMEM((2, HALF_ROWS, D_EXPERT), F32),
            pltpu.SemaphoreType.DMA((2,)),
        ],
        compiler_params=_params("arbitrary", "arbitrary", vmem_limit=V7X_VMEM_LIMIT_TAIL),
        name="mixer_tail",
    )(sinks, x2d, hb, q, kd, kd, vd, vd, kvm, g0, b0, um, cw, wall, wall, wall, wall, g1, b1, rw, rb, wg, wu)


RCHUNK = 1024
PBLK = 256


def _route_kernel(lg_ref, e_ref, w_ref, pos_ref, cnt_ref, rank_ref):
    n = lg_ref.shape[1]
    epg = EXPERTS_PER_GROUP

    def route_chunk(c, carry):
        sl = pl.ds(pl.multiple_of(c * RCHUNK, RCHUNK), RCHUNK)
        gl = [lg_ref[gi:gi + 1, sl] for gi in range(N_GROUPS)]
        gmax = functools.reduce(jnp.maximum, gl)
        ge = [jnp.exp(v - gmax) for v in gl]
        gsum = functools.reduce(jnp.add, ge)
        gp = [v / gsum for v in ge]
        p_group = functools.reduce(jnp.maximum, gp)
        gsel = jnp.full_like(p_group, float(N_GROUPS - 1))
        for gi in range(N_GROUPS - 2, -1, -1):
            gsel = jnp.where(gp[gi] >= p_group, float(gi), gsel)
        e_in = lg_ref[EXPERT_ROW0 + (N_GROUPS - 1) * epg:EXPERT_ROW0 + N_GROUPS * epg, sl]
        for gi in range(N_GROUPS - 2, -1, -1):
            e_in = jnp.where(gsel == float(gi),
                             lg_ref[EXPERT_ROW0 + gi * epg:EXPERT_ROW0 + (gi + 1) * epg, sl], e_in)
        ee = jnp.exp(e_in - jnp.max(e_in, axis=0, keepdims=True))
        pe = ee / jnp.sum(ee, axis=0, keepdims=True)
        ridx = lax.broadcasted_iota(jnp.int32, pe.shape, 0).astype(F32)
        p0 = jnp.max(pe, axis=0, keepdims=True)
        i0 = jnp.min(jnp.where(pe >= p0, ridx, float(epg)), axis=0, keepdims=True)
        pe2 = jnp.where(ridx == i0, -1.0, pe)
        p1 = jnp.max(pe2, axis=0, keepdims=True)
        i1 = jnp.min(jnp.where(pe2 >= p1, ridx, float(epg)), axis=0, keepdims=True)
        den = p0 + p1
        e_ref[0:1, sl] = (gsel * epg + i0).astype(jnp.int32)
        e_ref[1:2, sl] = (gsel * epg + i1).astype(jnp.int32)
        w_ref[0:1, sl] = p0 / den * p_group
        w_ref[1:2, sl] = p1 / den * p_group
        return carry

    lax.fori_loop(0, n // RCHUNK, route_chunk, 0)

    eid = lax.broadcasted_iota(jnp.int32, (N_EXPERTS, PBLK), 0)
    before = jnp.where(lax.broadcasted_iota(jnp.int32, (PBLK, PBLK), 0)
                       < lax.broadcasted_iota(jnp.int32, (PBLK, PBLK), 1), 1.0, 0.0).astype(BF16)
    ones = jnp.ones((PBLK, LANE), BF16)
    nblk = n // PBLK

    def onehot(k, blk):
        sl = pl.ds(pl.multiple_of(blk * PBLK, PBLK), PBLK)
        return sl, eid == e_ref[pl.ds(k, 1), sl]

    run = jnp.zeros((N_EXPERTS, LANE), F32)
    for k in range(2):
        def count_block(blk, run, k=k):
            sl, hit = onehot(k, blk)
            hit_bf = jnp.where(hit, 1.0, 0.0).astype(BF16)
            prior = _dot(hit_bf, before) + jnp.concatenate([run] * (PBLK // LANE), axis=1)
            rank_ref[pl.ds(k, 1), sl] = jnp.sum(jnp.where(hit, prior, 0.0), axis=0, keepdims=True)
            return run + _dot(hit_bf, ones)
        run = lax.fori_loop(0, nblk, count_block, run)

    cnt_ref[...] = run.astype(jnp.int32)
    hi = jnp.floor(run * (1.0 / LANE))
    lo = run - hi * LANE
    below = jnp.where(lax.broadcasted_iota(jnp.int32, (N_EXPERTS, N_EXPERTS), 1)
                      < lax.broadcasted_iota(jnp.int32, (N_EXPERTS, N_EXPERTS), 0), 1.0, 0.0).astype(BF16)
    off = _dot(below, hi.astype(BF16)) * LANE + _dot(below, lo.astype(BF16))
    off = jnp.concatenate([off] * (PBLK // LANE), axis=1)

    for k in range(2):
        def place_block(blk, carry, k=k):
            sl, hit = onehot(k, blk)
            base = jnp.sum(jnp.where(hit, off, 0.0), axis=0, keepdims=True)
            pos_ref[pl.ds(k, 1), sl] = (base + rank_ref[pl.ds(k, 1), sl]).astype(jnp.int32)
            return carry
        lax.fori_loop(0, nblk, place_block, 0)


def _route(logits_t):
    n = logits_t.shape[1]
    return pl.pallas_call(
        _route_kernel,
        out_shape=[
            jax.ShapeDtypeStruct((2, n), jnp.int32),
            jax.ShapeDtypeStruct((2, n), F32),
            jax.ShapeDtypeStruct((2, n), jnp.int32),
            jax.ShapeDtypeStruct((N_EXPERTS, LANE), jnp.int32),
        ],
        scratch_shapes=[pltpu.VMEM((2, n), F32)],
        compiler_params=pltpu.CompilerParams(vmem_limit_bytes=V7X_VMEM_LIMIT),
        name="route",
    )(logits_t)


def _row_copy(src_ref, src_row, dst_ref, dst_row, sem):
    return pltpu.make_async_copy(src_ref.at[pl.ds(src_row, 1)], dst_ref.at[pl.ds(dst_row, 1)], sem)


def _dispatch_kernel(pos_ref, h_ref, xs_ref, sem):
    for r in range(DISPATCH_TILE):
        for k in range(2):
            _row_copy(h_ref, r, xs_ref, pos_ref[k, r], sem).start(priority=k)
    for k in range(2):
        pltpu.make_async_copy(h_ref, xs_ref.at[pl.ds(0, DISPATCH_TILE)], sem).wait()


def _dispatch(pos, h1):
    n = h1.shape[0]
    return pl.pallas_call(
        _dispatch_kernel,
        grid=(n // DISPATCH_TILE,),
        in_specs=[
            pl.BlockSpec((2, DISPATCH_TILE), lambda i: (0, i), memory_space=pltpu.SMEM),
            pl.BlockSpec((DISPATCH_TILE, D_MODEL), lambda i: (i, 0)),
        ],
        out_specs=pl.BlockSpec(memory_space=pl.ANY),
        out_shape=jax.ShapeDtypeStruct((2 * n, D_MODEL), F32),
        scratch_shapes=[pltpu.SemaphoreType.DMA],
        compiler_params=_params("arbitrary"),
        name="dispatch",
    )(pos, h1)


W_SLOTS = 2
W_LOOKAHEAD = W_SLOTS - 1
def _moe_kernel(tile_ref, exp_ref, lo_ref, first_ref, slot_ref, ahead_ref, head_ref, n_ref,
                xs_ref, wgu_hbm, wd_hbm, y_ref,
                wg_buf, wu_buf, wd_buf, ytile_ref, sem):
    i = pl.program_id(0)

    def weight_copies(expert, slot):
        return (pltpu.make_async_copy(wgu_hbm.at[expert], wg_buf.at[slot], sem.at[slot, 0]),
                pltpu.make_async_copy(wgu_hbm.at[N_EXPERTS + expert], wu_buf.at[slot], sem.at[slot, 1]),
                pltpu.make_async_copy(wd_hbm.at[expert], wd_buf.at[slot], sem.at[slot, 2]))

    @pl.when(i == 0)
    def _():
        ytile_ref[...] = jnp.zeros_like(ytile_ref)
        for k in range(W_LOOKAHEAD):
            @pl.when(head_ref[k] >= 0)
            def _(k=k):
                for copy in weight_copies(head_ref[k], k):
                    copy.start()

    valid = i < n_ref[0]

    @pl.when(valid & (first_ref[i] == 1))
    def _():
        slot = slot_ref[i]
        for copy in weight_copies(exp_ref[i], slot):
            copy.wait()

        @pl.when(ahead_ref[i] >= 0)
        def _():
            for copy in weight_copies(ahead_ref[i], (slot + W_LOOKAHEAD) % W_SLOTS):
                copy.start()

    @pl.when(valid)
    def _():
        slot = slot_ref[i]
        x = xs_ref[...].astype(BF16)
        g = _dot(x, wg_buf[slot])
        u = _dot(x, wu_buf[slot])
        a = (g * jax.nn.sigmoid(g) * u).astype(BF16)
        y = _dot(a, wd_buf[slot].astype(BF16))
        row = lax.broadcasted_iota(jnp.int32, (MOE_TILE, 1), 0)
        merged = jnp.where(row >= lo_ref[i], y, ytile_ref[...])
        ytile_ref[...] = merged
        y_ref[...] = merged


def _moe_experts(items, xs, wgu, wd):
    rows = xs.shape[0]
    tile_map = lambda i, t, *_: (t[i], 0)
    grid_spec = pltpu.PrefetchScalarGridSpec(
        num_scalar_prefetch=len(items),
        grid=(MAX_ITEMS,),
        in_specs=[
            pl.BlockSpec((MOE_TILE, D_MODEL), tile_map),
            pl.BlockSpec(memory_space=pl.ANY),
            pl.BlockSpec(memory_space=pl.ANY),
        ],
        out_specs=pl.BlockSpec((MOE_TILE, D_MODEL), tile_map),
        scratch_shapes=[
            pltpu.VMEM((W_SLOTS, D_MODEL, D_EXPERT), BF16),
            pltpu.VMEM((W_SLOTS, D_MODEL, D_EXPERT), BF16),
            pltpu.VMEM((W_SLOTS, D_EXPERT, D_MODEL), F32),
            pltpu.VMEM((MOE_TILE, D_MODEL), F32),
            pltpu.SemaphoreType.DMA((W_SLOTS, 3)),
        ],
    )
    return pl.pallas_call(
        _moe_kernel,
        grid_spec=grid_spec,
        out_shape=jax.ShapeDtypeStruct((rows, D_MODEL), F32),
        compiler_params=_params("arbitrary"),
        name="moe_experts",
    )(*items, xs, wgu, wd)


def _work_items(counts):
    i32 = jnp.int32
    off = jnp.cumsum(counts) - counts
    end = off + counts
    first_tile = off // MOE_TILE
    n_e = jnp.where(counts > 0, (end - 1) // MOE_TILE - first_tile + 1, 0)
    item_end = jnp.cumsum(n_e)
    n_items = item_end[-1]
    idx = jnp.minimum(jnp.arange(MAX_ITEMS, dtype=i32), n_items - 1)
    exp = jnp.sum((item_end[None, :] <= idx[:, None]).astype(i32), axis=1)
    eid = jnp.arange(N_EXPERTS, dtype=i32)
    is_exp = exp[:, None] == eid[None, :]
    at_exp = lambda table: jnp.sum(jnp.where(is_exp, table[None, :], 0), axis=1)
    item0 = at_exp(item_end - n_e)
    tile = at_exp(first_tile) + idx - item0
    lo = jnp.maximum(at_exp(off) - tile * MOE_TILE, 0)
    present = counts > 0
    ordinal = jnp.cumsum(present.astype(i32)) - 1
    is_kth = present[None, :] & (ordinal[None, :] == eid[:, None])
    kth = jnp.where(jnp.any(is_kth, axis=1), jnp.sum(jnp.where(is_kth, eid[None, :], 0), axis=1), -1)
    item_ord = at_exp(ordinal)
    ahead = jnp.sum(jnp.where(item_ord[:, None] + W_LOOKAHEAD == eid[None, :], kth[None, :], 0), axis=1)
    ahead = jnp.where(item_ord + W_LOOKAHEAD < N_EXPERTS, ahead, -1)
    items = (tile, exp, lo, idx == item0, item_ord % W_SLOTS, ahead, kth[:W_LOOKAHEAD], n_items.reshape(1))
    return tuple(v.astype(i32) for v in items)


def _combine_kernel(pos_ref, pos_next_ref, h_ref, wt_ref, g_ref, b_ref, y_ref, o_ref, buf_ref, sem):
    i = pl.program_id(0)
    slot = i % 2

    def gather(tile_pos_ref, into):
        for r in range(COMBINE_TILE):
            for k in range(2):
                _row_copy(y_ref, tile_pos_ref[k, r], buf_ref.at[into, k], r, sem.at[into]).start(priority=k)

    @pl.when(i == 0)
    def _():
        gather(pos_ref, 0)

    for other in range(2):
        @pl.when((i + 1 < pl.num_programs(0)) & (slot == 1 - other))
        def _(other=other):
            gather(pos_next_ref, other)

    for k in range(2):
        pltpu.make_async_copy(y_ref.at[pl.ds(0, COMBINE_TILE)], buf_ref.at[slot, k], sem.at[slot]).wait()
    f = wt_ref[:, 0:1] * buf_ref[slot, 0] + wt_ref[:, 1:2] * buf_ref[slot, 1]
    o_ref[...] = _layer_norm(ALPHA * h_ref[...] + f, g_ref[...], b_ref[...])


def _combine_ln(pos, h1, wt, g, b, y):
    n = h1.shape[0]
    steps = n // COMBINE_TILE
    return pl.pallas_call(
        _combine_kernel,
        grid=(steps,),
        in_specs=[
            pl.BlockSpec((2, COMBINE_TILE), lambda i: (0, i), memory_space=pltpu.SMEM),
            pl.BlockSpec((2, COMBINE_TILE), lambda i: (0, jnp.minimum(i + 1, steps - 1)), memory_space=pltpu.SMEM),
            pl.BlockSpec((COMBINE_TILE, D_MODEL), lambda i: (i, 0)),
            pl.BlockSpec((COMBINE_TILE, 2), lambda i: (i, 0)),
            pl.BlockSpec((1, D_MODEL), lambda i: (0, 0)),
            pl.BlockSpec((1, D_MODEL), lambda i: (0, 0)),
            pl.BlockSpec(memory_space=pl.ANY),
        ],
        out_specs=pl.BlockSpec((COMBINE_TILE, D_MODEL), lambda i: (i, 0)),
        out_shape=jax.ShapeDtypeStruct((n, D_MODEL), F32),
        scratch_shapes=[pltpu.VMEM((2, 2, COMBINE_TILE, D_MODEL), F32), pltpu.SemaphoreType.DMA((2,))],
        compiler_params=_params("arbitrary"),
        name="combine_ln",
    )(pos, pos, h1, wt, g, b, y)


def _dup_heads(w):
    d = w.shape[0]
    w = w.reshape(d, N_KV_HEADS, 1, HEAD_DIM)
    return jnp.broadcast_to(w, (d, N_KV_HEADS, 2, HEAD_DIM)).reshape(d, 2 * KV_WIDTH)


def kernel(x, meta_tokens, ln_in_g, ln_in_b, w_in, sinks, conv_w, w_branch_attn, w_branch_conv, w_out,
           ln1_g, ln1_b, router_group_w, router_group_b, router_expert_w, router_expert_b,
           w_gate, w_up, w_down, ln2_g, ln2_b):
    batch, seq, d = x.shape
    n = batch * seq
    assert d == D_MODEL and seq % ROW_TILE == 0 and w_in.shape[0] == DEPTH == 1
    row = lambda v: v.reshape(1, -1).astype(F32)
    x2d = x.reshape(n, d)
    g0, b0 = row(ln_in_g), row(ln_in_b)

    win = w_in[0]
    kcol, vcol, ccol = D_MODEL, D_MODEL + KV_WIDTH, D_MODEL + 2 * KV_WIDTH
    wkv_dup = jnp.concatenate([_dup_heads(win[:, kcol:vcol]), _dup_heads(win[:, vcol:ccol])], axis=1).astype(BF16)
    gap = jnp.zeros((d + 1, EXPERT_ROW0 - N_GROUPS), F32)
    tail = jnp.zeros((d + 1, ROUTE_ROWS - EXPERT_ROW0 - N_EXPERTS), F32)
    group_wb = jnp.concatenate([router_group_w[0], router_group_b[0][None, :]], axis=0)
    expert_wb = jnp.concatenate([router_expert_w[0], router_expert_b[0][None, :]], axis=0)
    rwb = jnp.concatenate([group_wb, gap, expert_wb, tail], axis=1).T
    rw = rwb[:, :d]
    rb = jnp.broadcast_to(rwb[:, d:], (ROUTE_ROWS, LANE))

    hb, q, kd, vd, wall = _ln_qkv(x2d, g0, b0, win, wkv_dup, w_branch_attn[0], w_branch_conv[0], w_out[0])
    kvm, um = _meta_prep(meta_tokens.astype(F32), g0, b0, wkv_dup, wall)
    h1, logits_t, wgu = _mixer_tail(sinks[0].astype(F32), x2d, hb, q, kd, vd, kvm, g0, b0, um, conv_w[0], wall,
                                    row(ln1_g[0]), row(ln1_b[0]), rw.astype(BF16), rb, w_gate[0], w_up[0], seq)
    _, top_w, pos, counts = _route(logits_t)
    xs = _dispatch(pos, h1)
    y = _moe_experts(_work_items(counts[:, 0]), xs, wgu, w_down[0])
    out = _combine_ln(pos, h1, top_w.T, row(ln2_g[0]), row(ln2_b[0]), y)
    return out.reshape(batch, seq, d)
```

```python
import functools

import jax
import jax.numpy as jnp
from jax import lax
from jax.experimental import pallas as pl
from jax.experimental.pallas import tpu as pltpu

D_MODEL = 2048
N_META = 16
HEAD_DIM = 64
N_Q_HEADS = 32
N_KV_HEADS = 4
BLOCK = 128
KV_WIDTH = N_KV_HEADS * HEAD_DIM
CONV_K = 3
N_GROUPS = 4
EXPERTS_PER_GROUP = 8
N_EXPERTS = N_GROUPS * EXPERTS_PER_GROUP
D_EXPERT = D_MODEL // 4
LN_EPS = 1e-5
DEPTH = 1
ALPHA = (2.0 * DEPTH) ** 0.25
NEG_INF = -1e30
Q_SCALE = HEAD_DIM ** -0.5

V7X_VMEM_LIMIT = 56 * 1024 * 1024
V7X_VMEM_LIMIT_TAIL = 58 * 1024 * 1024
LANE = 128
CHUNK = 512
N_CHUNKS = D_MODEL // CHUNK
ROW_TILE = 512
MOE_TILE = 256
DISPATCH_TILE = 1024
COMBINE_TILE = 256
MAX_ITEMS = 2 * 8192 // MOE_TILE + N_EXPERTS
ROUTE_ROWS = 128
EXPERT_ROW0 = 8

BF16 = jnp.bfloat16
F32 = jnp.float32


def _layer_norm(x, g, b):
    mu = jnp.mean(x, axis=-1, keepdims=True)
    xc = x - mu
    var = jnp.mean(xc * xc, axis=-1, keepdims=True)
    return xc * lax.rsqrt(var + LN_EPS) * g + b


def _dot(a, b):
    return jnp.dot(a, b, preferred_element_type=F32)


def _dup_head_lanes(t):
    heads = t.shape[1] // HEAD_DIM
    return jnp.concatenate([t[:, (c // 2) * HEAD_DIM:(c // 2 + 1) * HEAD_DIM] for c in range(2 * heads)], axis=1)


def _dot_nt(a, b):
    return lax.dot_general(a, b, (((1,), (1,)), ((), ())), preferred_element_type=F32)


def _params(*sem, vmem_limit=V7X_VMEM_LIMIT):
    return pltpu.CompilerParams(dimension_semantics=sem, vmem_limit_bytes=vmem_limit)


W_B, W_C, W_H, W_GA, W_GC, W_BA, W_BC, W_OUT = (k * N_CHUNKS for k in range(8))
N_WBLOCKS = 8 * N_CHUNKS
IN_FIRST_MIX = (D_MODEL + 2 * KV_WIDTH) // CHUNK


def _meta_kernel(meta_ref, g_ref, b_ref, wkv_ref, wcc_ref, wch_ref, kv_ref, u_ref):
    hm = _layer_norm(meta_ref[...], g_ref[...], b_ref[...]).astype(BF16)
    kv_ref[...] = _dup_head_lanes(_dot(hm, wkv_ref[...])).astype(BF16)
    u_ref[...] = _dot(hm, wcc_ref[...]) * _dot(hm, wch_ref[...])


def _meta_prep(meta, g, b, wkv, wall):
    n_kv = wkv.shape[1] // N_CHUNKS
    return pl.pallas_call(
        _meta_kernel,
        grid=(N_CHUNKS,),
        in_specs=[
            pl.BlockSpec((N_META, D_MODEL), lambda j: (0, 0)),
            pl.BlockSpec((1, D_MODEL), lambda j: (0, 0)),
            pl.BlockSpec((1, D_MODEL), lambda j: (0, 0)),
            pl.BlockSpec((D_MODEL, n_kv), lambda j: (0, j)),
            pl.BlockSpec((D_MODEL, CHUNK), lambda j: (0, W_C + j)),
            pl.BlockSpec((D_MODEL, CHUNK), lambda j: (0, W_H + j)),
        ],
        out_specs=[
            pl.BlockSpec((N_META, 2 * n_kv), lambda j: (0, j)),
            pl.BlockSpec((N_META, CHUNK), lambda j: (0, j)),
        ],
        out_shape=[
            jax.ShapeDtypeStruct((N_META, 2 * wkv.shape[1]), BF16),
            jax.ShapeDtypeStruct((N_META, D_MODEL), F32),
        ],
        compiler_params=_params("arbitrary"),
        name="meta_prep",
    )(meta, g, b, wkv, wall, wall)


def _qkv_kernel(x_ref, g_ref, b_ref, wq_ref, wkv_ref, win_hbm, ba_hbm, bc_hbm, out_hbm,
                hb_ref, q_ref, k_ref, v_ref, wall_ref, wq_bf, stage, sem):
    i = pl.program_id(0)
    kw = 2 * KV_WIDTH

    def start_block(j, slot):
        def copy(src_hbm, block):
            col = pl.multiple_of(block * CHUNK, CHUNK)
            pltpu.make_async_copy(src_hbm.at[:, pl.ds(col, CHUNK)], stage.at[slot], sem.at[slot]).start()

        @pl.when(j < W_BA)
        def _():
            copy(win_hbm, IN_FIRST_MIX + j)

        for src_hbm, first in ((ba_hbm, W_BA), (bc_hbm, W_BC), (out_hbm, W_OUT)):
            @pl.when((j >= first) & (j < first + N_CHUNKS))
            def _(src_hbm=src_hbm, first=first):
                copy(src_hbm, j - first)

    @pl.when(i == 0)
    def _():
        start_block(0, 0)
        wq_bf[...] = wq_ref[...].astype(BF16)

    slot = i % 2
    pltpu.make_async_copy(ba_hbm.at[:, pl.ds(0, CHUNK)], stage.at[slot], sem.at[slot]).wait()

    @pl.when(i + 1 < N_WBLOCKS)
    def _():
        start_block(i + 1, 1 - slot)

    wall_ref[...] = stage[slot].astype(BF16)
    hb = _layer_norm(x_ref[...], g_ref[...], b_ref[...]).astype(BF16)
    hb_ref[...] = hb
    q_ref[...] = (_dot(hb, wq_bf[...]) * Q_SCALE).astype(BF16)
    kv = _dot(hb, wkv_ref[...])
    k_ref[...] = _dup_head_lanes(kv[:, :KV_WIDTH]).astype(BF16)
    v_ref[...] = _dup_head_lanes(kv[:, KV_WIDTH:]).astype(BF16)


def _ln_qkv(x2d, g, b, win, wkv, wba, wbc, wout):
    n = x2d.shape[0]
    tile = n // N_WBLOCKS
    assert tile * N_WBLOCKS == n and tile % 16 == 0
    kw = 2 * KV_WIDTH
    row = lambda i: (i, 0)
    const = lambda i: (0, 0)
    once = dict(pipeline_mode=pl.Buffered(1))
    any_spec = pl.BlockSpec(memory_space=pl.ANY)
    return pl.pallas_call(
        _qkv_kernel,
        grid=(N_WBLOCKS,),
        in_specs=[
            pl.BlockSpec((tile, D_MODEL), row),
            pl.BlockSpec((1, D_MODEL), const),
            pl.BlockSpec((1, D_MODEL), const),
            pl.BlockSpec((D_MODEL, D_MODEL), const, **once),
            pl.BlockSpec(wkv.shape, const, **once),
            any_spec, any_spec, any_spec, any_spec,
        ],
        out_specs=[
            pl.BlockSpec((tile, D_MODEL), row),
            pl.BlockSpec((tile, D_MODEL), row),
            pl.BlockSpec((tile, kw), row),
            pl.BlockSpec((tile, kw), row),
            pl.BlockSpec((D_MODEL, CHUNK), lambda i: (0, i)),
        ],
        out_shape=[
            jax.ShapeDtypeStruct((n, D_MODEL), BF16),
            jax.ShapeDtypeStruct((n, D_MODEL), BF16),
            jax.ShapeDtypeStruct((n, kw), BF16),
            jax.ShapeDtypeStruct((n, kw), BF16),
            jax.ShapeDtypeStruct((D_MODEL, N_WBLOCKS * CHUNK), BF16),
        ],
        scratch_shapes=[
            pltpu.VMEM((D_MODEL, D_MODEL), BF16),
            pltpu.VMEM((2, D_MODEL, CHUNK), F32),
            pltpu.SemaphoreType.DMA((2,)),
        ],
        compiler_params=_params("arbitrary"),
        name="ln_qkv",
    )(x2d, g, b, win, wkv, win, wba, wbc, wout)


PAIRS = N_Q_HEADS // N_KV_HEADS // 2
QROWS = PAIRS * BLOCK


class _AttnBlock:
    def __init__(self, sink_ref, q_ref, kp_ref, kc_ref, vp_ref, vc_ref, kvm_ref, has_prev):
        self.sink_ref, self.q_ref, self.kvm_ref, self.has_prev = sink_ref, q_ref, kvm_ref, has_prev
        self.k_refs, self.v_refs = (kp_ref, kc_ref), (vp_ref, vc_ref)
        self.lane = lax.broadcasted_iota(jnp.int32, (BLOCK, LANE), 1)
        qi = lax.broadcasted_iota(jnp.int32, (QROWS, BLOCK), 0) % BLOCK
        kj = lax.broadcasted_iota(jnp.int32, (QROWS, BLOCK), 1)
        self.from_prev = kj > qi
        self.is_meta = kj < N_META
        self.rowpair = lax.broadcasted_iota(jnp.int32, (QROWS, 1), 0) // BLOCK
        self.zpad = jnp.zeros((BLOCK - N_META, LANE), BF16)

    def _masked_rows(self, refs, meta_cols, g, half):
        keep = (self.lane >= HEAD_DIM) if half else (self.lane < HEAD_DIM)
        gs = slice(g * LANE, (g + 1) * LANE)
        blocks = [r[:, gs] for r in refs] + [jnp.concatenate([self.kvm_ref[:, meta_cols], self.zpad], axis=0)]
        return jnp.concatenate([jnp.where(keep, b, jnp.zeros((), BF16)) for b in blocks], axis=0)

    def scores(self, g, half):
        qs = jnp.concatenate([self.q_ref[:, (g * PAIRS + p) * LANE:(g * PAIRS + p + 1) * LANE]
                              for p in range(PAIRS)], axis=0)
        return _dot_nt(qs, self._masked_rows(self.k_refs, slice(g * LANE, (g + 1) * LANE), g, half))

    def probs(self, s, g, half):
        s_prev = jnp.where(self.has_prev, s[:, :BLOCK], NEG_INF)
        s_band = jnp.where(self.from_prev, s_prev, s[:, BLOCK:2 * BLOCK])
        s_meta = jnp.where(self.is_meta, s[:, 2 * BLOCK:], NEG_INF)
        sink = jnp.zeros((QROWS, 1), F32)
        for p in range(PAIRS):
            sink = jnp.where(self.rowpair == p, self.sink_ref[g * 2 * PAIRS + 2 * p + half], sink)
        m = jnp.maximum(jnp.max(jnp.maximum(s_band, s_meta), axis=-1, keepdims=True), sink)
        e_band = jnp.exp(s_band - m)
        e_meta = jnp.exp(s_meta - m)
        den = jnp.sum(e_band + e_meta, axis=-1, keepdims=True) + jnp.exp(sink - m)
        p_band = e_band / den
        return jnp.concatenate([jnp.where(self.from_prev, p_band, 0.0).astype(BF16),
                                jnp.where(self.from_prev, 0.0, p_band).astype(BF16),
                                (e_meta / den).astype(BF16)], axis=1)

    def values(self, probs, g, half):
        meta_cols = slice((N_KV_HEADS + g) * LANE, (N_KV_HEADS + g + 1) * LANE)
        return _dot(probs, self._masked_rows(self.v_refs, meta_cols, g, half))

    @staticmethod
    def store(o_ref, g, acc):
        for p in range(PAIRS):
            col = (g * PAIRS + p) * LANE
            o_ref[:, col:col + LANE] = acc[p * BLOCK:(p + 1) * BLOCK].astype(BF16)


TAIL_STEPS = 2 * N_CHUNKS + 1
HALF_ROWS = D_MODEL // 2


def _lane_concat(ref):
    return jnp.concatenate([ref[k] for k in range(ref.shape[0])], axis=1)


def _tail_kernel(sink_ref, x_ref, hb_ref, q_ref, kp_ref, kc_ref, vp_ref, vc_ref, kvm_ref,
                 g0_ref, b0_ref, um_ref, cw_ref, wa_ref, wb_ref, wc_ref, wd_ref,
                 g1_ref, b1_ref, rw_ref, rb_ref, wg_hbm, wu_hbm, h1_ref, lg_ref, wgu_ref,
                 uext_ref, carry_ref, z_ref, m_ref, attn_ref, stage_ref, wsem, *, tiles_per_seq):
    i = pl.program_id(0)
    s = pl.program_id(1)
    tm = ROW_TILE

    cast_step = i * (2 * N_CHUNKS) + s
    cast_slot = cast_step % 2

    def start_half(t, slot):
        mat, half = t // 2, t % 2
        rows = pl.ds(pl.multiple_of(half * HALF_ROWS, HALF_ROWS), HALF_ROWS)

        @pl.when(mat < N_EXPERTS)
        def _():
            pltpu.make_async_copy(wg_hbm.at[mat, rows], stage_ref.at[slot], wsem.at[slot]).start()

        @pl.when(mat >= N_EXPERTS)
        def _():
            pltpu.make_async_copy(wu_hbm.at[mat - N_EXPERTS, rows], stage_ref.at[slot], wsem.at[slot]).start()

    @pl.when((i == 0) & (s == 0))
    def _():
        carry_ref[...] = jnp.zeros_like(carry_ref)
        start_half(0, 0)

    @pl.when(s < 2 * N_CHUNKS)
    def _():
        pltpu.make_async_copy(wg_hbm.at[0, pl.ds(0, HALF_ROWS)], stage_ref.at[cast_slot], wsem.at[cast_slot]).wait()

        @pl.when(cast_step + 1 < pl.num_programs(0) * 2 * N_CHUNKS)
        def _():
            start_half(cast_step + 1, 1 - cast_slot)

    @pl.when(s < N_CHUNKS)
    def _():
        wgu_ref[0] = stage_ref[cast_slot].astype(BF16)
        hb = hb_ref[...]
        first = i % tiles_per_seq == 0
        attn_out = attn_ref.at[pl.ds(pl.multiple_of(s * BLOCK, BLOCK), BLOCK)]
        attn = _AttnBlock(sink_ref, q_ref, kp_ref, kc_ref, vp_ref, vc_ref, kvm_ref,
                          has_prev=(i % tiles_per_seq) * N_CHUNKS + s > 0)
        units = [(g, half) for g in range(N_KV_HEADS) for half in range(2)]
        half_cols = CHUNK // 2
        dots = [(w_ref, slice(h * half_cols, (h + 1) * half_cols))
                for h in range(2) for w_ref in (wa_ref, wb_ref, wc_ref)]

        def conv_half(cs, c_h, h_h, b_h):
            u = c_h * h_h
            uext_ref[0:8, cs] = jnp.where(first, um_ref[N_META - 8:N_META, cs], carry_ref[s, :, cs])
            uext_ref[8:tm + 8, cs] = u
            conv = (cw_ref[0:1, cs] * uext_ref[6:tm + 6, cs] + cw_ref[1:2, cs] * uext_ref[7:tm + 7, cs]
                    + cw_ref[2:3, cs] * u)
            carry_ref[s, :, cs] = u[tm - 8:tm, :]
            z_ref[s, :, cs] = (b_h * conv).astype(BF16)

        scores = attn.scores(*units[0])
        outs, acc = [], None
        for k, unit in enumerate(units):
            if k < len(dots):
                w_ref, cs = dots[k]
                outs.append(_dot(hb, w_ref[:, cs]))
            probs = attn.probs(scores, *unit)
            part = attn.values(probs, *unit)
            if k + 1 < len(units):
                scores = attn.scores(*units[k + 1])
            acc = part if unit[1] == 0 else acc + part
            if unit[1] == 1:
                attn.store(attn_out, unit[0], acc)
            if k % 3 == 2 and k < len(dots):
                conv_half(dots[k][1], *outs[k - 2:k + 1])

    @pl.when((s >= N_CHUNKS) & (s < 2 * N_CHUNKS))
    def _():
        wgu_ref[0] = stage_ref[cast_slot].astype(BF16)
        hb = hb_ref[...]
        conv_part = jax.nn.sigmoid(_dot(hb, wb_ref[...])) * _dot(_lane_concat(z_ref), wd_ref[...])
        gate_attn = jax.nn.sigmoid(_dot(hb, wa_ref[...]))
        m_ref[s - N_CHUNKS] = (gate_attn * _dot(attn_ref[...], wc_ref[...]) + conv_part).astype(BF16)

    @pl.when(s == 2 * N_CHUNKS)
    def _():
        h = _layer_norm(x_ref[...], g0_ref[...], b0_ref[...])
        m = _lane_concat(m_ref)
        for k, w_ref in enumerate((wa_ref, wb_ref, wc_ref, wd_ref)):
            cs = slice(k * CHUNK, (k + 1) * CHUNK)
            h1_ref[:, cs] = ALPHA * h[:, cs] + _dot(m, w_ref[...])
        h1 = _layer_norm(h1_ref[...], g1_ref[...], b1_ref[...])
        h1_ref[...] = h1
        lg_ref[...] = _dot_nt(rw_ref[...], h1.astype(BF16)) + rb_ref[...][:, 0:1]


def _mixer_tail(sinks, x2d, hb, q, kd, vd, kvm, g0, b0, um, cw, wall, g1, b1, rw, rb, wg, wu, seq):
    n = x2d.shape[0]
    nc = N_CHUNKS
    assert ROW_TILE == nc * BLOCK
    assert (n // ROW_TILE) * 2 * nc == 2 * 2 * N_EXPERTS
    cast_map = lambda i, s: ((i * 2 * nc + jnp.minimum(s, 2 * nc - 1)) // 2, (i * 2 * nc + jnp.minimum(s, 2 * nc - 1)) % 2, 0)
    any_spec = pl.BlockSpec(memory_space=pl.ANY)
    tiles_per_seq = seq // ROW_TILE
    kw = 2 * KV_WIDTH
    cur = lambda i, s: (i * nc + jnp.minimum(s, nc - 1), 0)
    prev = lambda i, s: (jnp.maximum(i * nc + jnp.minimum(s, nc - 1) - 1, (i // tiles_per_seq) * (seq // BLOCK)), 0)

    def slot(conv_first, merge_first, out_block, hold=False):
        def index(i, s):
            conv = conv_first if hold else conv_first + s
            return 0, jnp.where(s < nc, conv, jnp.where(s < 2 * nc, merge_first + s - nc, W_OUT + out_block))
        return pl.BlockSpec((D_MODEL, CHUNK), index)

    lo = lambda i, s: (0, jnp.minimum(s, nc - 1))
    const = lambda i, s: (0, 0)
    row = lambda i, s: (i, 0)
    return pl.pallas_call(
        functools.partial(_tail_kernel, tiles_per_seq=tiles_per_seq),
        grid=(n // ROW_TILE, TAIL_STEPS),
        in_specs=[
            pl.BlockSpec(memory_space=pltpu.SMEM),
            pl.BlockSpec((ROW_TILE, D_MODEL), row),
            pl.BlockSpec((ROW_TILE, D_MODEL), row),
            pl.BlockSpec((BLOCK, D_MODEL), cur),
            pl.BlockSpec((BLOCK, kw), prev),
            pl.BlockSpec((BLOCK, kw), cur),
            pl.BlockSpec((BLOCK, kw), prev),
            pl.BlockSpec((BLOCK, kw), cur),
            pl.BlockSpec((N_META, 2 * kw), const),
            pl.BlockSpec((1, D_MODEL), const),
            pl.BlockSpec((1, D_MODEL), const),
            pl.BlockSpec((N_META, CHUNK), lo),
            pl.BlockSpec((CONV_K, CHUNK), lo),
            slot(W_C, W_GA, 0),
            slot(W_H, W_GC, 1),
            slot(W_B, W_BA, 2),
            slot(W_BC, W_BC, 3, hold=True),
            pl.BlockSpec((1, D_MODEL), const),
            pl.BlockSpec((1, D_MODEL), const),
            pl.BlockSpec((ROUTE_ROWS, D_MODEL), const),
            pl.BlockSpec((ROUTE_ROWS, LANE), const),
            any_spec, any_spec,
        ],
        out_specs=[
            pl.BlockSpec((ROW_TILE, D_MODEL), row),
            pl.BlockSpec((ROUTE_ROWS, ROW_TILE), lambda i, s: (0, i)),
            pl.BlockSpec((1, HALF_ROWS, D_EXPERT), cast_map),
        ],
        out_shape=[
            jax.ShapeDtypeStruct((n, D_MODEL), F32),
            jax.ShapeDtypeStruct((ROUTE_ROWS, n), F32),
            jax.ShapeDtypeStruct((2 * N_EXPERTS, D_MODEL, D_EXPERT), BF16),
        ],
        scratch_shapes=[
            pltpu.VMEM((ROW_TILE + 8, CHUNK), F32),
            pltpu.VMEM((nc, 8, CHUNK), F32),
            pltpu.VMEM((nc, ROW_TILE, CHUNK), BF16),
            pltpu.VMEM((nc, ROW_TILE, CHUNK), BF16),
            pltpu.VMEM((ROW_TILE, D_MODEL), BF16),
            pltpu.VMEM((2, HALF_ROWS, D_EXPERT), F32),
            pltpu.SemaphoreType.DMA((2,)),
        ],
        compiler_params=_params("arbitrary", "arbitrary", vmem_limit=V7X_VMEM_LIMIT_TAIL),
        name="mixer_tail",
    )(sinks, x2d, hb, q, kd, kd, vd, vd, kvm, g0, b0, um, cw, wall, wall, wall, wall, g1, b1, rw, rb, wg, wu)


RCHUNK = 1024
PBLK = 256


def _route_kernel(lg_ref, e_ref, w_ref, pos_ref, cnt_ref, rank_ref):
    n = lg_ref.shape[1]
    epg = EXPERTS_PER_GROUP

    def route_chunk(c, carry):
        sl = pl.ds(pl.multiple_of(c * RCHUNK, RCHUNK), RCHUNK)
        gl = [lg_ref[gi:gi + 1, sl] for gi in range(N_GROUPS)]
        gmax = functools.reduce(jnp.maximum, gl)
        ge = [jnp.exp(v - gmax) for v in gl]
        gsum = functools.reduce(jnp.add, ge)
        gp = [v / gsum for v in ge]
        p_group = functools.reduce(jnp.maximum, gp)
        gsel = jnp.full_like(p_group, float(N_GROUPS - 1))
        for gi in range(N_GROUPS - 2, -1, -1):
            gsel = jnp.where(gp[gi] >= p_group, float(gi), gsel)
        e_in = lg_ref[EXPERT_ROW0 + (N_GROUPS - 1) * epg:EXPERT_ROW0 + N_GROUPS * epg, sl]
        for gi in range(N_GROUPS - 2, -1, -1):
            e_in = jnp.where(gsel == float(gi),
                             lg_ref[EXPERT_ROW0 + gi * epg:EXPERT_ROW0 + (gi + 1) * epg, sl], e_in)
        ee = jnp.exp(e_in - jnp.max(e_in, axis=0, keepdims=True))
        pe = ee / jnp.sum(ee, axis=0, keepdims=True)
        ridx = lax.broadcasted_iota(jnp.int32, pe.shape, 0).astype(F32)
        p0 = jnp.max(pe, axis=0, keepdims=True)
        i0 = jnp.min(jnp.where(pe >= p0, ridx, float(epg)), axis=0, keepdims=True)
        pe2 = jnp.where(ridx == i0, -1.0, pe)
        p1 = jnp.max(pe2, axis=0, keepdims=True)
        i1 = jnp.min(jnp.where(pe2 >= p1, ridx, float(epg)), axis=0, keepdims=True)
        den = p0 + p1
        e_ref[0:1, sl] = (gsel * epg + i0).astype(jnp.int32)
        e_ref[1:2, sl] = (gsel * epg + i1).astype(jnp.int32)
        w_ref[0:1, sl] = p0 / den * p_group
        w_ref[1:2, sl] = p1 / den * p_group
        return carry

    lax.fori_loop(0, n // RCHUNK, route_chunk, 0)

    eid = lax.broadcasted_iota(jnp.int32, (N_EXPERTS, PBLK), 0)
    before = jnp.where(lax.broadcasted_iota(jnp.int32, (PBLK, PBLK), 0)
                       < lax.broadcasted_iota(jnp.int32, (PBLK, PBLK), 1), 1.0, 0.0).astype(BF16)
    ones = jnp.ones((PBLK, LANE), BF16)
    nblk = n // PBLK

    def onehot(k, blk):
        sl = pl.ds(pl.multiple_of(blk * PBLK, PBLK), PBLK)
        return sl, eid == e_ref[pl.ds(k, 1), sl]

    run = jnp.zeros((N_EXPERTS, LANE), F32)
    for k in range(2):
        def count_block(blk, run, k=k):
            sl, hit = onehot(k, blk)
            hit_bf = jnp.where(hit, 1.0, 0.0).astype(BF16)
            prior = _dot(hit_bf, before) + jnp.concatenate([run] * (PBLK // LANE), axis=1)
            rank_ref[pl.ds(k, 1), sl] = jnp.sum(jnp.where(hit, prior, 0.0), axis=0, keepdims=True)
            return run + _dot(hit_bf, ones)
        run = lax.fori_loop(0, nblk, count_block, run)

    cnt_ref[...] = run.astype(jnp.int32)
    hi = jnp.floor(run * (1.0 / LANE))
    lo = run - hi * LANE
    below = jnp.where(lax.broadcasted_iota(jnp.int32, (N_EXPERTS, N_EXPERTS), 1)
                      < lax.broadcasted_iota(jnp.int32, (N_EXPERTS, N_EXPERTS), 0), 1.0, 0.0).astype(BF16)
    off = _dot(below, hi.astype(BF16)) * LANE + _dot(below, lo.astype(BF16))
    off = jnp.concatenate([off] * (PBLK // LANE), axis=1)

    for k in range(2):
        def place_block(blk, carry, k=k):
            sl, hit = onehot(k, blk)
            base = jnp.sum(jnp.where(hit, off, 0.0), axis=0, keepdims=True)
            pos_ref[pl.ds(k, 1), sl] = (base + rank_ref[pl.ds(k, 1), sl]).astype(jnp.int32)
            return carry
        lax.fori_loop(0, nblk, place_block, 0)


def _route(logits_t):
    n = logits_t.shape[1]
    return pl.pallas_call(
        _route_kernel,
        out_shape=[
            jax.ShapeDtypeStruct((2, n), jnp.int32),
            jax.ShapeDtypeStruct((2, n), F32),
            jax.ShapeDtypeStruct((2, n), jnp.int32),
            jax.ShapeDtypeStruct((N_EXPERTS, LANE), jnp.int32),
        ],
        scratch_shapes=[pltpu.VMEM((2, n), F32)],
        compiler_params=pltpu.CompilerParams(vmem_limit_bytes=V7X_VMEM_LIMIT),
        name="route",
    )(logits_t)


def _row_copy(src_ref, src_row, dst_ref, dst_row, sem):
    return pltpu.make_async_copy(src_ref.at[pl.ds(src_row, 1)], dst_ref.at[pl.ds(dst_row, 1)], sem)


def _dispatch_kernel(pos_ref, h_ref, xs_ref, sem):
    for r in range(DISPATCH_TILE):
        for k in range(2):
            _row_copy(h_ref, r, xs_ref, pos_ref[k, r], sem).start(priority=k)
    for k in range(2):
        pltpu.make_async_copy(h_ref, xs_ref.at[pl.ds(0, DISPATCH_TILE)], sem).wait()


def _dispatch(pos, h1):
    n = h1.shape[0]
    return pl.pallas_call(
        _dispatch_kernel,
        grid=(n // DISPATCH_TILE,),
        in_specs=[
            pl.BlockSpec((2, DISPATCH_TILE), lambda i: (0, i), memory_space=pltpu.SMEM),
            pl.BlockSpec((DISPATCH_TILE, D_MODEL), lambda i: (i, 0)),
        ],
        out_specs=pl.BlockSpec(memory_space=pl.ANY),
        out_shape=jax.ShapeDtypeStruct((2 * n, D_MODEL), F32),
        scratch_shapes=[pltpu.SemaphoreType.DMA],
        compiler_params=_params("arbitrary"),
        name="dispatch",
    )(pos, h1)


W_SLOTS = 2
W_LOOKAHEAD = W_SLOTS - 1
def _moe_kernel(tile_ref, exp_ref, lo_ref, first_ref, slot_ref, ahead_ref, head_ref, n_ref,
                xs_ref, wgu_hbm, wd_hbm, y_ref,
                wg_buf, wu_buf, wd_buf, ytile_ref, sem):
    i = pl.program_id(0)

    def weight_copies(expert, slot):
        return (pltpu.make_async_copy(wgu_hbm.at[expert], wg_buf.at[slot], sem.at[slot, 0]),
                pltpu.make_async_copy(wgu_hbm.at[N_EXPERTS + expert], wu_buf.at[slot], sem.at[slot, 1]),
                pltpu.make_async_copy(wd_hbm.at[expert], wd_buf.at[slot], sem.at[slot, 2]))

    @pl.when(i == 0)
    def _():
        ytile_ref[...] = jnp.zeros_like(ytile_ref)
        for k in range(W_LOOKAHEAD):
            @pl.when(head_ref[k] >= 0)
            def _(k=k):
                for copy in weight_copies(head_ref[k], k):
                    copy.start()

    valid = i < n_ref[0]

    @pl.when(valid & (first_ref[i] == 1))
    def _():
        slot = slot_ref[i]
        for copy in weight_copies(exp_ref[i], slot):
            copy.wait()

        @pl.when(ahead_ref[i] >= 0)
        def _():
            for copy in weight_copies(ahead_ref[i], (slot + W_LOOKAHEAD) % W_SLOTS):
                copy.start()

    @pl.when(valid)
    def _():
        slot = slot_ref[i]
        x = xs_ref[...].astype(BF16)
        g = _dot(x, wg_buf[slot])
        u = _dot(x, wu_buf[slot])
        a = (g * jax.nn.sigmoid(g) * u).astype(BF16)
        y = _dot(a, wd_buf[slot].astype(BF16))
        row = lax.broadcasted_iota(jnp.int32, (MOE_TILE, 1), 0)
        merged = jnp.where(row >= lo_ref[i], y, ytile_ref[...])
        ytile_ref[...] = merged
        y_ref[...] = merged


def _moe_experts(items, xs, wgu, wd):
    rows = xs.shape[0]
    tile_map = lambda i, t, *_: (t[i], 0)
    grid_spec = pltpu.PrefetchScalarGridSpec(
        num_scalar_prefetch=len(items),
        grid=(MAX_ITEMS,),
        in_specs=[
            pl.BlockSpec((MOE_TILE, D_MODEL), tile_map),
            pl.BlockSpec(memory_space=pl.ANY),
            pl.BlockSpec(memory_space=pl.ANY),
        ],
        out_specs=pl.BlockSpec((MOE_TILE, D_MODEL), tile_map),
        scratch_shapes=[
            pltpu.VMEM((W_SLOTS, D_MODEL, D_EXPERT), BF16),
            pltpu.VMEM((W_SLOTS, D_MODEL, D_EXPERT), BF16),
            pltpu.VMEM((W_SLOTS, D_EXPERT, D_MODEL), F32),
            pltpu.VMEM((MOE_TILE, D_MODEL), F32),
            pltpu.SemaphoreType.DMA((W_SLOTS, 3)),
        ],
    )
    return pl.pallas_call(
        _moe_kernel,
        grid_spec=grid_spec,
        out_shape=jax.ShapeDtypeStruct((rows, D_MODEL), F32),
        compiler_params=_params("arbitrary"),
        name="moe_experts",
    )(*items, xs, wgu, wd)


def _work_items(counts):
    i32 = jnp.int32
    off = jnp.cumsum(counts) - counts
    end = off + counts
    first_tile = off // MOE_TILE
    n_e = jnp.where(counts > 0, (end - 1) // MOE_TILE - first_tile + 1, 0)
    item_end = jnp.cumsum(n_e)
    n_items = item_end[-1]
    idx = jnp.minimum(jnp.arange(MAX_ITEMS, dtype=i32), n_items - 1)
    exp = jnp.sum((item_end[None, :] <= idx[:, None]).astype(i32), axis=1)
    eid = jnp.arange(N_EXPERTS, dtype=i32)
    is_exp = exp[:, None] == eid[None, :]
    at_exp = lambda table: jnp.sum(jnp.where(is_exp, table[None, :], 0), axis=1)
    item0 = at_exp(item_end - n_e)
    tile = at_exp(first_tile) + idx - item0
    lo = jnp.maximum(at_exp(off) - tile * MOE_TILE, 0)
    present = counts > 0
    ordinal = jnp.cumsum(present.astype(i32)) - 1
    is_kth = present[None, :] & (ordinal[None, :] == eid[:, None])
    kth = jnp.where(jnp.any(is_kth, axis=1), jnp.sum(jnp.where(is_kth, eid[None, :], 0), axis=1), -1)
    item_ord = at_exp(ordinal)
    ahead = jnp.sum(jnp.where(item_ord[:, None] + W_LOOKAHEAD == eid[None, :], kth[None, :], 0), axis=1)
    ahead = jnp.where(item_ord + W_LOOKAHEAD < N_EXPERTS, ahead, -1)
    items = (tile, exp, lo, idx == item0, item_ord % W_SLOTS, ahead, kth[:W_LOOKAHEAD], n_items.reshape(1))
    return tuple(v.astype(i32) for v in items)


def _combine_kernel(pos_ref, pos_next_ref, h_ref, wt_ref, g_ref, b_ref, y_ref, o_ref, buf_ref, sem):
    i = pl.program_id(0)
    slot = i % 2

    def gather(tile_pos_ref, into):
        for r in range(COMBINE_TILE):
            for k in range(2):
                _row_copy(y_ref, tile_pos_ref[k, r], buf_ref.at[into, k], r, sem.at[into]).start(priority=k)

    @pl.when(i == 0)
    def _():
        gather(pos_ref, 0)

    for other in range(2):
        @pl.when((i + 1 < pl.num_programs(0)) & (slot == 1 - other))
        def _(other=other):
            gather(pos_next_ref, other)

    for k in range(2):
        pltpu.make_async_copy(y_ref.at[pl.ds(0, COMBINE_TILE)], buf_ref.at[slot, k], sem.at[slot]).wait()
    f = wt_ref[:, 0:1] * buf_ref[slot, 0] + wt_ref[:, 1:2] * buf_ref[slot, 1]
    o_ref[...] = _layer_norm(ALPHA * h_ref[...] + f, g_ref[...], b_ref[...])


def _combine_ln(pos, h1, wt, g, b, y):
    n = h1.shape[0]
    steps = n // COMBINE_TILE
    return pl.pallas_call(
        _combine_kernel,
        grid=(steps,),
        in_specs=[
            pl.BlockSpec((2, COMBINE_TILE), lambda i: (0, i), memory_space=pltpu.SMEM),
            pl.BlockSpec((2, COMBINE_TILE), lambda i: (0, jnp.minimum(i + 1, steps - 1)), memory_space=pltpu.SMEM),
            pl.BlockSpec((COMBINE_TILE, D_MODEL), lambda i: (i, 0)),
            pl.BlockSpec((COMBINE_TILE, 2), lambda i: (i, 0)),
            pl.BlockSpec((1, D_MODEL), lambda i: (0, 0)),
            pl.BlockSpec((1, D_MODEL), lambda i: (0, 0)),
            pl.BlockSpec(memory_space=pl.ANY),
        ],
        out_specs=pl.BlockSpec((COMBINE_TILE, D_MODEL), lambda i: (i, 0)),
        out_shape=jax.ShapeDtypeStruct((n, D_MODEL), F32),
        scratch_shapes=[pltpu.VMEM((2, 2, COMBINE_TILE, D_MODEL), F32), pltpu.SemaphoreType.DMA((2,))],
        compiler_params=_params("arbitrary"),
        name="combine_ln",
    )(pos, pos, h1, wt, g, b, y)


def kernel(x, meta_tokens, ln_in_g, ln_in_b, w_in, sinks, conv_w, w_branch_attn, w_branch_conv, w_out,
           ln1_g, ln1_b, router_group_w, router_group_b, router_expert_w, router_expert_b,
           w_gate, w_up, w_down, ln2_g, ln2_b):
    batch, seq, d = x.shape
    n = batch * seq
    assert d == D_MODEL and seq % ROW_TILE == 0 and w_in.shape[0] == DEPTH == 1
    row = lambda v: v.reshape(1, -1).astype(F32)
    x2d = x.reshape(n, d)
    g0, b0 = row(ln_in_g), row(ln_in_b)

    win = w_in[0]
    wkv = win[:, D_MODEL:D_MODEL + 2 * KV_WIDTH].astype(BF16)
    gap = jnp.zeros((d + 1, EXPERT_ROW0 - N_GROUPS), F32)
    tail = jnp.zeros((d + 1, ROUTE_ROWS - EXPERT_ROW0 - N_EXPERTS), F32)
    group_wb = jnp.concatenate([router_group_w[0], router_group_b[0][None, :]], axis=0)
    expert_wb = jnp.concatenate([router_expert_w[0], router_expert_b[0][None, :]], axis=0)
    rwb = jnp.concatenate([group_wb, gap, expert_wb, tail], axis=1).T
    rw = rwb[:, :d]
    rb = jnp.broadcast_to(rwb[:, d:], (ROUTE_ROWS, LANE))

    hb, q, kd, vd, wall = _ln_qkv(x2d, g0, b0, win, wkv, w_branch_attn[0], w_branch_conv[0], w_out[0])
    kvm, um = _meta_prep(meta_tokens.astype(F32), g0, b0, wkv, wall)
    h1, logits_t, wgu = _mixer_tail(sinks[0].astype(F32), x2d, hb, q, kd, vd, kvm, g0, b0, um, conv_w[0], wall,
                                    row(ln1_g[0]), row(ln1_b[0]), rw.astype(BF16), rb, w_gate[0], w_up[0], seq)
    _, top_w, pos, counts = _route(logits_t)
    xs = _dispatch(pos, h1)
    y = _moe_experts(_work_items(counts[:, 0]), xs, wgu, w_down[0])
    out = _combine_ln(pos, h1, top_w.T, row(ln2_g[0]), row(ln2_b[0]), y)
    return out.reshape(batch, seq, d)
```

```python
import functools

import jax
import jax.numpy as jnp
from jax import lax
from jax.experimental import pallas as pl
from jax.experimental.pallas import tpu as pltpu

D_MODEL = 2048
N_META = 16
HEAD_DIM = 64
N_Q_HEADS = 32
N_KV_HEADS = 4
BLOCK = 128
KV_WIDTH = N_KV_HEADS * HEAD_DIM
CONV_K = 3
N_GROUPS = 4
EXPERTS_PER_GROUP = 8
N_EXPERTS = N_GROUPS * EXPERTS_PER_GROUP
D_EXPERT = D_MODEL // 4
LN_EPS = 1e-5
DEPTH = 1
ALPHA = (2.0 * DEPTH) ** 0.25
NEG_INF = -1e30
Q_SCALE = HEAD_DIM ** -0.5

V7X_VMEM_LIMIT = 56 * 1024 * 1024
V7X_VMEM_LIMIT_TAIL = 58 * 1024 * 1024
LANE = 128
CHUNK = 512
N_CHUNKS = D_MODEL // CHUNK
ROW_TILE = 512
MOE_TILE = 256
DISPATCH_TILE = 1024
COMBINE_TILE = 256
ROUTE_ROWS = 128
EXPERT_ROW0 = 8

BF16 = jnp.bfloat16
F32 = jnp.float32


def _layer_norm(x, g, b):
    mu = jnp.mean(x, axis=-1, keepdims=True)
    xc = x - mu
    var = jnp.mean(xc * xc, axis=-1, keepdims=True)
    return xc * lax.rsqrt(var + LN_EPS) * g + b


def _dot(a, b):
    return jnp.dot(a, b, preferred_element_type=F32)


def _dup_head_lanes(t):
    width = t.shape[1]
    src = lax.broadcasted_iota(jnp.int32, (width, 2 * width), 0)
    dst = lax.broadcasted_iota(jnp.int32, (width, 2 * width), 1)
    select = jnp.where(src == (dst // LANE) * HEAD_DIM + dst % HEAD_DIM, 1.0, 0.0).astype(BF16)
    return _dot(t.astype(BF16), select).astype(BF16)


def _dot_nt(a, b):
    return lax.dot_general(a, b, (((1,), (1,)), ((), ())), preferred_element_type=F32)


def _params(*sem, vmem_limit=V7X_VMEM_LIMIT):
    return pltpu.CompilerParams(dimension_semantics=sem, vmem_limit_bytes=vmem_limit)


W_B, W_C, W_H, W_GA, W_GC, W_BA, W_BC, W_OUT = (k * N_CHUNKS for k in range(8))
N_WBLOCKS = 8 * N_CHUNKS
IN_FIRST_MIX = (D_MODEL + 2 * KV_WIDTH) // CHUNK


def _meta_kernel(meta_ref, g_ref, b_ref, wkv_ref, wcc_ref, wch_ref, kv_ref, u_ref):
    hm = _layer_norm(meta_ref[...], g_ref[...], b_ref[...]).astype(BF16)
    kv_ref[...] = _dup_head_lanes(_dot(hm, wkv_ref[...]))
    u_ref[...] = _dot(hm, wcc_ref[...]) * _dot(hm, wch_ref[...])


def _meta_prep(meta, g, b, wkv, wall):
    n_kv = wkv.shape[1] // N_CHUNKS
    return pl.pallas_call(
        _meta_kernel,
        grid=(N_CHUNKS,),
        in_specs=[
            pl.BlockSpec((N_META, D_MODEL), lambda j: (0, 0)),
            pl.BlockSpec((1, D_MODEL), lambda j: (0, 0)),
            pl.BlockSpec((1, D_MODEL), lambda j: (0, 0)),
            pl.BlockSpec((D_MODEL, n_kv), lambda j: (0, j)),
            pl.BlockSpec((D_MODEL, CHUNK), lambda j: (0, W_C + j)),
            pl.BlockSpec((D_MODEL, CHUNK), lambda j: (0, W_H + j)),
        ],
        out_specs=[
            pl.BlockSpec((N_META, 2 * n_kv), lambda j: (0, j)),
            pl.BlockSpec((N_META, CHUNK), lambda j: (0, j)),
        ],
        out_shape=[
            jax.ShapeDtypeStruct((N_META, 2 * wkv.shape[1]), BF16),
            jax.ShapeDtypeStruct((N_META, D_MODEL), F32),
        ],
        compiler_params=_params("arbitrary"),
        name="meta_prep",
    )(meta, g, b, wkv, wall, wall)


def _qkv_kernel(x_ref, g_ref, b_ref, wq_ref, wkv_ref, win_hbm, ba_hbm, bc_hbm, out_hbm,
                hb_ref, q_ref, k_ref, v_ref, wall_ref, wq_bf, stage, sem):
    i = pl.program_id(0)

    def start_block(j, slot):
        def copy(src_hbm, block):
            col = pl.multiple_of(block * CHUNK, CHUNK)
            pltpu.make_async_copy(src_hbm.at[:, pl.ds(col, CHUNK)], stage.at[slot], sem.at[slot]).start()

        @pl.when(j < W_BA)
        def _():
            copy(win_hbm, IN_FIRST_MIX + j)

        for src_hbm, first in ((ba_hbm, W_BA), (bc_hbm, W_BC), (out_hbm, W_OUT)):
            @pl.when((j >= first) & (j < first + N_CHUNKS))
            def _(src_hbm=src_hbm, first=first):
                copy(src_hbm, j - first)

    @pl.when(i == 0)
    def _():
        start_block(0, 0)
        wq_bf[...] = wq_ref[...].astype(BF16)

    slot = i % 2
    pltpu.make_async_copy(ba_hbm.at[:, pl.ds(0, CHUNK)], stage.at[slot], sem.at[slot]).wait()

    @pl.when(i + 1 < N_WBLOCKS)
    def _():
        start_block(i + 1, 1 - slot)

    wall_ref[...] = stage[slot].astype(BF16)
    hb = _layer_norm(x_ref[...], g_ref[...], b_ref[...]).astype(BF16)
    hb_ref[...] = hb
    q_ref[...] = (_dot(hb, wq_bf[...]) * Q_SCALE).astype(BF16)
    kv = _dot(hb, wkv_ref[...])
    k_ref[...] = _dup_head_lanes(kv[:, :KV_WIDTH])
    v_ref[...] = _dup_head_lanes(kv[:, KV_WIDTH:])


def _ln_qkv(x2d, g, b, win, wkv, wba, wbc, wout):
    n = x2d.shape[0]
    tile = n // N_WBLOCKS
    assert tile * N_WBLOCKS == n and tile % 16 == 0
    kw = 2 * KV_WIDTH
    row = lambda i: (i, 0)
    const = lambda i: (0, 0)
    once = dict(pipeline_mode=pl.Buffered(1))
    any_spec = pl.BlockSpec(memory_space=pl.ANY)
    return pl.pallas_call(
        _qkv_kernel,
        grid=(N_WBLOCKS,),
        in_specs=[
            pl.BlockSpec((tile, D_MODEL), row),
            pl.BlockSpec((1, D_MODEL), const),
            pl.BlockSpec((1, D_MODEL), const),
            pl.BlockSpec((D_MODEL, D_MODEL), const, **once),
            pl.BlockSpec(wkv.shape, const, **once),
            any_spec, any_spec, any_spec, any_spec,
        ],
        out_specs=[
            pl.BlockSpec((tile, D_MODEL), row),
            pl.BlockSpec((tile, D_MODEL), row),
            pl.BlockSpec((tile, kw), row),
            pl.BlockSpec((tile, kw), row),
            pl.BlockSpec((D_MODEL, CHUNK), lambda i: (0, i)),
        ],
        out_shape=[
            jax.ShapeDtypeStruct((n, D_MODEL), BF16),
            jax.ShapeDtypeStruct((n, D_MODEL), BF16),
            jax.ShapeDtypeStruct((n, kw), BF16),
            jax.ShapeDtypeStruct((n, kw), BF16),
            jax.ShapeDtypeStruct((D_MODEL, N_WBLOCKS * CHUNK), BF16),
        ],
        scratch_shapes=[
            pltpu.VMEM((D_MODEL, D_MODEL), BF16),
            pltpu.VMEM((2, D_MODEL, CHUNK), F32),
            pltpu.SemaphoreType.DMA((2,)),
        ],
        compiler_params=_params("arbitrary"),
        name="ln_qkv",
    )(x2d, g, b, win, wkv, win, wba, wbc, wout)


PAIRS = N_Q_HEADS // N_KV_HEADS // 2
QROWS = PAIRS * BLOCK


class _AttnBlock:
    def __init__(self, sink_ref, q_ref, kp_ref, kc_ref, vp_ref, vc_ref, kvm_ref, has_prev):
        self.sink_ref, self.q_ref, self.kvm_ref, self.has_prev = sink_ref, q_ref, kvm_ref, has_prev
        self.k_refs, self.v_refs = (kp_ref, kc_ref), (vp_ref, vc_ref)
        self.lane = lax.broadcasted_iota(jnp.int32, (BLOCK, LANE), 1)
        qi = lax.broadcasted_iota(jnp.int32, (QROWS, BLOCK), 0) % BLOCK
        kj = lax.broadcasted_iota(jnp.int32, (QROWS, BLOCK), 1)
        self.from_prev = kj > qi
        self.is_meta = kj < N_META
        self.rowpair = lax.broadcasted_iota(jnp.int32, (QROWS, 1), 0) // BLOCK
        self.zpad = jnp.zeros((BLOCK - N_META, LANE), BF16)

    def _masked_rows(self, refs, meta_cols, g, half):
        keep = (self.lane >= HEAD_DIM) if half else (self.lane < HEAD_DIM)
        gs = slice(g * LANE, (g + 1) * LANE)
        blocks = [r[:, gs] for r in refs] + [jnp.concatenate([self.kvm_ref[:, meta_cols], self.zpad], axis=0)]
        return jnp.concatenate([jnp.where(keep, b, jnp.zeros((), BF16)) for b in blocks], axis=0)

    def scores(self, g, half):
        qs = jnp.concatenate([self.q_ref[:, (g * PAIRS + p) * LANE:(g * PAIRS + p + 1) * LANE]
                              for p in range(PAIRS)], axis=0)
        return _dot_nt(qs, self._masked_rows(self.k_refs, slice(g * LANE, (g + 1) * LANE), g, half))

    def probs(self, s, g, half):
        s_prev = jnp.where(self.has_prev, s[:, :BLOCK], NEG_INF)
        s_band = jnp.where(self.from_prev, s_prev, s[:, BLOCK:2 * BLOCK])
        s_meta = jnp.where(self.is_meta, s[:, 2 * BLOCK:], NEG_INF)
        sink = jnp.zeros((QROWS, 1), F32)
        for p in range(PAIRS):
            sink = jnp.where(self.rowpair == p, self.sink_ref[g * 2 * PAIRS + 2 * p + half], sink)
        m = jnp.maximum(jnp.max(jnp.maximum(s_band, s_meta), axis=-1, keepdims=True), sink)
        e_band = jnp.exp(s_band - m)
        e_meta = jnp.exp(s_meta - m)
        den = jnp.sum(e_band + e_meta, axis=-1, keepdims=True) + jnp.exp(sink - m)
        p_band = e_band / den
        return jnp.concatenate([jnp.where(self.from_prev, p_band, 0.0).astype(BF16),
                                jnp.where(self.from_prev, 0.0, p_band).astype(BF16),
                                (e_meta / den).astype(BF16)], axis=1)

    def values(self, probs, g, half):
        meta_cols = slice((N_KV_HEADS + g) * LANE, (N_KV_HEADS + g + 1) * LANE)
        return _dot(probs, self._masked_rows(self.v_refs, meta_cols, g, half))

    @staticmethod
    def store(o_ref, g, acc):
        for p in range(PAIRS):
            col = (g * PAIRS + p) * LANE
            o_ref[:, col:col + LANE] = acc[p * BLOCK:(p + 1) * BLOCK].astype(BF16)


TAIL_STEPS = 2 * N_CHUNKS + 1
HALF_ROWS = D_MODEL // 2


def _lane_concat(ref):
    return jnp.concatenate([ref[k] for k in range(ref.shape[0])], axis=1)


def _tail_kernel(sink_ref, x_ref, hb_ref, q_ref, kp_ref, kc_ref, vp_ref, vc_ref, kvm_ref,
                 g0_ref, b0_ref, um_ref, cw_ref, wa_ref, wb_ref, wc_ref, wd_ref,
                 g1_ref, b1_ref, rw_ref, rb_ref, wg_hbm, wu_hbm, h1_ref, lg_ref, wgu_ref,
                 uext_ref, carry_ref, z_ref, m_ref, attn_ref, stage_ref, wsem, *, tiles_per_seq):
    i = pl.program_id(0)
    s = pl.program_id(1)
    tm = ROW_TILE

    cast_step = i * (2 * N_CHUNKS) + s
    cast_slot = cast_step % 2

    def start_half(t, slot):
        mat, half = t // 2, t % 2
        rows = pl.ds(pl.multiple_of(half * HALF_ROWS, HALF_ROWS), HALF_ROWS)

        @pl.when(mat < N_EXPERTS)
        def _():
            pltpu.make_async_copy(wg_hbm.at[mat, rows], stage_ref.at[slot], wsem.at[slot]).start()

        @pl.when(mat >= N_EXPERTS)
        def _():
            pltpu.make_async_copy(wu_hbm.at[mat - N_EXPERTS, rows], stage_ref.at[slot], wsem.at[slot]).start()

    @pl.when((i == 0) & (s == 0))
    def _():
        carry_ref[...] = jnp.zeros_like(carry_ref)
        start_half(0, 0)

    @pl.when(s < 2 * N_CHUNKS)
    def _():
        pltpu.make_async_copy(wg_hbm.at[0, pl.ds(0, HALF_ROWS)], stage_ref.at[cast_slot], wsem.at[cast_slot]).wait()

        @pl.when(cast_step + 1 < pl.num_programs(0) * 2 * N_CHUNKS)
        def _():
            start_half(cast_step + 1, 1 - cast_slot)

    @pl.when(s < N_CHUNKS)
    def _():
        wgu_ref[0] = stage_ref[cast_slot].astype(BF16)
        hb = hb_ref[...]
        first = i % tiles_per_seq == 0
        attn_out = attn_ref.at[pl.ds(pl.multiple_of(s * BLOCK, BLOCK), BLOCK)]
        attn = _AttnBlock(sink_ref, q_ref, kp_ref, kc_ref, vp_ref, vc_ref, kvm_ref,
                          has_prev=(i % tiles_per_seq) * N_CHUNKS + s > 0)
        units = [(g, half) for g in range(N_KV_HEADS) for half in range(2)]
        half_cols = CHUNK // 2
        dots = [(w_ref, slice(h * half_cols, (h + 1) * half_cols))
                for h in range(2) for w_ref in (wa_ref, wb_ref, wc_ref)]

        def conv_half(cs, c_h, h_h, b_h):
            u = c_h * h_h
            uext_ref[0:8, cs] = jnp.where(first, um_ref[N_META - 8:N_META, cs], carry_ref[s, :, cs])
            uext_ref[8:tm + 8, cs] = u
            conv = (cw_ref[0:1, cs] * uext_ref[6:tm + 6, cs] + cw_ref[1:2, cs] * uext_ref[7:tm + 7, cs]
                    + cw_ref[2:3, cs] * u)
            carry_ref[s, :, cs] = u[tm - 8:tm, :]
            z_ref[s, :, cs] = (b_h * conv).astype(BF16)

        scores = attn.scores(*units[0])
        outs, acc = [], None
        for k, unit in enumerate(units):
            if k < len(dots):
                w_ref, cs = dots[k]
                outs.append(_dot(hb, w_ref[:, cs]))
            probs = attn.probs(scores, *unit)
            part = attn.values(probs, *unit)
            if k + 1 < len(units):
                scores = attn.scores(*units[k + 1])
            acc = part if unit[1] == 0 else acc + part
            if unit[1] == 1:
                attn.store(attn_out, unit[0], acc)
            if k % 3 == 2 and k < len(dots):
                conv_half(dots[k][1], *outs[k - 2:k + 1])

    @pl.when((s >= N_CHUNKS) & (s < 2 * N_CHUNKS))
    def _():
        wgu_ref[0] = stage_ref[cast_slot].astype(BF16)
        hb = hb_ref[...]
        conv_part = jax.nn.sigmoid(_dot(hb, wb_ref[...])) * _dot(_lane_concat(z_ref), wd_ref[...])
        gate_attn = jax.nn.sigmoid(_dot(hb, wa_ref[...]))
        m_ref[s - N_CHUNKS] = (gate_attn * _dot(attn_ref[...], wc_ref[...]) + conv_part).astype(BF16)

    @pl.when(s == 2 * N_CHUNKS)
    def _():
        h = _layer_norm(x_ref[...], g0_ref[...], b0_ref[...])
        m = _lane_concat(m_ref)
        for k, w_ref in enumerate((wa_ref, wb_ref, wc_ref, wd_ref)):
            cs = slice(k * CHUNK, (k + 1) * CHUNK)
            h1_ref[:, cs] = ALPHA * h[:, cs] + _dot(m, w_ref[...])
        h1 = _layer_norm(h1_ref[...], g1_ref[...], b1_ref[...])
        h1_ref[...] = h1
        lg_ref[...] = _dot_nt(rw_ref[...], h1.astype(BF16)) + rb_ref[...][:, 0:1]


def _mixer_tail(sinks, x2d, hb, q, kd, vd, kvm, g0, b0, um, cw, wall, g1, b1, rw, rb, wg, wu, seq):
    n = x2d.shape[0]
    nc = N_CHUNKS
    assert ROW_TILE == nc * BLOCK
    assert (n // ROW_TILE) * 2 * nc == 2 * 2 * N_EXPERTS
    cast_map = lambda i, s: ((i * 2 * nc + jnp.minimum(s, 2 * nc - 1)) // 2, (i * 2 * nc + jnp.minimum(s, 2 * nc - 1)) % 2, 0)
    any_spec = pl.BlockSpec(memory_space=pl.ANY)
    tiles_per_seq = seq // ROW_TILE
    kw = 2 * KV_WIDTH
    cur = lambda i, s: (i * nc + jnp.minimum(s, nc - 1), 0)
    prev = lambda i, s: (jnp.maximum(i * nc + jnp.minimum(s, nc - 1) - 1, (i // tiles_per_seq) * (seq // BLOCK)), 0)

    def slot(conv_first, merge_first, out_block, hold=False):
        def index(i, s):
            conv = conv_first if hold else conv_first + s
            return 0, jnp.where(s < nc, conv, jnp.where(s < 2 * nc, merge_first + s - nc, W_OUT + out_block))
        return pl.BlockSpec((D_MODEL, CHUNK), index)

    lo = lambda i, s: (0, jnp.minimum(s, nc - 1))
    const = lambda i, s: (0, 0)
    row = lambda i, s: (i, 0)
    return pl.pallas_call(
        functools.partial(_tail_kernel, tiles_per_seq=tiles_per_seq),
        grid=(n // ROW_TILE, TAIL_STEPS),
        in_specs=[
            pl.BlockSpec(memory_space=pltpu.SMEM),
            pl.BlockSpec((ROW_TILE, D_MODEL), row),
            pl.BlockSpec((ROW_TILE, D_MODEL), row),
            pl.BlockSpec((BLOCK, D_MODEL), cur),
            pl.BlockSpec((BLOCK, kw), prev),
            pl.BlockSpec((BLOCK, kw), cur),
            pl.BlockSpec((BLOCK, kw), prev),
            pl.BlockSpec((BLOCK, kw), cur),
            pl.BlockSpec((N_META, 2 * kw), const),
            pl.BlockSpec((1, D_MODEL), const),
            pl.BlockSpec((1, D_MODEL), const),
            pl.BlockSpec((N_META, CHUNK), lo),
            pl.BlockSpec((CONV_K, CHUNK), lo),
            slot(W_C, W_GA, 0),
            slot(W_H, W_GC, 1),
            slot(W_B, W_BA, 2),
            slot(W_BC, W_BC, 3, hold=True),
            pl.BlockSpec((1, D_MODEL), const),
            pl.BlockSpec((1, D_MODEL), const),
            pl.BlockSpec((ROUTE_ROWS, D_MODEL), const),
            pl.BlockSpec((ROUTE_ROWS, LANE), const),
            any_spec, any_spec,
        ],
        out_specs=[
            pl.BlockSpec((ROW_TILE, D_MODEL), row),
            pl.BlockSpec((ROUTE_ROWS, ROW_TILE), lambda i, s: (0, i)),
            pl.BlockSpec((1, HALF_ROWS, D_EXPERT), cast_map),
        ],
        out_shape=[
            jax.ShapeDtypeStruct((n, D_MODEL), F32),
            jax.ShapeDtypeStruct((ROUTE_ROWS, n), F32),
            jax.ShapeDtypeStruct((2 * N_EXPERTS, D_MODEL, D_EXPERT), BF16),
        ],
        scratch_shapes=[
            pltpu.VMEM((ROW_TILE + 8, CHUNK), F32),
            pltpu.VMEM((nc, 8, CHUNK), F32),
            pltpu.VMEM((nc, ROW_TILE, CHUNK), BF16),
            pltpu.VMEM((nc, ROW_TILE, CHUNK), BF16),
            pltpu.VMEM((ROW_TILE, D_MODEL), BF16),
            pltpu.VMEM((2, HALF_ROWS, D_EXPERT), F32),
            pltpu.SemaphoreType.DMA((2,)),
        ],
        compiler_params=_params("arbitrary", "arbitrary", vmem_limit=V7X_VMEM_LIMIT_TAIL),
        name="mixer_tail",
    )(sinks, x2d, hb, q, kd, kd, vd, vd, kvm, g0, b0, um, cw, wall, wall, wall, wall, g1, b1, rw, rb, wg, wu)


RCHUNK = 1024
PBLK = 256


def _route_kernel(lg_ref, e_ref, w_ref, pos_ref, cnt_ref, rank_ref):
    n = lg_ref.shape[1]
    epg = EXPERTS_PER_GROUP

    def route_chunk(c, carry):
        sl = pl.ds(pl.multiple_of(c * RCHUNK, RCHUNK), RCHUNK)
        gl = [lg_ref[gi:gi + 1, sl] for gi in range(N_GROUPS)]
        gmax = functools.reduce(jnp.maximum, gl)
        ge = [jnp.exp(v - gmax) for v in gl]
        gsum = functools.reduce(jnp.add, ge)
        gp = [v / gsum for v in ge]
        p_group = functools.reduce(jnp.maximum, gp)
        gsel = jnp.full_like(p_group, float(N_GROUPS - 1))
        for gi in range(N_GROUPS - 2, -1, -1):
            gsel = jnp.where(gp[gi] >= p_group, float(gi), gsel)
        e_in = lg_ref[EXPERT_ROW0 + (N_GROUPS - 1) * epg:EXPERT_ROW0 + N_GROUPS * epg, sl]
        for gi in range(N_GROUPS - 2, -1, -1):
            e_in = jnp.where(gsel == float(gi),
                             lg_ref[EXPERT_ROW0 + gi * epg:EXPERT_ROW0 + (gi + 1) * epg, sl], e_in)
        ee = jnp.exp(e_in - jnp.max(e_in, axis=0, keepdims=True))
        pe = ee / jnp.sum(ee, axis=0, keepdims=True)
        ridx = lax.broadcasted_iota(jnp.int32, pe.shape, 0).astype(F32)
        p0 = jnp.max(pe, axis=0, keepdims=True)
        i0 = jnp.min(jnp.where(pe >= p0, ridx, float(epg)), axis=0, keepdims=True)
        pe2 = jnp.where(ridx == i0, -1.0, pe)
        p1 = jnp.max(pe2, axis=0, keepdims=True)
        i1 = jnp.min(jnp.where(pe2 >= p1, ridx, float(epg)), axis=0, keepdims=True)
        den = p0 + p1
        e_ref[0:1, sl] = (gsel * epg + i0).astype(jnp.int32)
        e_ref[1:2, sl] = (gsel * epg + i1).astype(jnp.int32)
        w_ref[0:1, sl] = p0 / den * p_group
        w_ref[1:2, sl] = p1 / den * p_group
        return carry

    lax.fori_loop(0, n // RCHUNK, route_chunk, 0)

    eid = lax.broadcasted_iota(jnp.int32, (N_EXPERTS, PBLK), 0)
    before = jnp.where(lax.broadcasted_iota(jnp.int32, (PBLK, PBLK), 0)
                       < lax.broadcasted_iota(jnp.int32, (PBLK, PBLK), 1), 1.0, 0.0).astype(BF16)
    ones = jnp.ones((PBLK, LANE), BF16)
    nblk = n // PBLK

    def onehot(k, blk):
        sl = pl.ds(pl.multiple_of(blk * PBLK, PBLK), PBLK)
        return sl, eid == e_ref[pl.ds(k, 1), sl]

    run = jnp.zeros((N_EXPERTS, LANE), F32)
    for k in range(2):
        def count_block(blk, run, k=k):
            sl, hit = onehot(k, blk)
            hit_bf = jnp.where(hit, 1.0, 0.0).astype(BF16)
            prior = _dot(hit_bf, before) + jnp.concatenate([run] * (PBLK // LANE), axis=1)
            rank_ref[pl.ds(k, 1), sl] = jnp.sum(jnp.where(hit, prior, 0.0), axis=0, keepdims=True)
            return run + _dot(hit_bf, ones)
        run = lax.fori_loop(0, nblk, count_block, run)

    cnt_ref[...] = run.astype(jnp.int32)
    hi = jnp.floor(run * (1.0 / LANE))
    lo = run - hi * LANE
    below = jnp.where(lax.broadcasted_iota(jnp.int32, (N_EXPERTS, N_EXPERTS), 1)
                      < lax.broadcasted_iota(jnp.int32, (N_EXPERTS, N_EXPERTS), 0), 1.0, 0.0).astype(BF16)
    off = _dot(below, hi.astype(BF16)) * LANE + _dot(below, lo.astype(BF16))
    off = jnp.concatenate([off] * (PBLK // LANE), axis=1)

    for k in range(2):
        def place_block(blk, carry, k=k):
            sl, hit = onehot(k, blk)
            base = jnp.sum(jnp.where(hit, off, 0.0), axis=0, keepdims=True)
            pos_ref[pl.ds(k, 1), sl] = (base + rank_ref[pl.ds(k, 1), sl]).astype(jnp.int32)
            return carry
        lax.fori_loop(0, nblk, place_block, 0)


def _route(logits_t):
    n = logits_t.shape[1]
    return pl.pallas_call(
        _route_kernel,
        out_shape=[
            jax.ShapeDtypeStruct((2, n), jnp.int32),
            jax.ShapeDtypeStruct((2, n), F32),
            jax.ShapeDtypeStruct((2, n), jnp.int32),
            jax.ShapeDtypeStruct((N_EXPERTS, LANE), jnp.int32),
        ],
        scratch_shapes=[pltpu.VMEM((2, n), F32)],
        compiler_params=pltpu.CompilerParams(vmem_limit_bytes=V7X_VMEM_LIMIT),
        name="route",
    )(logits_t)


def _row_copy(src_ref, src_row, dst_ref, dst_row, sem):
    return pltpu.make_async_copy(src_ref.at[pl.ds(src_row, 1)], dst_ref.at[pl.ds(dst_row, 1)], sem)


def _dispatch_kernel(pos_ref, h_ref, xs_ref, sem):
    for r in range(DISPATCH_TILE):
        for k in range(2):
            _row_copy(h_ref, r, xs_ref, pos_ref[k, r], sem).start(priority=k)
    for k in range(2):
        pltpu.make_async_copy(h_ref, xs_ref.at[pl.ds(0, DISPATCH_TILE)], sem).wait()


def _dispatch(pos, h1):
    n = h1.shape[0]
    return pl.pallas_call(
        _dispatch_kernel,
        grid=(n // DISPATCH_TILE,),
        in_specs=[
            pl.BlockSpec((2, DISPATCH_TILE), lambda i: (0, i), memory_space=pltpu.SMEM),
            pl.BlockSpec((DISPATCH_TILE, D_MODEL), lambda i: (i, 0)),
        ],
        out_specs=pl.BlockSpec(memory_space=pl.ANY),
        out_shape=jax.ShapeDtypeStruct((2 * n, D_MODEL), F32),
        scratch_shapes=[pltpu.SemaphoreType.DMA],
        compiler_params=_params("arbitrary"),
        name="dispatch",
    )(pos, h1)


W_SLOTS = 2
W_LOOKAHEAD = W_SLOTS - 1
def _moe_kernel(tile_ref, exp_ref, lo_ref, first_ref, slot_ref, ahead_ref, head_ref, n_ref,
                xs_ref, wgu_hbm, wd_hbm, y_ref,
                wg_buf, wu_buf, wd_buf, ytile_ref, sem):
    i = pl.program_id(0)

    def weight_copies(expert, slot):
        return (pltpu.make_async_copy(wgu_hbm.at[expert], wg_buf.at[slot], sem.at[slot, 0]),
                pltpu.make_async_copy(wgu_hbm.at[N_EXPERTS + expert], wu_buf.at[slot], sem.at[slot, 1]),
                pltpu.make_async_copy(wd_hbm.at[expert], wd_buf.at[slot], sem.at[slot, 2]))

    @pl.when(i == 0)
    def _():
        ytile_ref[...] = jnp.zeros_like(ytile_ref)
        for k in range(W_LOOKAHEAD):
            @pl.when(head_ref[k] >= 0)
            def _(k=k):
                for copy in weight_copies(head_ref[k], k):
                    copy.start()

    valid = i < n_ref[0]

    @pl.when(valid & (first_ref[i] == 1))
    def _():
        slot = slot_ref[i]
        for copy in weight_copies(exp_ref[i], slot):
            copy.wait()

        @pl.when(ahead_ref[i] >= 0)
        def _():
            for copy in weight_copies(ahead_ref[i], (slot + W_LOOKAHEAD) % W_SLOTS):
                copy.start()

    @pl.when(valid)
    def _():
        slot = slot_ref[i]
        x = xs_ref[...].astype(BF16)
        g = _dot(x, wg_buf[slot])
        u = _dot(x, wu_buf[slot])
        a = (g * jax.nn.sigmoid(g) * u).astype(BF16)
        y = _dot(a, wd_buf[slot].astype(BF16))
        row = lax.broadcasted_iota(jnp.int32, (MOE_TILE, 1), 0)
        merged = jnp.where(row >= lo_ref[i], y, ytile_ref[...])
        ytile_ref[...] = merged
        y_ref[...] = merged


def _max_items(rows):
    return rows // MOE_TILE + N_EXPERTS - 1


def _moe_experts(items, xs, wgu, wd):
    rows = xs.shape[0]
    tile_map = lambda i, t, *_: (t[i], 0)
    grid_spec = pltpu.PrefetchScalarGridSpec(
        num_scalar_prefetch=len(items),
        grid=(_max_items(rows),),
        in_specs=[
            pl.BlockSpec((MOE_TILE, D_MODEL), tile_map),
            pl.BlockSpec(memory_space=pl.ANY),
            pl.BlockSpec(memory_space=pl.ANY),
        ],
        out_specs=pl.BlockSpec((MOE_TILE, D_MODEL), tile_map),
        scratch_shapes=[
            pltpu.VMEM((W_SLOTS, D_MODEL, D_EXPERT), BF16),
            pltpu.VMEM((W_SLOTS, D_MODEL, D_EXPERT), BF16),
            pltpu.VMEM((W_SLOTS, D_EXPERT, D_MODEL), F32),
            pltpu.VMEM((MOE_TILE, D_MODEL), F32),
            pltpu.SemaphoreType.DMA((W_SLOTS, 3)),
        ],
    )
    return pl.pallas_call(
        _moe_kernel,
        grid_spec=grid_spec,
        out_shape=jax.ShapeDtypeStruct((rows, D_MODEL), F32),
        compiler_params=_params("arbitrary"),
        name="moe_experts",
    )(*items, xs, wgu, wd)


def _work_items(counts, rows):
    i32 = jnp.int32
    off = jnp.cumsum(counts) - counts
    end = off + counts
    first_tile = off // MOE_TILE
    n_e = jnp.where(counts > 0, (end - 1) // MOE_TILE - first_tile + 1, 0)
    item_end = jnp.cumsum(n_e)
    n_items = item_end[-1]
    idx = jnp.minimum(jnp.arange(_max_items(rows), dtype=i32), n_items - 1)
    exp = jnp.sum((item_end[None, :] <= idx[:, None]).astype(i32), axis=1)
    eid = jnp.arange(N_EXPERTS, dtype=i32)
    is_exp = exp[:, None] == eid[None, :]
    at_exp = lambda table: jnp.sum(jnp.where(is_exp, table[None, :], 0), axis=1)
    item0 = at_exp(item_end - n_e)
    tile = at_exp(first_tile) + idx - item0
    lo = jnp.maximum(at_exp(off) - tile * MOE_TILE, 0)
    present = counts > 0
    ordinal = jnp.cumsum(present.astype(i32)) - 1
    is_kth = present[None, :] & (ordinal[None, :] == eid[:, None])
    kth = jnp.where(jnp.any(is_kth, axis=1), jnp.sum(jnp.where(is_kth, eid[None, :], 0), axis=1), -1)
    item_ord = at_exp(ordinal)
    ahead = jnp.sum(jnp.where(item_ord[:, None] + W_LOOKAHEAD == eid[None, :], kth[None, :], 0), axis=1)
    ahead = jnp.where(item_ord + W_LOOKAHEAD < N_EXPERTS, ahead, -1)
    items = (tile, exp, lo, idx == item0, item_ord % W_SLOTS, ahead, kth[:W_LOOKAHEAD], n_items.reshape(1))
    return tuple(v.astype(i32) for v in items)


def _combine_kernel(pos_ref, pos_next_ref, h_ref, wt_ref, g_ref, b_ref, y_ref, o_ref, buf_ref, sem):
    i = pl.program_id(0)
    slot = i % 2

    def gather(tile_pos_ref, into):
        for r in range(COMBINE_TILE):
            for k in range(2):
                _row_copy(y_ref, tile_pos_ref[k, r], buf_ref.at[into, k], r, sem.at[into]).start(priority=k)

    @pl.when(i == 0)
    def _():
        gather(pos_ref, 0)

    for other in range(2):
        @pl.when((i + 1 < pl.num_programs(0)) & (slot == 1 - other))
        def _(other=other):
            gather(pos_next_ref, other)

    for k in range(2):
        pltpu.make_async_copy(y_ref.at[pl.ds(0, COMBINE_TILE)], buf_ref.at[slot, k], sem.at[slot]).wait()
    f = wt_ref[:, 0:1] * buf_ref[slot, 0] + wt_ref[:, 1:2] * buf_ref[slot, 1]
    o_ref[...] = _layer_norm(ALPHA * h_ref[...] + f, g_ref[...], b_ref[...])


def _combine_ln(pos, h1, wt, g, b, y):
    n = h1.shape[0]
    steps = n // COMBINE_TILE
    return pl.pallas_call(
        _combine_kernel,
        grid=(steps,),
        in_specs=[
            pl.BlockSpec((2, COMBINE_TILE), lambda i: (0, i), memory_space=pltpu.SMEM),
            pl.BlockSpec((2, COMBINE_TILE), lambda i: (0, jnp.minimum(i + 1, steps - 1)), memory_space=pltpu.SMEM),
            pl.BlockSpec((COMBINE_TILE, D_MODEL), lambda i: (i, 0)),
            pl.BlockSpec((COMBINE_TILE, 2), lambda i: (i, 0)),
            pl.BlockSpec((1, D_MODEL), lambda i: (0, 0)),
            pl.BlockSpec((1, D_MODEL), lambda i: (0, 0)),
            pl.BlockSpec(memory_space=pl.ANY),
        ],
        out_specs=pl.BlockSpec((COMBINE_TILE, D_MODEL), lambda i: (i, 0)),
        out_shape=jax.ShapeDtypeStruct((n, D_MODEL), F32),
        scratch_shapes=[pltpu.VMEM((2, 2, COMBINE_TILE, D_MODEL), F32), pltpu.SemaphoreType.DMA((2,))],
        compiler_params=_params("arbitrary"),
        name="combine_ln",
    )(pos, pos, h1, wt, g, b, y)


def kernel(x, meta_tokens, ln_in_g, ln_in_b, w_in, sinks, conv_w, w_branch_attn, w_branch_conv, w_out,
           ln1_g, ln1_b, router_group_w, router_group_b, router_expert_w, router_expert_b,
           w_gate, w_up, w_down, ln2_g, ln2_b):
    batch, seq, d = x.shape
    n = batch * seq
    assert d == D_MODEL and seq % ROW_TILE == 0 and w_in.shape[0] == DEPTH == 1
    row = lambda v: v.reshape(1, -1).astype(F32)
    x2d = x.reshape(n, d)
    g0, b0 = row(ln_in_g), row(ln_in_b)

    win = w_in[0]
    wkv = win[:, D_MODEL:D_MODEL + 2 * KV_WIDTH].astype(BF16)
    gap = jnp.zeros((d + 1, EXPERT_ROW0 - N_GROUPS), F32)
    tail = jnp.zeros((d + 1, ROUTE_ROWS - EXPERT_ROW0 - N_EXPERTS), F32)
    group_wb = jnp.concatenate([router_group_w[0], router_group_b[0][None, :]], axis=0)
    expert_wb = jnp.concatenate([router_expert_w[0], router_expert_b[0][None, :]], axis=0)
    rwb = jnp.concatenate([group_wb, gap, expert_wb, tail], axis=1).T
    rw = rwb[:, :d]
    rb = jnp.broadcast_to(rwb[:, d:], (ROUTE_ROWS, LANE))

    hb, q, kd, vd, wall = _ln_qkv(x2d, g0, b0, win, wkv, w_branch_attn[0], w_branch_conv[0], w_out[0])
    kvm, um = _meta_prep(meta_tokens.astype(F32), g0, b0, wkv, wall)
    h1, logits_t, wgu = _mixer_tail(sinks[0].astype(F32), x2d, hb, q, kd, vd, kvm, g0, b0, um, conv_w[0], wall,
                                    row(ln1_g[0]), row(ln1_b[0]), rw.astype(BF16), rb, w_gate[0], w_up[0], seq)
    _, top_w, pos, counts = _route(logits_t)
    xs = _dispatch(pos, h1)
    y = _moe_experts(_work_items(counts[:, 0], xs.shape[0]), xs, wgu, w_down[0])
    out = _combine_ln(pos, h1, top_w.T, row(ln2_g[0]), row(ln2_b[0]), y)
    return out.reshape(batch, seq, d)
```

```python
import functools

import jax
import jax.numpy as jnp
from jax import lax
from jax.experimental import pallas as pl
from jax.experimental.pallas import tpu as pltpu

D_MODEL = 2048
N_META = 16
HEAD_DIM = 64
N_Q_HEADS = 32
N_KV_HEADS = 4
BLOCK = 128
KV_WIDTH = N_KV_HEADS * HEAD_DIM
CONV_K = 3
N_GROUPS = 4
EXPERTS_PER_GROUP = 8
N_EXPERTS = N_GROUPS * EXPERTS_PER_GROUP
D_EXPERT = D_MODEL // 4
LN_EPS = 1e-5
DEPTH = 1
ALPHA = (2.0 * DEPTH) ** 0.25
NEG_INF = -1e30
Q_SCALE = HEAD_DIM ** -0.5

V7X_VMEM_LIMIT = 56 * 1024 * 1024
V7X_VMEM_LIMIT_TAIL = 58 * 1024 * 1024
LANE = 128
CHUNK = 512
N_CHUNKS = D_MODEL // CHUNK
ROW_TILE = 512
MOE_TILE = 256
DISPATCH_TILE = 1024
COMBINE_TILE = 256
ROUTE_ROWS = 128
EXPERT_ROW0 = 8

BF16 = jnp.bfloat16
F32 = jnp.float32


def _layer_norm(x, g, b):
    mu = jnp.mean(x, axis=-1, keepdims=True)
    xc = x - mu
    var = jnp.mean(xc * xc, axis=-1, keepdims=True)
    return xc * lax.rsqrt(var + LN_EPS) * g + b


def _dot(a, b):
    return jnp.dot(a, b, preferred_element_type=F32)


def _dup_head_lanes(t):
    width = t.shape[1]
    src = lax.broadcasted_iota(jnp.int32, (width, 2 * width), 0)
    dst = lax.broadcasted_iota(jnp.int32, (width, 2 * width), 1)
    select = jnp.where(src == (dst // LANE) * HEAD_DIM + dst % HEAD_DIM, 1.0, 0.0).astype(BF16)
    return _dot(t.astype(BF16), select).astype(BF16)


def _dot_nt(a, b):
    return lax.dot_general(a, b, (((1,), (1,)), ((), ())), preferred_element_type=F32)


def _params(*sem, vmem_limit=V7X_VMEM_LIMIT):
    return pltpu.CompilerParams(dimension_semantics=sem, vmem_limit_bytes=vmem_limit)


W_B, W_C, W_H, W_GA, W_GC, W_BA, W_BC, W_OUT = (k * N_CHUNKS for k in range(8))
N_WBLOCKS = 8 * N_CHUNKS
IN_FIRST_MIX = (D_MODEL + 2 * KV_WIDTH) // CHUNK


def _meta_kernel(meta_ref, g_ref, b_ref, wkv_ref, wcc_ref, wch_ref, kv_ref, u_ref):
    hm = _layer_norm(meta_ref[...], g_ref[...], b_ref[...]).astype(BF16)
    kv_ref[...] = _dup_head_lanes(_dot(hm, wkv_ref[...]))
    u_ref[...] = _dot(hm, wcc_ref[...]) * _dot(hm, wch_ref[...])


def _meta_prep(meta, g, b, wkv, wall):
    n_kv = wkv.shape[1] // N_CHUNKS
    return pl.pallas_call(
        _meta_kernel,
        grid=(N_CHUNKS,),
        in_specs=[
            pl.BlockSpec((N_META, D_MODEL), lambda j: (0, 0)),
            pl.BlockSpec((1, D_MODEL), lambda j: (0, 0)),
            pl.BlockSpec((1, D_MODEL), lambda j: (0, 0)),
            pl.BlockSpec((D_MODEL, n_kv), lambda j: (0, j)),
            pl.BlockSpec((D_MODEL, CHUNK), lambda j: (0, W_C + j)),
            pl.BlockSpec((D_MODEL, CHUNK), lambda j: (0, W_H + j)),
        ],
        out_specs=[
            pl.BlockSpec((N_META, 2 * n_kv), lambda j: (0, j)),
            pl.BlockSpec((N_META, CHUNK), lambda j: (0, j)),
        ],
        out_shape=[
            jax.ShapeDtypeStruct((N_META, 2 * wkv.shape[1]), BF16),
            jax.ShapeDtypeStruct((N_META, D_MODEL), F32),
        ],
        compiler_params=_params("arbitrary"),
        name="meta_prep",
    )(meta, g, b, wkv, wall, wall)


def _qkv_kernel(x_ref, g_ref, b_ref, wq_ref, wkv_ref, win_hbm, ba_hbm, bc_hbm, out_hbm,
                hb_ref, q_ref, k_ref, v_ref, wall_ref, wq_bf, stage, sem):
    i = pl.program_id(0)

    def start_block(j, slot):
        def copy(src_hbm, block):
            col = pl.multiple_of(block * CHUNK, CHUNK)
            pltpu.make_async_copy(src_hbm.at[:, pl.ds(col, CHUNK)], stage.at[slot], sem.at[slot]).start()

        @pl.when(j < W_BA)
        def _():
            copy(win_hbm, IN_FIRST_MIX + j)

        for src_hbm, first in ((ba_hbm, W_BA), (bc_hbm, W_BC), (out_hbm, W_OUT)):
            @pl.when((j >= first) & (j < first + N_CHUNKS))
            def _(src_hbm=src_hbm, first=first):
                copy(src_hbm, j - first)

    @pl.when(i == 0)
    def _():
        start_block(0, 0)
        wq_bf[...] = wq_ref[...].astype(BF16)

    slot = i % 2
    pltpu.make_async_copy(ba_hbm.at[:, pl.ds(0, CHUNK)], stage.at[slot], sem.at[slot]).wait()

    @pl.when(i + 1 < N_WBLOCKS)
    def _():
        start_block(i + 1, 1 - slot)

    wall_ref[...] = stage[slot].astype(BF16)
    hb = _layer_norm(x_ref[...], g_ref[...], b_ref[...]).astype(BF16)
    hb_ref[...] = hb
    q_ref[...] = (_dot(hb, wq_bf[...]) * Q_SCALE).astype(BF16)
    kv = _dot(hb, wkv_ref[...])
    k_ref[...] = _dup_head_lanes(kv[:, :KV_WIDTH])
    v_ref[...] = _dup_head_lanes(kv[:, KV_WIDTH:])


def _ln_qkv(x2d, g, b, win, wkv, wba, wbc, wout):
    n = x2d.shape[0]
    tile = n // N_WBLOCKS
    assert tile * N_WBLOCKS == n and tile % 16 == 0
    kw = 2 * KV_WIDTH
    row = lambda i: (i, 0)
    const = lambda i: (0, 0)
    once = dict(pipeline_mode=pl.Buffered(1))
    any_spec = pl.BlockSpec(memory_space=pl.ANY)
    return pl.pallas_call(
        _qkv_kernel,
        grid=(N_WBLOCKS,),
        in_specs=[
            pl.BlockSpec((tile, D_MODEL), row),
            pl.BlockSpec((1, D_MODEL), const),
            pl.BlockSpec((1, D_MODEL), const),
            pl.BlockSpec((D_MODEL, D_MODEL), const, **once),
            pl.BlockSpec(wkv.shape, const, **once),
            any_spec, any_spec, any_spec, any_spec,
        ],
        out_specs=[
            pl.BlockSpec((tile, D_MODEL), row),
            pl.BlockSpec((tile, D_MODEL), row),
            pl.BlockSpec((tile, kw), row),
            pl.BlockSpec((tile, kw), row),
            pl.BlockSpec((D_MODEL, CHUNK), lambda i: (0, i)),
        ],
        out_shape=[
            jax.ShapeDtypeStruct((n, D_MODEL), BF16),
            jax.ShapeDtypeStruct((n, D_MODEL), BF16),
            jax.ShapeDtypeStruct((n, kw), BF16),
            jax.ShapeDtypeStruct((n, kw), BF16),
            jax.ShapeDtypeStruct((D_MODEL, N_WBLOCKS * CHUNK), BF16),
        ],
        scratch_shapes=[
            pltpu.VMEM((D_MODEL, D_MODEL), BF16),
            pltpu.VMEM((2, D_MODEL, CHUNK), F32),
            pltpu.SemaphoreType.DMA((2,)),
        ],
        compiler_params=_params("arbitrary"),
        name="ln_qkv",
    )(x2d, g, b, win, wkv, win, wba, wbc, wout)


PAIRS = N_Q_HEADS // N_KV_HEADS // 2
QROWS = PAIRS * BLOCK


class _AttnBlock:
    def __init__(self, sink_ref, q_ref, kp_ref, kc_ref, vp_ref, vc_ref, kvm_ref, has_prev):
        self.sink_ref, self.q_ref, self.kvm_ref, self.has_prev = sink_ref, q_ref, kvm_ref, has_prev
        self.k_refs, self.v_refs = (kp_ref, kc_ref), (vp_ref, vc_ref)
        self.lane = lax.broadcasted_iota(jnp.int32, (BLOCK, LANE), 1)
        qi = lax.broadcasted_iota(jnp.int32, (QROWS, BLOCK), 0) % BLOCK
        kj = lax.broadcasted_iota(jnp.int32, (QROWS, BLOCK), 1)
        self.from_prev = kj > qi
        self.is_meta = kj < N_META
        self.rowpair = lax.broadcasted_iota(jnp.int32, (QROWS, 1), 0) // BLOCK
        self.zpad = jnp.zeros((BLOCK - N_META, LANE), BF16)

    def _masked_rows(self, refs, meta_cols, g, half):
        keep = (self.lane >= HEAD_DIM) if half else (self.lane < HEAD_DIM)
        gs = slice(g * LANE, (g + 1) * LANE)
        blocks = [r[:, gs] for r in refs] + [jnp.concatenate([self.kvm_ref[:, meta_cols], self.zpad], axis=0)]
        return jnp.concatenate([jnp.where(keep, b, jnp.zeros((), BF16)) for b in blocks], axis=0)

    def scores(self, g, half):
        qs = jnp.concatenate([self.q_ref[:, (g * PAIRS + p) * LANE:(g * PAIRS + p + 1) * LANE]
                              for p in range(PAIRS)], axis=0)
        return _dot_nt(qs, self._masked_rows(self.k_refs, slice(g * LANE, (g + 1) * LANE), g, half))

    def probs(self, s, g, half):
        s_prev = jnp.where(self.has_prev, s[:, :BLOCK], NEG_INF)
        s_band = jnp.where(self.from_prev, s_prev, s[:, BLOCK:2 * BLOCK])
        s_meta = jnp.where(self.is_meta, s[:, 2 * BLOCK:], NEG_INF)
        sink = jnp.zeros((QROWS, 1), F32)
        for p in range(PAIRS):
            sink = jnp.where(self.rowpair == p, self.sink_ref[g * 2 * PAIRS + 2 * p + half], sink)
        m = jnp.maximum(jnp.max(jnp.maximum(s_band, s_meta), axis=-1, keepdims=True), sink)
        e_band = jnp.exp(s_band - m)
        e_meta = jnp.exp(s_meta - m)
        den = jnp.sum(e_band + e_meta, axis=-1, keepdims=True) + jnp.exp(sink - m)
        p_band = e_band / den
        return jnp.concatenate([jnp.where(self.from_prev, p_band, 0.0).astype(BF16),
                                jnp.where(self.from_prev, 0.0, p_band).astype(BF16),
                                (e_meta / den).astype(BF16)], axis=1)

    def values(self, probs, g, half):
        meta_cols = slice((N_KV_HEADS + g) * LANE, (N_KV_HEADS + g + 1) * LANE)
        return _dot(probs, self._masked_rows(self.v_refs, meta_cols, g, half))

    @staticmethod
    def store(o_ref, g, acc):
        for p in range(PAIRS):
            col = (g * PAIRS + p) * LANE
            o_ref[:, col:col + LANE] = acc[p * BLOCK:(p + 1) * BLOCK].astype(BF16)


TAIL_STEPS = 2 * N_CHUNKS + 1
HALF_ROWS = D_MODEL // 2


def _lane_concat(ref):
    return jnp.concatenate([ref[k] for k in range(ref.shape[0])], axis=1)


def _tail_kernel(sink_ref, x_ref, hb_ref, q_ref, kp_ref, kc_ref, vp_ref, vc_ref, kvm_ref,
                 g0_ref, b0_ref, um_ref, cw_ref, wa_ref, wb_ref, wc_ref, wd_ref,
                 g1_ref, b1_ref, rw_ref, rb_ref, wg_hbm, wu_hbm, h1_ref, lg_ref, wgu_ref,
                 uext_ref, carry_ref, z_ref, m_ref, attn_ref, stage_ref, wsem, *, tiles_per_seq):
    i = pl.program_id(0)
    s = pl.program_id(1)
    tm = ROW_TILE

    cast_step = i * (2 * N_CHUNKS) + s
    cast_slot = cast_step % 2

    def start_half(t, slot):
        mat, half = t // 2, t % 2
        rows = pl.ds(pl.multiple_of(half * HALF_ROWS, HALF_ROWS), HALF_ROWS)

        @pl.when(mat < N_EXPERTS)
        def _():
            pltpu.make_async_copy(wg_hbm.at[mat, rows], stage_ref.at[slot], wsem.at[slot]).start()

        @pl.when(mat >= N_EXPERTS)
        def _():
            pltpu.make_async_copy(wu_hbm.at[mat - N_EXPERTS, rows], stage_ref.at[slot], wsem.at[slot]).start()

    @pl.when((i == 0) & (s == 0))
    def _():
        carry_ref[...] = jnp.zeros_like(carry_ref)
        start_half(0, 0)

    @pl.when(s < 2 * N_CHUNKS)
    def _():
        pltpu.make_async_copy(wg_hbm.at[0, pl.ds(0, HALF_ROWS)], stage_ref.at[cast_slot], wsem.at[cast_slot]).wait()

        @pl.when(cast_step + 1 < pl.num_programs(0) * 2 * N_CHUNKS)
        def _():
            start_half(cast_step + 1, 1 - cast_slot)

    @pl.when(s < N_CHUNKS)
    def _():
        wgu_ref[0] = stage_ref[cast_slot].astype(BF16)
        hb = hb_ref[...]
        first = i % tiles_per_seq == 0
        attn_out = attn_ref.at[pl.ds(pl.multiple_of(s * BLOCK, BLOCK), BLOCK)]
        attn = _AttnBlock(sink_ref, q_ref, kp_ref, kc_ref, vp_ref, vc_ref, kvm_ref,
                          has_prev=(i % tiles_per_seq) * N_CHUNKS + s > 0)
        units = [(g, half) for g in range(N_KV_HEADS) for half in range(2)]
        half_cols = CHUNK // 2
        dots = [(w_ref, slice(h * half_cols, (h + 1) * half_cols))
                for h in range(2) for w_ref in (wa_ref, wb_ref, wc_ref)]

        def conv_half(cs, c_h, h_h, b_h):
            u = c_h * h_h
            uext_ref[0:8, cs] = jnp.where(first, um_ref[N_META - 8:N_META, cs], carry_ref[s, :, cs])
            uext_ref[8:tm + 8, cs] = u
            conv = (cw_ref[0:1, cs] * uext_ref[6:tm + 6, cs] + cw_ref[1:2, cs] * uext_ref[7:tm + 7, cs]
                    + cw_ref[2:3, cs] * u)
            carry_ref[s, :, cs] = u[tm - 8:tm, :]
            z_ref[s, :, cs] = (b_h * conv).astype(BF16)

        scores = attn.scores(*units[0])
        outs, acc = [], None
        for k, unit in enumerate(units):
            if k < len(dots):
                w_ref, cs = dots[k]
                outs.append(_dot(hb, w_ref[:, cs]))
            probs = attn.probs(scores, *unit)
            part = attn.values(probs, *unit)
            if k + 1 < len(units):
                scores = attn.scores(*units[k + 1])
            acc = part if unit[1] == 0 else acc + part
            if unit[1] == 1:
                attn.store(attn_out, unit[0], acc)
            if k % 3 == 2 and k < len(dots):
                conv_half(dots[k][1], *outs[k - 2:k + 1])

    @pl.when((s >= N_CHUNKS) & (s < 2 * N_CHUNKS))
    def _():
        wgu_ref[0] = stage_ref[cast_slot].astype(BF16)
        hb = hb_ref[...]
        conv_part = jax.nn.sigmoid(_dot(hb, wb_ref[...])) * _dot(_lane_concat(z_ref), wd_ref[...])
        gate_attn = jax.nn.sigmoid(_dot(hb, wa_ref[...]))
        m_ref[s - N_CHUNKS] = (gate_attn * _dot(attn_ref[...], wc_ref[...]) + conv_part).astype(BF16)

    @pl.when(s == 2 * N_CHUNKS)
    def _():
        h = _layer_norm(x_ref[...], g0_ref[...], b0_ref[...])
        m = _lane_concat(m_ref)
        for k, w_ref in enumerate((wa_ref, wb_ref, wc_ref, wd_ref)):
            cs = slice(k * CHUNK, (k + 1) * CHUNK)
            h1_ref[:, cs] = ALPHA * h[:, cs] + _dot(m, w_ref[...])
        h1 = _layer_norm(h1_ref[...], g1_ref[...], b1_ref[...])
        h1_ref[...] = h1
        lg_ref[...] = _dot_nt(rw_ref[...], h1.astype(BF16)) + rb_ref[...][:, 0:1]


def _mixer_tail(sinks, x2d, hb, q, kd, vd, kvm, g0, b0, um, cw, wall, g1, b1, rw, rb, wg, wu, seq):
    n = x2d.shape[0]
    nc = N_CHUNKS
    assert ROW_TILE == nc * BLOCK
    assert (n // ROW_TILE) * 2 * nc == 2 * 2 * N_EXPERTS
    cast_map = lambda i, s: ((i * 2 * nc + jnp.minimum(s, 2 * nc - 1)) // 2, (i * 2 * nc + jnp.minimum(s, 2 * nc - 1)) % 2, 0)
    any_spec = pl.BlockSpec(memory_space=pl.ANY)
    tiles_per_seq = seq // ROW_TILE
    kw = 2 * KV_WIDTH
    cur = lambda i, s: (i * nc + jnp.minimum(s, nc - 1), 0)
    prev = lambda i, s: (jnp.maximum(i * nc + jnp.minimum(s, nc - 1) - 1, (i // tiles_per_seq) * (seq // BLOCK)), 0)

    def slot(conv_first, merge_first, out_block, hold=False):
        def index(i, s):
            conv = conv_first if hold else conv_first + s
            return 0, jnp.where(s < nc, conv, jnp.where(s < 2 * nc, merge_first + s - nc, W_OUT + out_block))
        return pl.BlockSpec((D_MODEL, CHUNK), index)

    lo = lambda i, s: (0, jnp.minimum(s, nc - 1))
    const = lambda i, s: (0, 0)
    row = lambda i, s: (i, 0)
    return pl.pallas_call(
        functools.partial(_tail_kernel, tiles_per_seq=tiles_per_seq),
        grid=(n // ROW_TILE, TAIL_STEPS),
        in_specs=[
            pl.BlockSpec(memory_space=pltpu.SMEM),
            pl.BlockSpec((ROW_TILE, D_MODEL), row),
            pl.BlockSpec((ROW_TILE, D_MODEL), row),
            pl.BlockSpec((BLOCK, D_MODEL), cur),
            pl.BlockSpec((BLOCK, kw), prev),
            pl.BlockSpec((BLOCK, kw), cur),
            pl.BlockSpec((BLOCK, kw), prev),
            pl.BlockSpec((BLOCK, kw), cur),
            pl.BlockSpec((N_META, 2 * kw), const),
            pl.BlockSpec((1, D_MODEL), const),
            pl.BlockSpec((1, D_MODEL), const),
            pl.BlockSpec((N_META, CHUNK), lo),
            pl.BlockSpec((CONV_K, CHUNK), lo),
            slot(W_C, W_GA, 0),
            slot(W_H, W_GC, 1),
            slot(W_B, W_BA, 2),
            slot(W_BC, W_BC, 3, hold=True),
            pl.BlockSpec((1, D_MODEL), const),
            pl.BlockSpec((1, D_MODEL), const),
            pl.BlockSpec((ROUTE_ROWS, D_MODEL), const),
            pl.BlockSpec((ROUTE_ROWS, LANE), const),
            any_spec, any_spec,
        ],
        out_specs=[
            pl.BlockSpec((ROW_TILE, D_MODEL), row),
            pl.BlockSpec((ROUTE_ROWS, ROW_TILE), lambda i, s: (0, i)),
            pl.BlockSpec((1, HALF_ROWS, D_EXPERT), cast_map),
        ],
        out_shape=[
            jax.ShapeDtypeStruct((n, D_MODEL), F32),
            jax.ShapeDtypeStruct((ROUTE_ROWS, n), F32),
            jax.ShapeDtypeStruct((2 * N_EXPERTS, D_MODEL, D_EXPERT), BF16),
        ],
        scratch_shapes=[
            pltpu.VMEM((ROW_TILE + 8, CHUNK), F32),
            pltpu.VMEM((nc, 8, CHUNK), F32),
            pltpu.VMEM((nc, ROW_TILE, CHUNK), BF16),
            pltpu.VMEM((nc, ROW_TILE, CHUNK), BF16),
            pltpu.VMEM((ROW_TILE, D_MODEL), BF16),
            pltpu.VMEM((2, HALF_ROWS, D_EXPERT), F32),
            pltpu.SemaphoreType.DMA((2,)),
        ],
        compiler_params=_params("arbitrary", "arbitrary", vmem_limit=V7X_VMEM_LIMIT_TAIL),
        name="mixer_tail",
    )(sinks, x2d, hb, q, kd, kd, vd, vd, kvm, g0, b0, um, cw, wall, wall, wall, wall, g1, b1, rw, rb, wg, wu)


RCHUNK = 1024
PBLK = 256


def _route_kernel(lg_ref, e_ref, w_ref, pos_ref, cnt_ref, rank_ref):
    n = lg_ref.shape[1]
    epg = EXPERTS_PER_GROUP

    def route_chunk(c, carry):
        sl = pl.ds(pl.multiple_of(c * RCHUNK, RCHUNK), RCHUNK)
        gl = [lg_ref[gi:gi + 1, sl] for gi in range(N_GROUPS)]
        gmax = functools.reduce(jnp.maximum, gl)
        ge = [jnp.exp(v - gmax) for v in gl]
        gsum = functools.reduce(jnp.add, ge)
        gp = [v / gsum for v in ge]
        p_group = functools.reduce(jnp.maximum, gp)
        gsel = jnp.full_like(p_group, float(N_GROUPS - 1))
        for gi in range(N_GROUPS - 2, -1, -1):
            gsel = jnp.where(gp[gi] >= p_group, float(gi), gsel)
        e_in = lg_ref[EXPERT_ROW0 + (N_GROUPS - 1) * epg:EXPERT_ROW0 + N_GROUPS * epg, sl]
        for gi in range(N_GROUPS - 2, -1, -1):
            e_in = jnp.where(gsel == float(gi),
                             lg_ref[EXPERT_ROW0 + gi * epg:EXPERT_ROW0 + (gi + 1) * epg, sl], e_in)
        ee = jnp.exp(e_in - jnp.max(e_in, axis=0, keepdims=True))
        pe = ee / jnp.sum(ee, axis=0, keepdims=True)
        ridx = lax.broadcasted_iota(jnp.int32, pe.shape, 0).astype(F32)
        p0 = jnp.max(pe, axis=0, keepdims=True)
        i0 = jnp.min(jnp.where(pe >= p0, ridx, float(epg)), axis=0, keepdims=True)
        pe2 = jnp.where(ridx == i0, -1.0, pe)
        p1 = jnp.max(pe2, axis=0, keepdims=True)
        i1 = jnp.min(jnp.where(pe2 >= p1, ridx, float(epg)), axis=0, keepdims=True)
        den = p0 + p1
        e_ref[0:1, sl] = (gsel * epg + i0).astype(jnp.int32)
        e_ref[1:2, sl] = (gsel * epg + i1).astype(jnp.int32)
        w_ref[0:1, sl] = p0 / den * p_group
        w_ref[1:2, sl] = p1 / den * p_group
        return carry

    lax.fori_loop(0, n // RCHUNK, route_chunk, 0)

    eid = lax.broadcasted_iota(jnp.int32, (N_EXPERTS, PBLK), 0)
    before = jnp.where(lax.broadcasted_iota(jnp.int32, (PBLK, PBLK), 0)
                       < lax.broadcasted_iota(jnp.int32, (PBLK, PBLK), 1), 1.0, 0.0).astype(BF16)
    ones = jnp.ones((PBLK, LANE), BF16)
    nblk = n // PBLK

    def onehot(k, blk):
        sl = pl.ds(pl.multiple_of(blk * PBLK, PBLK), PBLK)
        return sl, eid == e_ref[pl.ds(k, 1), sl]

    run = jnp.zeros((N_EXPERTS, LANE), F32)
    for k in range(2):
        def count_block(blk, run, k=k):
            sl, hit = onehot(k, blk)
            hit_bf = jnp.where(hit, 1.0, 0.0).astype(BF16)
            prior = _dot(hit_bf, before) + jnp.concatenate([run] * (PBLK // LANE), axis=1)
            rank_ref[pl.ds(k, 1), sl] = jnp.sum(jnp.where(hit, prior, 0.0), axis=0, keepdims=True)
            return run + _dot(hit_bf, ones)
        run = lax.fori_loop(0, nblk, count_block, run)

    cnt_ref[...] = run.astype(jnp.int32)
    hi = jnp.floor(run * (1.0 / LANE))
    lo = run - hi * LANE
    below = jnp.where(lax.broadcasted_iota(jnp.int32, (N_EXPERTS, N_EXPERTS), 1)
                      < lax.broadcasted_iota(jnp.int32, (N_EXPERTS, N_EXPERTS), 0), 1.0, 0.0).astype(BF16)
    off = _dot(below, hi.astype(BF16)) * LANE + _dot(below, lo.astype(BF16))
    off = jnp.concatenate([off] * (PBLK // LANE), axis=1)

    for k in range(2):
        def place_block(blk, carry, k=k):
            sl, hit = onehot(k, blk)
            base = jnp.sum(jnp.where(hit, off, 0.0), axis=0, keepdims=True)
            pos_ref[pl.ds(k, 1), sl] = (base + rank_ref[pl.ds(k, 1), sl]).astype(jnp.int32)
            return carry
        lax.fori_loop(0, nblk, place_block, 0)


def _route(logits_t):
    n = logits_t.shape[1]
    return pl.pallas_call(
        _route_kernel,
        out_shape=[
            jax.ShapeDtypeStruct((2, n), jnp.int32),
            jax.ShapeDtypeStruct((2, n), F32),
            jax.ShapeDtypeStruct((2, n), jnp.int32),
            jax.ShapeDtypeStruct((N_EXPERTS, LANE), jnp.int32),
        ],
        scratch_shapes=[pltpu.VMEM((2, n), F32)],
        compiler_params=pltpu.CompilerParams(vmem_limit_bytes=V7X_VMEM_LIMIT),
        name="route",
    )(logits_t)


def _row_copy(src_ref, src_row, dst_ref, dst_row, sem):
    return pltpu.make_async_copy(src_ref.at[pl.ds(src_row, 1)], dst_ref.at[pl.ds(dst_row, 1)], sem)


def _dispatch_kernel(pos_ref, h_ref, xs_ref, sem):
    for r in range(DISPATCH_TILE):
        for k in range(2):
            _row_copy(h_ref, r, xs_ref, pos_ref[k, r], sem).start(priority=k)
    for k in range(2):
        pltpu.make_async_copy(h_ref, xs_ref.at[pl.ds(0, DISPATCH_TILE)], sem).wait()


def _dispatch(pos, h1):
    n = h1.shape[0]
    return pl.pallas_call(
        _dispatch_kernel,
        grid=(n // DISPATCH_TILE,),
        in_specs=[
            pl.BlockSpec((2, DISPATCH_TILE), lambda i: (0, i), memory_space=pltpu.SMEM),
            pl.BlockSpec((DISPATCH_TILE, D_MODEL), lambda i: (i, 0)),
        ],
        out_specs=pl.BlockSpec(memory_space=pl.ANY),
        out_shape=jax.ShapeDtypeStruct((2 * n, D_MODEL), F32),
        scratch_shapes=[pltpu.SemaphoreType.DMA],
        compiler_params=_params("arbitrary"),
        name="dispatch",
    )(pos, h1)


W_SLOTS = 3
W_LOOKAHEAD = W_SLOTS - 1
def _moe_kernel(tile_ref, exp_ref, lo_ref, first_ref, slot_ref, ahead_ref, head_ref, n_ref,
                xs_ref, wgu_hbm, wd_hbm, y_ref,
                wg_buf, wu_buf, wd_buf, ytile_ref, sem):
    i = pl.program_id(0)

    def weight_copies(expert, slot):
        return (pltpu.make_async_copy(wgu_hbm.at[expert], wg_buf.at[slot], sem.at[slot, 0]),
                pltpu.make_async_copy(wgu_hbm.at[N_EXPERTS + expert], wu_buf.at[slot], sem.at[slot, 1]),
                pltpu.make_async_copy(wd_hbm.at[expert], wd_buf.at[slot], sem.at[slot, 2]))

    @pl.when(i == 0)
    def _():
        ytile_ref[...] = jnp.zeros_like(ytile_ref)
        for k in range(W_LOOKAHEAD):
            @pl.when(head_ref[k] >= 0)
            def _(k=k):
                for copy in weight_copies(head_ref[k], k):
                    copy.start()

    valid = i < n_ref[0]

    @pl.when(valid & (first_ref[i] == 1))
    def _():
        slot = slot_ref[i]
        for copy in weight_copies(exp_ref[i], slot):
            copy.wait()

        @pl.when(ahead_ref[i] >= 0)
        def _():
            for copy in weight_copies(ahead_ref[i], (slot + W_LOOKAHEAD) % W_SLOTS):
                copy.start()

    @pl.when(valid)
    def _():
        slot = slot_ref[i]
        x = xs_ref[...].astype(BF16)
        g = _dot(x, wg_buf[slot])
        u = _dot(x, wu_buf[slot])
        a = (g * jax.nn.sigmoid(g) * u).astype(BF16)
        y = _dot(a, wd_buf[slot].astype(BF16))
        row = lax.broadcasted_iota(jnp.int32, (MOE_TILE, 1), 0)
        merged = jnp.where(row >= lo_ref[i], y, ytile_ref[...])
        ytile_ref[...] = merged
        y_ref[...] = merged


def _max_items(rows):
    return rows // MOE_TILE + N_EXPERTS - 1


def _moe_experts(items, xs, wgu, wd):
    rows = xs.shape[0]
    tile_map = lambda i, t, *_: (t[i], 0)
    grid_spec = pltpu.PrefetchScalarGridSpec(
        num_scalar_prefetch=len(items),
        grid=(_max_items(rows),),
        in_specs=[
            pl.BlockSpec((MOE_TILE, D_MODEL), tile_map),
            pl.BlockSpec(memory_space=pl.ANY),
            pl.BlockSpec(memory_space=pl.ANY),
        ],
        out_specs=pl.BlockSpec((MOE_TILE, D_MODEL), tile_map),
        scratch_shapes=[
            pltpu.VMEM((W_SLOTS, D_MODEL, D_EXPERT), BF16),
            pltpu.VMEM((W_SLOTS, D_MODEL, D_EXPERT), BF16),
            pltpu.VMEM((W_SLOTS, D_EXPERT, D_MODEL), F32),
            pltpu.VMEM((MOE_TILE, D_MODEL), F32),
            pltpu.SemaphoreType.DMA((W_SLOTS, 3)),
        ],
    )
    return pl.pallas_call(
        _moe_kernel,
        grid_spec=grid_spec,
        out_shape=jax.ShapeDtypeStruct((rows, D_MODEL), F32),
        compiler_params=_params("arbitrary"),
        name="moe_experts",
    )(*items, xs, wgu, wd)


def _work_items(counts, rows):
    i32 = jnp.int32
    off = jnp.cumsum(counts) - counts
    end = off + counts
    first_tile = off // MOE_TILE
    n_e = jnp.where(counts > 0, (end - 1) // MOE_TILE - first_tile + 1, 0)
    item_end = jnp.cumsum(n_e)
    n_items = item_end[-1]
    idx = jnp.minimum(jnp.arange(_max_items(rows), dtype=i32), n_items - 1)
    exp = jnp.sum((item_end[None, :] <= idx[:, None]).astype(i32), axis=1)
    eid = jnp.arange(N_EXPERTS, dtype=i32)
    is_exp = exp[:, None] == eid[None, :]
    at_exp = lambda table: jnp.sum(jnp.where(is_exp, table[None, :], 0), axis=1)
    item0 = at_exp(item_end - n_e)
    tile = at_exp(first_tile) + idx - item0
    lo = jnp.maximum(at_exp(off) - tile * MOE_TILE, 0)
    present = counts > 0
    ordinal = jnp.cumsum(present.astype(i32)) - 1
    is_kth = present[None, :] & (ordinal[None, :] == eid[:, None])
    kth = jnp.where(jnp.any(is_kth, axis=1), jnp.sum(jnp.where(is_kth, eid[None, :], 0), axis=1), -1)
    item_ord = at_exp(ordinal)
    ahead = jnp.sum(jnp.where(item_ord[:, None] + W_LOOKAHEAD == eid[None, :], kth[None, :], 0), axis=1)
    ahead = jnp.where(item_ord + W_LOOKAHEAD < N_EXPERTS, ahead, -1)
    items = (tile, exp, lo, idx == item0, item_ord % W_SLOTS, ahead, kth[:W_LOOKAHEAD], n_items.reshape(1))
    return tuple(v.astype(i32) for v in items)


def _combine_kernel(pos_ref, pos_next_ref, h_ref, wt_ref, g_ref, b_ref, y_ref, o_ref, buf_ref, sem):
    i = pl.program_id(0)
    slot = i % 2

    def gather(tile_pos_ref, into):
        for r in range(COMBINE_TILE):
            for k in range(2):
                _row_copy(y_ref, tile_pos_ref[k, r], buf_ref.at[into, k], r, sem.at[into]).start(priority=k)

    @pl.when(i == 0)
    def _():
        gather(pos_ref, 0)

    for other in range(2):
        @pl.when((i + 1 < pl.num_programs(0)) & (slot == 1 - other))
        def _(other=other):
            gather(pos_next_ref, other)

    for k in range(2):
        pltpu.make_async_copy(y_ref.at[pl.ds(0, COMBINE_TILE)], buf_ref.at[slot, k], sem.at[slot]).wait()
    f = wt_ref[:, 0:1] * buf_ref[slot, 0] + wt_ref[:, 1:2] * buf_ref[slot, 1]
    o_ref[...] = _layer_norm(ALPHA * h_ref[...] + f, g_ref[...], b_ref[...])


def _combine_ln(pos, h1, wt, g, b, y):
    n = h1.shape[0]
    steps = n // COMBINE_TILE
    return pl.pallas_call(
        _combine_kernel,
        grid=(steps,),
        in_specs=[
            pl.BlockSpec((2, COMBINE_TILE), lambda i: (0, i), memory_space=pltpu.SMEM),
            pl.BlockSpec((2, COMBINE_TILE), lambda i: (0, jnp.minimum(i + 1, steps - 1)), memory_space=pltpu.SMEM),
            pl.BlockSpec((COMBINE_TILE, D_MODEL), lambda i: (i, 0)),
            pl.BlockSpec((COMBINE_TILE, 2), lambda i: (i, 0)),
            pl.BlockSpec((1, D_MODEL), lambda i: (0, 0)),
            pl.BlockSpec((1, D_MODEL), lambda i: (0, 0)),
            pl.BlockSpec(memory_space=pl.ANY),
        ],
        out_specs=pl.BlockSpec((COMBINE_TILE, D_MODEL), lambda i: (i, 0)),
        out_shape=jax.ShapeDtypeStruct((n, D_MODEL), F32),
        scratch_shapes=[pltpu.VMEM((2, 2, COMBINE_TILE, D_MODEL), F32), pltpu.SemaphoreType.DMA((2,))],
        compiler_params=_params("arbitrary"),
        name="combine_ln",
    )(pos, pos, h1, wt, g, b, y)


def kernel(x, meta_tokens, ln_in_g, ln_in_b, w_in, sinks, conv_w, w_branch_attn, w_branch_conv, w_out,
           ln1_g, ln1_b, router_group_w, router_group_b, router_expert_w, router_expert_b,
           w_gate, w_up, w_down, ln2_g, ln2_b):
    batch, seq, d = x.shape
    n = batch * seq
    assert d == D_MODEL and seq % ROW_TILE == 0 and w_in.shape[0] == DEPTH == 1
    row = lambda v: v.reshape(1, -1).astype(F32)
    x2d = x.reshape(n, d)
    g0, b0 = row(ln_in_g), row(ln_in_b)

    win = w_in[0]
    wkv = win[:, D_MODEL:D_MODEL + 2 * KV_WIDTH].astype(BF16)
    gap = jnp.zeros((d + 1, EXPERT_ROW0 - N_GROUPS), F32)
    tail = jnp.zeros((d + 1, ROUTE_ROWS - EXPERT_ROW0 - N_EXPERTS), F32)
    group_wb = jnp.concatenate([router_group_w[0], router_group_b[0][None, :]], axis=0)
    expert_wb = jnp.concatenate([router_expert_w[0], router_expert_b[0][None, :]], axis=0)
    rwb = jnp.concatenate([group_wb, gap, expert_wb, tail], axis=1).T
    rw = rwb[:, :d]
    rb = jnp.broadcast_to(rwb[:, d:], (ROUTE_ROWS, LANE))

    hb, q, kd, vd, wall = _ln_qkv(x2d, g0, b0, win, wkv, w_branch_attn[0], w_branch_conv[0], w_out[0])
    kvm, um = _meta_prep(meta_tokens.astype(F32), g0, b0, wkv, wall)
    h1, logits_t, wgu = _mixer_tail(sinks[0].astype(F32), x2d, hb, q, kd, vd, kvm, g0, b0, um, conv_w[0], wall,
                                    row(ln1_g[0]), row(ln1_b[0]), rw.astype(BF16), rb, w_gate[0], w_up[0], seq)
    _, top_w, pos, counts = _route(logits_t)
    xs = _dispatch(pos, h1)
    y = _moe_experts(_work_items(counts[:, 0], xs.shape[0]), xs, wgu, w_down[0])
    out = _combine_ln(pos, h1, top_w.T, row(ln2_g[0]), row(ln2_b[0]), y)
    return out.reshape(batch, seq, d)
```
